```python
import jax, jax.numpy as jnp
from jax import lax
import numpy as np

D_MODEL = 1024
BATCH = 2
SEQ = 8192
DEPTH = 2

GRID_W = 64
CTX_LEN = 256
N_MIXERS = 2
N_POOL_GROUPS = 4
POOL_WINDOWS = (2, 4, 8, 16)
POOL_GROUP_DIM = D_MODEL // N_POOL_GROUPS
HEAD_DIM = 128
N_Q_HEADS = D_MODEL // HEAD_DIM
N_KV_HEADS = 2
Q_PER_KV = N_Q_HEADS // N_KV_HEADS
QKV_DIM = (N_Q_HEADS + 2 * N_KV_HEADS) * HEAD_DIM
ROPE_THETA = 10000.0
Q_BLOCK = 128
D_FF = 4 * D_MODEL
N_MOD = 6
EPS = 1e-6
N_POOL_LAYERS = (DEPTH + 1) // 2
N_ATTN_LAYERS = DEPTH // 2

kernel_name = 'hybrid_pool_attn_dit_block'


def rms_norm(x, g):
    xf = x.astype(jnp.float32)
    y = xf * lax.rsqrt(jnp.mean(xf * xf, axis=-1, keepdims=True) + EPS)
    return (y * g.astype(jnp.float32)).astype(x.dtype)


def ada_mods(cond, w, b):
    m = (jax.nn.silu(cond) @ w + b)[..., None, :]
    return jnp.split(m, N_MOD, axis=-1)


def modulate(h, shift, scale):
    return h * (1.0 + scale) + shift


def pool_mixer(h, w_pool, pool_scale):
    B, L, D = h.shape
    t = jnp.arange(L)
    hf = h.astype(jnp.float32)
    cs = jnp.concatenate([jnp.zeros((B, 1, D), jnp.float32), jnp.cumsum(hf, axis=1)], axis=1)
    outs = []
    for g, w in enumerate(POOL_WINDOWS):
        sl = slice(g * POOL_GROUP_DIM, (g + 1) * POOL_GROUP_DIM)
        lo = jnp.clip(t - w // 2, 0, L)
        hi = jnp.clip(t + w - w // 2, 0, L)
        csg = cs[..., sl]
        cnt = (hi - lo).astype(jnp.float32)[:, None]
        mean = (jnp.take(csg, hi, axis=1) - jnp.take(csg, lo, axis=1)) / cnt
        diff = (mean - hf[..., sl]).astype(h.dtype)
        outs.append(diff @ w_pool[g])
    return jnp.concatenate(outs, axis=-1) * pool_scale


def axial_rope_tables(L):
    rows = L // GRID_W
    row = jnp.broadcast_to(jnp.arange(rows)[:, None], (rows, GRID_W)).reshape(L).astype(jnp.float32)
    col = jnp.broadcast_to(jnp.arange(GRID_W)[None, :], (rows, GRID_W)).reshape(L).astype(jnp.float32)
    half = HEAD_DIM // 2
    inv_freq = jnp.power(jnp.float32(ROPE_THETA), -jnp.arange(0, half, 2, dtype=jnp.float32) / half)
    ang_r = row[:, None] * inv_freq
    ang_c = col[:, None] * inv_freq
    return (jnp.cos(ang_r)[:, None, :], jnp.sin(ang_r)[:, None, :],
            jnp.cos(ang_c)[:, None, :], jnp.sin(ang_c)[:, None, :])


def _rotate(x, cos, sin):
    x1, x2 = jnp.split(x, 2, axis=-1)
    return jnp.concatenate([x1 * cos - x2 * sin, x2 * cos + x1 * sin], axis=-1)


def apply_axial_rope(x, tables):
    cr, sr, cc, sc = tables
    xf = x.astype(jnp.float32)
    half = HEAD_DIM // 2
    out = jnp.concatenate([_rotate(xf[..., :half], cr, sr), _rotate(xf[..., half:], cc, sc)], axis=-1)
    return out.astype(x.dtype)


def qkv_project(h, w_qkv, g_q, g_k):
    B, L, _ = h.shape
    qkv = h @ w_qkv
    q = qkv[..., :N_Q_HEADS * HEAD_DIM].reshape(B, L, N_Q_HEADS, HEAD_DIM)
    k = qkv[..., N_Q_HEADS * HEAD_DIM:(N_Q_HEADS + N_KV_HEADS) * HEAD_DIM].reshape(B, L, N_KV_HEADS, HEAD_DIM)
    v = qkv[..., (N_Q_HEADS + N_KV_HEADS) * HEAD_DIM:].reshape(B, L, N_KV_HEADS, HEAD_DIM)
    return rms_norm(q, g_q), rms_norm(k, g_k), v


def attend(q, k, v):
    s = jnp.einsum('bqhgd,bkhd->bhgqk', q, k).astype(jnp.float32) * (HEAD_DIM ** -0.5)
    p = jax.nn.softmax(s, axis=-1).astype(v.dtype)
    return jnp.einsum('bhgqk,bkhd->bqhgd', p, v)


def attention_mixer(h_lat, h_ctx, w_qkv, g_q, g_k, w_o, need_ctx_out):
    B, L, D = h_lat.shape
    C = h_ctx.shape[1]
    tables = axial_rope_tables(L)
    q_l, k_l, v_l = qkv_project(h_lat, w_qkv, g_q, g_k)
    q_l = apply_axial_rope(q_l, tables)
    k_l = apply_axial_rope(k_l, tables)
    q_c, k_c, v_c = qkv_project(h_ctx, w_qkv, g_q, g_k)
    k_all = jnp.concatenate([k_c, k_l], axis=1)
    v_all = jnp.concatenate([v_c, v_l], axis=1)
    nb = L // Q_BLOCK
    qb = q_l.reshape(B, nb, Q_BLOCK, N_KV_HEADS, Q_PER_KV, HEAD_DIM).transpose(1, 0, 2, 3, 4, 5)
    o_blocks = lax.map(lambda qblk: attend(qblk, k_all, v_all), qb)
    o_lat = o_blocks.transpose(1, 0, 2, 3, 4, 5).reshape(B, L, D) @ w_o
    o_ctx = None
    if need_ctx_out:
        qc = q_c.reshape(B, C, N_KV_HEADS, Q_PER_KV, HEAD_DIM)
        o_ctx = attend(qc, k_c, v_c).reshape(B, C, D) @ w_o
    return o_lat, o_ctx


def sq_relu_mlp(h, w_in, w_out):
    return jnp.square(jax.nn.relu(h @ w_in)) @ w_out


def setup_inputs(seed: int = 0) -> dict:
    key = jax.random.key(seed)
    ks = jax.random.split(key, 20)
    f32 = jnp.float32
    nrm = lambda k, shape: jax.random.normal(k, shape, f32)
    gain = lambda k, shape: 1.0 + 0.05 * nrm(k, shape)
    return {
        'x': nrm(ks[0], (BATCH, SEQ, D_MODEL)),
        'c': nrm(ks[1], (BATCH, D_MODEL)),
        'ctx': nrm(ks[2], (BATCH, CTX_LEN, D_MODEL)),
        'c_ctx': nrm(ks[3], (D_MODEL,)),
        'w_ada': nrm(ks[4], (DEPTH, D_MODEL, N_MOD * D_MODEL)) * (0.5 * D_MODEL ** -0.5),
        'b_ada': 0.01 * nrm(ks[5], (DEPTH, N_MOD * D_MODEL)),
        'g_mix_pre': gain(ks[6], (DEPTH, D_MODEL)),
        'g_mix_post': gain(ks[7], (DEPTH, D_MODEL)),
        'g_mlp_pre': gain(ks[8], (DEPTH, D_MODEL)),
        'g_mlp_post': gain(ks[9], (DEPTH, D_MODEL)),
        'w_pool': nrm(ks[10], (N_POOL_LAYERS, N_POOL_GROUPS, POOL_GROUP_DIM, POOL_GROUP_DIM)) * POOL_GROUP_DIM ** -0.5,
        'pool_scale': 1.0 + 0.1 * nrm(ks[11], (N_POOL_LAYERS, D_MODEL)),
        'w_qkv': nrm(ks[12], (N_ATTN_LAYERS, D_MODEL, QKV_DIM)) * D_MODEL ** -0.5,
        'g_q': gain(ks[13], (N_ATTN_LAYERS, HEAD_DIM)),
        'g_k': gain(ks[14], (N_ATTN_LAYERS, HEAD_DIM)),
        'w_o': nrm(ks[15], (N_ATTN_LAYERS, D_MODEL, D_MODEL)) * D_MODEL ** -0.5,
        'w_mlp_in': nrm(ks[16], (DEPTH, D_MODEL, D_FF)) * D_MODEL ** -0.5,
        'w_mlp_out': nrm(ks[17], (DEPTH, D_FF, D_MODEL)) * D_FF ** -0.5,
    }


def reference(x, c, ctx, c_ctx, w_ada, b_ada, g_mix_pre, g_mix_post, g_mlp_pre, g_mlp_post,
              w_pool, pool_scale, w_qkv, g_q, g_k, w_o, w_mlp_in, w_mlp_out):
    for i in range(DEPTH):
        last = i == DEPTH - 1
        j = i // N_MIXERS
        sh1, sc1, gt1, sh2, sc2, gt2 = ada_mods(c, w_ada[i], b_ada[i])
        csh1, csc1, cgt1, csh2, csc2, cgt2 = ada_mods(c_ctx, w_ada[i], b_ada[i])
        h_lat = modulate(rms_norm(x, g_mix_pre[i]), sh1, sc1)
        h_ctx = modulate(rms_norm(ctx, g_mix_pre[i]), csh1, csc1)
        if i % N_MIXERS == 0:
            y_lat = pool_mixer(h_lat, w_pool[j], pool_scale[j])
            y_ctx = None if last else pool_mixer(h_ctx, w_pool[j], pool_scale[j])
        else:
            y_lat, y_ctx = attention_mixer(h_lat, h_ctx, w_qkv[j], g_q[j], g_k[j], w_o[j], not last)
        x = x + gt1 * rms_norm(y_lat, g_mix_post[i])
        m_lat = sq_relu_mlp(modulate(rms_norm(x, g_mlp_pre[i]), sh2, sc2), w_mlp_in[i], w_mlp_out[i])
        x = x + gt2 * rms_norm(m_lat, g_mlp_post[i])
        if not last:
            ctx = ctx + cgt1 * rms_norm(y_ctx, g_mix_post[i])
            m_ctx = sq_relu_mlp(modulate(rms_norm(ctx, g_mlp_pre[i]), csh2, csc2), w_mlp_in[i], w_mlp_out[i])
            ctx = ctx + cgt2 * rms_norm(m_ctx, g_mlp_post[i])
    return x
```

```python
import functools
import math

import numpy as np
import jax
import jax.numpy as jnp
from jax import lax
from jax.experimental import pallas as pl
from jax.experimental.pallas import tpu as pltpu

D_MODEL = 1024
DEPTH = 2
GRID_W = 64
POOL_WINDOWS = (2, 4, 8, 16)
POOL_GROUP_DIM = D_MODEL // len(POOL_WINDOWS)
POOL_HALO = 8
HEAD_DIM = 128
N_Q_HEADS = D_MODEL // HEAD_DIM
N_KV_HEADS = 2
Q_PER_KV = N_Q_HEADS // N_KV_HEADS
ROPE_THETA = 10000.0
D_FF = 4 * D_MODEL
N_MOD = 6
EPS = 1e-6
MOD_ROWS = 8
Q_SCALE = (HEAD_DIM ** -0.5) * math.log2(math.e)

ROW_TILE = 512
ATTN_TQ = 128
ATTN_TK = 512
FF_CHUNK = 1024
ADA_TN = 1536
VMEM_LIMIT = 56 * 1024 * 1024

f32 = jnp.float32
bf16 = jnp.bfloat16


def _params(*semantics):
    return pltpu.CompilerParams(dimension_semantics=semantics, vmem_limit_bytes=VMEM_LIMIT)


def _rms_scale(x):
    return x * lax.rsqrt(jnp.mean(x * x, axis=-1, keepdims=True) + EPS)


def _mod_slices(mods_ref, first):
    return [mods_ref[:, (first + j) * D_MODEL:(first + j + 1) * D_MODEL] for j in range(3)]


def _ada_kernel(c_ref, w_ref, b_ref, o_ref):
    c = c_ref[...]
    s = c * jax.nn.sigmoid(c)
    o_ref[...] = jnp.dot(s.astype(bf16), w_ref[...].astype(bf16), preferred_element_type=f32) + b_ref[...]


def _ada_mods(cond, w_ada, b_ada):
    n = N_MOD * D_MODEL
    return pl.pallas_call(
        _ada_kernel,
        grid=(DEPTH, n // ADA_TN),
        in_specs=[
            pl.BlockSpec((MOD_ROWS, D_MODEL), lambda i, j: (0, 0)),
            pl.BlockSpec((None, D_MODEL, ADA_TN), lambda i, j: (i, 0, j)),
            pl.BlockSpec((None, 1, ADA_TN), lambda i, j: (i, 0, j)),
        ],
        out_specs=pl.BlockSpec((None, MOD_ROWS, ADA_TN), lambda i, j: (i, 0, j)),
        out_shape=jax.ShapeDtypeStruct((DEPTH, MOD_ROWS, n), f32),
        compiler_params=_params("arbitrary", "arbitrary"),
        name="ada_mods",
    )(cond, w_ada, b_ada.reshape(DEPTH, 1, n))


def _pool_kernel(xp_ref, x_ref, xn_ref, mods_ref, gpre_ref, gpost_ref, ps_ref, w_ref, o_ref, hbuf,
                 *, tile, seq_len):
    i = pl.program_id(1)
    n_tiles = seq_len // tile
    sh, sc, gt = _mod_slices(mods_ref, 0)
    a = gpre_ref[...] * (1.0 + sc)

    def hmod(xv):
        return _rms_scale(xv) * a + sh

    x = x_ref[...]
    hbuf[POOL_HALO:POOL_HALO + tile, :] = hmod(x)
    hbuf[0:POOL_HALO, :] = jnp.where(i > 0, hmod(xp_ref[...]), 0.0)
    hbuf[POOL_HALO + tile:, :] = jnp.where(i < n_tiles - 1, hmod(xn_ref[...]), 0.0)

    t = i * tile + lax.broadcasted_iota(jnp.int32, (tile, 1), 0)
    ys = []
    for g, w in enumerate(POOL_WINDOWS):
        cols = slice(g * POOL_GROUP_DIM, (g + 1) * POOL_GROUP_DIM)
        acc = None
        for d in range(-(w // 2), w - w // 2):
            v = hbuf[POOL_HALO + d:POOL_HALO + d + tile, cols]
            acc = v if acc is None else acc + v
        cnt = (jnp.minimum(t + (w - w // 2), seq_len) - jnp.maximum(t - w // 2, 0)).astype(f32)
        diff = acc / cnt - hbuf[POOL_HALO:POOL_HALO + tile, cols]
        y = jnp.dot(diff.astype(bf16), w_ref[g], preferred_element_type=f32)
        ys.append(y * ps_ref[:, cols])
    y = jnp.concatenate(ys, axis=-1)
    o_ref[...] = x + gt * (_rms_scale(y) * gpost_ref[...])


def _pool_sublayer(x, mods, mod_row, g_pre, g_post, pool_scale, w_pool, tile):
    bsz, seq_len, _ = x.shape
    hb = tile // POOL_HALO
    last_halo = seq_len // POOL_HALO - 1
    row = lambda d: pl.BlockSpec((1, d), lambda b, i: (0, 0))
    return pl.pallas_call(
        functools.partial(_pool_kernel, tile=tile, seq_len=seq_len),
        grid=(bsz, seq_len // tile),
        in_specs=[
            pl.BlockSpec((None, POOL_HALO, D_MODEL), lambda b, i: (b, jnp.maximum(i * hb - 1, 0), 0)),
            pl.BlockSpec((None, tile, D_MODEL), lambda b, i: (b, i, 0)),
            pl.BlockSpec((None, POOL_HALO, D_MODEL), lambda b, i: (b, jnp.minimum((i + 1) * hb, last_halo), 0)),
            pl.BlockSpec((None, 1, N_MOD * D_MODEL), lambda b, i: (mod_row(b), 0, 0)),
            row(D_MODEL), row(D_MODEL), row(D_MODEL),
            pl.BlockSpec(w_pool.shape, lambda b, i: (0, 0, 0)),
        ],
        out_specs=pl.BlockSpec((None, tile, D_MODEL), lambda b, i: (b, i, 0)),
        out_shape=jax.ShapeDtypeStruct(x.shape, f32),
        scratch_shapes=[pltpu.VMEM((tile + 2 * POOL_HALO, D_MODEL), f32)],
        compiler_params=_params("arbitrary", "arbitrary"),
        name="pool_sublayer",
    )(x, x, x, mods, g_pre, g_post, pool_scale, w_pool)


def _mlp_kernel(*refs, has_proj):
    if has_proj:
        x_ref, a_ref, wo_ref, gmix_ref, mods_ref, gpre_ref, gpost_ref, win_ref, wout_ref, o_ref = refs
    else:
        x_ref, mods_ref, gpre_ref, gpost_ref, win_ref, wout_ref, o_ref = refs
    x = x_ref[...]
    if has_proj:
        gt1 = mods_ref[:, 2 * D_MODEL:3 * D_MODEL]
        y = jnp.dot(a_ref[...], wo_ref[...], preferred_element_type=f32)
        x = x + gt1 * (_rms_scale(y) * gmix_ref[...])
    sh, sc, gt = _mod_slices(mods_ref, 3)
    h = (_rms_scale(x) * (gpre_ref[...] * (1.0 + sc)) + sh).astype(bf16)
    acc = None
    for c in range(D_FF // FF_CHUNK):
        cols = slice(c * FF_CHUNK, (c + 1) * FF_CHUNK)
        u = jnp.maximum(jnp.dot(h, win_ref[:, cols], preferred_element_type=f32), 0.0)
        part = jnp.dot((u * u).astype(bf16), wout_ref[cols, :], preferred_element_type=f32)
        acc = part if acc is None else acc + part
    o_ref[...] = x + gt * (_rms_scale(acc) * gpost_ref[...])


def _mlp_sublayer(x, mods, mod_row, g_pre, g_post, w_in, w_out, tile, proj=None):
    bsz, seq_len, _ = x.shape
    tok = pl.BlockSpec((None, tile, D_MODEL), lambda b, i: (b, i, 0))
    row = pl.BlockSpec((1, D_MODEL), lambda b, i: (0, 0))
    whole = lambda w: pl.BlockSpec(w.shape, lambda b, i: (0, 0), pipeline_mode=pl.Buffered(1))
    mods_spec = pl.BlockSpec((None, 1, N_MOD * D_MODEL), lambda b, i: (mod_row(b), 0, 0))
    if proj is None:
        args = (x, mods, g_pre, g_post, w_in, w_out)
        specs = [tok, mods_spec, row, row, whole(w_in), whole(w_out)]
    else:
        attn_out, w_o, g_mix = proj
        args = (x, attn_out, w_o, g_mix, mods, g_pre, g_post, w_in, w_out)
        specs = [tok, tok, whole(w_o), row, mods_spec, row, row, whole(w_in), whole(w_out)]
    return pl.pallas_call(
        functools.partial(_mlp_kernel, has_proj=proj is not None),
        grid=(bsz, seq_len // tile),
        in_specs=specs,
        out_specs=tok,
        out_shape=jax.ShapeDtypeStruct(x.shape, f32),
        compiler_params=_params("arbitrary", "arbitrary"),
        name="proj_mlp_sublayer" if proj is not None else "mlp_sublayer",
    )(*args)


def _rope_tables(seq_len):
    half = HEAD_DIM // 2
    t = np.arange(seq_len)
    inv_freq = np.power(np.float32(ROPE_THETA), -np.arange(0, half, 2, dtype=np.float32) / np.float32(half))
    ang_r = (t // GRID_W).astype(np.float32)[:, None] * inv_freq
    ang_c = (t % GRID_W).astype(np.float32)[:, None] * inv_freq
    zero = np.zeros_like(ang_r)
    cos = np.concatenate([np.cos(ang_r)] * 2 + [np.cos(ang_c)] * 2, axis=-1)
    sin_a = np.concatenate([-np.sin(ang_r), zero, -np.sin(ang_c), zero], axis=-1)
    sin_b = np.concatenate([zero, np.sin(ang_r), zero, np.sin(ang_c)], axis=-1)
    return tuple(jnp.asarray(a, f32) for a in (cos, sin_a, sin_b))


def _qkv_kernel(*refs, n_q, rope):
    x_ref, mods_ref, gpre_ref, w_ref, gq_ref, gk_ref = refs[:6]
    refs = refs[6:]
    if rope:
        cos_ref, sina_ref, sinb_ref = refs[:3]
        refs = refs[3:]
    if n_q:
        q_ref, k_ref, v_ref = refs
    else:
        k_ref, v_ref = refs
    sh, sc, _ = _mod_slices(mods_ref, 0)
    h = (_rms_scale(x_ref[...]) * (gpre_ref[...] * (1.0 + sc)) + sh).astype(bf16)
    qkv = jnp.dot(h, w_ref[...], preferred_element_type=f32)
    quarter = HEAD_DIM // 4
    for hh in range(n_q + N_KV_HEADS):
        z = _rms_scale(qkv[:, hh * HEAD_DIM:(hh + 1) * HEAD_DIM]) * (gq_ref[...] if hh < n_q else gk_ref[...])
        if rope:
            z = (z * cos_ref[...] + pltpu.roll(z, HEAD_DIM - quarter, 1) * sina_ref[...]
                 + pltpu.roll(z, quarter, 1) * sinb_ref[...])
        if hh < n_q:
            q_ref[hh] = (z * Q_SCALE).astype(bf16)
        else:
            k_ref[hh - n_q] = z.astype(bf16)
    for hh in range(N_KV_HEADS):
        c0 = (n_q + N_KV_HEADS + hh) * HEAD_DIM
        v_ref[hh] = qkv[:, c0:c0 + HEAD_DIM].astype(bf16)


def _qkv_project(x, mods, mod_row, g_pre, w, g_q, g_k, tile, n_q, rope):
    bsz, seq_len, _ = x.shape
    row = lambda d: pl.BlockSpec((1, d), lambda b, i: (0, 0))
    heads = lambda n: pl.BlockSpec((None, n, tile, HEAD_DIM), lambda b, i: (b, 0, i, 0))
    args = [x, mods, g_pre, w, g_q, g_k]
    specs = [
        pl.BlockSpec((None, tile, D_MODEL), lambda b, i: (b, i, 0)),
        pl.BlockSpec((None, 1, N_MOD * D_MODEL), lambda b, i: (mod_row(b), 0, 0)),
        row(D_MODEL),
        pl.BlockSpec(w.shape, lambda b, i: (0, 0), pipeline_mode=pl.Buffered(1)),
        row(HEAD_DIM), row(HEAD_DIM),
    ]
    if rope:
        args += list(_rope_tables(seq_len))
        specs += [pl.BlockSpec((tile, HEAD_DIM), lambda b, i: (i, 0))] * 3
    kv_shape = jax.ShapeDtypeStruct((bsz, N_KV_HEADS, seq_len, HEAD_DIM), bf16)
    out_shape, out_specs = [kv_shape, kv_shape], [heads(N_KV_HEADS), heads(N_KV_HEADS)]
    if n_q:
        out_shape.insert(0, jax.ShapeDtypeStruct((bsz, n_q, seq_len, HEAD_DIM), bf16))
        out_specs.insert(0, heads(n_q))
    return pl.pallas_call(
        functools.partial(_qkv_kernel, n_q=n_q, rope=rope),
        grid=(bsz, seq_len // tile),
        in_specs=specs,
        out_specs=out_specs,
        out_shape=out_shape,
        compiler_params=_params("arbitrary", "arbitrary"),
        name="qkv_project" if n_q else "kv_project",
    )(*args)


def _attn_kernel(q_ref, kc_ref, vc_ref, kl_ref, vl_ref, o_ref, *, tq, tk):
    rows = Q_PER_KV * tq
    q = q_ref[...].reshape(rows, HEAD_DIM)

    def step(k, v, carry):
        m, l, acc = carry
        s = lax.dot_general(q, k, (((1,), (1,)), ((), ())), preferred_element_type=f32)
        m_new = jnp.maximum(m, jnp.max(s, axis=-1, keepdims=True))
        alpha = jnp.exp2(m - m_new)
        p = jnp.exp2(s - m_new)
        l = alpha * l + jnp.sum(p, axis=-1, keepdims=True)
        acc = alpha * acc + jnp.dot(p.astype(bf16), v, preferred_element_type=f32)
        return m_new, l, acc

    carry = (jnp.full((rows, 1), -jnp.inf, f32), jnp.zeros((rows, 1), f32), jnp.zeros((rows, HEAD_DIM), f32))
    carry = step(kc_ref[...], vc_ref[...], carry)

    def body(j, carry):
        off = pl.multiple_of(j * tk, tk)
        return step(kl_ref[pl.ds(off, tk), :], vl_ref[pl.ds(off, tk), :], carry)

    _, l, acc = lax.fori_loop(0, kl_ref.shape[0] // tk, body, carry)
    out = (acc / l).astype(bf16)
    for g in range(Q_PER_KV):
        o_ref[:, g * HEAD_DIM:(g + 1) * HEAD_DIM] = out[g * tq:(g + 1) * tq]


def _attention(q, k_ctx, v_ctx, k_lat, v_lat):
    bsz, _, seq_len, _ = q.shape
    ctx_len = k_ctx.shape[2]
    kv = lambda n: pl.BlockSpec((None, None, n, HEAD_DIM), lambda b, h, i: (b, h, 0, 0))
    return pl.pallas_call(
        functools.partial(_attn_kernel, tq=ATTN_TQ, tk=ATTN_TK),
        grid=(bsz, N_KV_HEADS, seq_len // ATTN_TQ),
        in_specs=[
            pl.BlockSpec((None, Q_PER_KV, ATTN_TQ, HEAD_DIM), lambda b, h, i: (b, h, i, 0)),
            kv(ctx_len), kv(ctx_len), kv(seq_len), kv(seq_len),
        ],
        out_specs=pl.BlockSpec((None, ATTN_TQ, Q_PER_KV * HEAD_DIM), lambda b, h, i: (b, i, h)),
        out_shape=jax.ShapeDtypeStruct((bsz, seq_len, N_Q_HEADS * HEAD_DIM), bf16),
        compiler_params=_params("arbitrary", "arbitrary", "arbitrary"),
        name="attention",
    )(q, k_ctx, v_ctx, k_lat, v_lat)


def kernel(x, c, ctx, c_ctx, w_ada, b_ada, g_mix_pre, g_mix_post, g_mlp_pre, g_mlp_post, w_pool, pool_scale,
           w_qkv, g_q, g_k, w_o, w_mlp_in, w_mlp_out):
    bsz, seq_len, d = x.shape
    ctx_len = ctx.shape[1]
    assert d == D_MODEL and seq_len % ROW_TILE == 0 and bsz + 1 <= MOD_ROWS
    assert ctx_len % POOL_HALO == 0 and (bsz * ctx_len) % ROW_TILE == 0
    row = lambda v: v.reshape(1, -1)

    cond = jnp.zeros((MOD_ROWS, D_MODEL), f32).at[:bsz].set(c).at[bsz].set(c_ctx)
    mods = _ada_mods(cond, w_ada, b_ada).reshape(DEPTH, MOD_ROWS, 1, N_MOD * D_MODEL)
    lat_row = lambda b: b
    ctx_row = lambda b: bsz

    w_pool0 = w_pool[0].astype(bf16)
    w_in0, w_out0 = w_mlp_in[0].astype(bf16), w_mlp_out[0].astype(bf16)
    pool = functools.partial(_pool_sublayer, mods=mods[0], g_pre=row(g_mix_pre[0]), g_post=row(g_mix_post[0]),
                             pool_scale=row(pool_scale[0]), w_pool=w_pool0)
    mlp0 = functools.partial(_mlp_sublayer, mods=mods[0], g_pre=row(g_mlp_pre[0]), g_post=row(g_mlp_post[0]),
                             w_in=w_in0, w_out=w_out0, tile=ROW_TILE)
    x = mlp0(pool(x, mod_row=lat_row, tile=ROW_TILE), mod_row=lat_row)
    ctx = pool(ctx, mod_row=ctx_row, tile=ctx_len)
    ctx = mlp0(ctx.reshape(1, bsz * ctx_len, D_MODEL), mod_row=ctx_row).reshape(bsz, ctx_len, D_MODEL)

    w_qkv1 = w_qkv[0].astype(bf16)
    qkv = functools.partial(_qkv_project, mods=mods[1], g_pre=row(g_mix_pre[1]), g_q=row(g_q[0]), g_k=row(g_k[0]))
    q, k_lat, v_lat = qkv(x, mod_row=lat_row, w=w_qkv1, tile=ROW_TILE, n_q=N_Q_HEADS, rope=True)
    k_ctx, v_ctx = qkv(ctx, mod_row=ctx_row, w=w_qkv1[:, N_Q_HEADS * HEAD_DIM:], tile=ctx_len, n_q=0, rope=False)
    attn_out = _attention(q, k_ctx, v_ctx, k_lat, v_lat)
    return _mlp_sublayer(x, mods[1], lat_row, row(g_mlp_pre[1]), row(g_mlp_post[1]),
                         w_mlp_in[1].astype(bf16), w_mlp_out[1].astype(bf16), ROW_TILE,
                         proj=(attn_out, w_o[0].astype(bf16), row(g_mix_post[1])))
```

```python
import functools
import math

import numpy as np
import jax
import jax.numpy as jnp
from jax import lax
from jax.experimental import pallas as pl
from jax.experimental.pallas import tpu as pltpu

D_MODEL = 1024
DEPTH = 2
GRID_W = 64
POOL_WINDOWS = (2, 4, 8, 16)
POOL_GROUP_DIM = D_MODEL // len(POOL_WINDOWS)
POOL_HALO = 8
HEAD_DIM = 128
N_Q_HEADS = D_MODEL // HEAD_DIM
N_KV_HEADS = 2
Q_PER_KV = N_Q_HEADS // N_KV_HEADS
ROPE_THETA = 10000.0
D_FF = 4 * D_MODEL
N_MOD = 6
EPS = 1e-6
MOD_ROWS = 8
Q_SCALE = (HEAD_DIM ** -0.5) * math.log2(math.e)

ROW_TILE = 512
ATTN_TQ = 128
ATTN_TK = 512
ATTN_TK_BOUNDED = 256
SCORE_LOG2_LIMIT = 100.0
FF_CHUNK = 1024
ADA_TN = 1536
VMEM_LIMIT = 56 * 1024 * 1024

f32 = jnp.float32
bf16 = jnp.bfloat16


def _params(*semantics):
    return pltpu.CompilerParams(dimension_semantics=semantics, vmem_limit_bytes=VMEM_LIMIT)


def _rms_scale(x):
    return x * lax.rsqrt(jnp.mean(x * x, axis=-1, keepdims=True) + EPS)


def _mod_slices(mods_ref, first):
    return [mods_ref[:, (first + j) * D_MODEL:(first + j + 1) * D_MODEL] for j in range(3)]


def _ada_kernel(c_ref, w_ref, b_ref, o_ref):
    c = c_ref[...]
    s = c * jax.nn.sigmoid(c)
    o_ref[...] = jnp.dot(s.astype(bf16), w_ref[...].astype(bf16), preferred_element_type=f32) + b_ref[...]


def _ada_mods(cond, w_ada, b_ada):
    n = N_MOD * D_MODEL
    return pl.pallas_call(
        _ada_kernel,
        grid=(DEPTH, n // ADA_TN),
        in_specs=[
            pl.BlockSpec((MOD_ROWS, D_MODEL), lambda i, j: (0, 0)),
            pl.BlockSpec((None, D_MODEL, ADA_TN), lambda i, j: (i, 0, j)),
            pl.BlockSpec((None, 1, ADA_TN), lambda i, j: (i, 0, j)),
        ],
        out_specs=pl.BlockSpec((None, MOD_ROWS, ADA_TN), lambda i, j: (i, 0, j)),
        out_shape=jax.ShapeDtypeStruct((DEPTH, MOD_ROWS, n), f32),
        compiler_params=_params("arbitrary", "arbitrary"),
        name="ada_mods",
    )(cond, w_ada, b_ada.reshape(DEPTH, 1, n))


def _pool_kernel(xp_ref, x_ref, xn_ref, mods_ref, gpre_ref, gpost_ref, ps_ref, w_ref, o_ref, hbuf,
                 *, tile, seq_len):
    i = pl.program_id(1)
    n_tiles = seq_len // tile
    sh, sc, gt = _mod_slices(mods_ref, 0)
    a = gpre_ref[...] * (1.0 + sc)

    def hmod(xv):
        return _rms_scale(xv) * a + sh

    x = x_ref[...]
    hbuf[POOL_HALO:POOL_HALO + tile, :] = hmod(x)
    hbuf[0:POOL_HALO, :] = jnp.where(i > 0, hmod(xp_ref[...]), 0.0)
    hbuf[POOL_HALO + tile:, :] = jnp.where(i < n_tiles - 1, hmod(xn_ref[...]), 0.0)

    t = i * tile + lax.broadcasted_iota(jnp.int32, (tile, 1), 0)
    ys = []
    for g, w in enumerate(POOL_WINDOWS):
        cols = slice(g * POOL_GROUP_DIM, (g + 1) * POOL_GROUP_DIM)
        acc = None
        for d in range(-(w // 2), w - w // 2):
            v = hbuf[POOL_HALO + d:POOL_HALO + d + tile, cols]
            acc = v if acc is None else acc + v
        cnt = (jnp.minimum(t + (w - w // 2), seq_len) - jnp.maximum(t - w // 2, 0)).astype(f32)
        diff = acc / cnt - hbuf[POOL_HALO:POOL_HALO + tile, cols]
        y = jnp.dot(diff.astype(bf16), w_ref[g], preferred_element_type=f32)
        ys.append(y * ps_ref[:, cols])
    y = jnp.concatenate(ys, axis=-1)
    o_ref[...] = x + gt * (_rms_scale(y) * gpost_ref[...])


def _pool_sublayer(x, mods, mod_row, g_pre, g_post, pool_scale, w_pool, tile):
    bsz, seq_len, _ = x.shape
    hb = tile // POOL_HALO
    last_halo = seq_len // POOL_HALO - 1
    row = lambda d: pl.BlockSpec((1, d), lambda b, i: (0, 0))
    return pl.pallas_call(
        functools.partial(_pool_kernel, tile=tile, seq_len=seq_len),
        grid=(bsz, seq_len // tile),
        in_specs=[
            pl.BlockSpec((None, POOL_HALO, D_MODEL), lambda b, i: (b, jnp.maximum(i * hb - 1, 0), 0)),
            pl.BlockSpec((None, tile, D_MODEL), lambda b, i: (b, i, 0)),
            pl.BlockSpec((None, POOL_HALO, D_MODEL), lambda b, i: (b, jnp.minimum((i + 1) * hb, last_halo), 0)),
            pl.BlockSpec((None, 1, N_MOD * D_MODEL), lambda b, i: (mod_row(b), 0, 0)),
            row(D_MODEL), row(D_MODEL), row(D_MODEL),
            pl.BlockSpec(w_pool.shape, lambda b, i: (0, 0, 0)),
        ],
        out_specs=pl.BlockSpec((None, tile, D_MODEL), lambda b, i: (b, i, 0)),
        out_shape=jax.ShapeDtypeStruct(x.shape, f32),
        scratch_shapes=[pltpu.VMEM((tile + 2 * POOL_HALO, D_MODEL), f32)],
        compiler_params=_params("arbitrary", "arbitrary"),
        name="pool_sublayer",
    )(x, x, x, mods, g_pre, g_post, pool_scale, w_pool)


def _mlp_kernel(*refs, has_proj):
    if has_proj:
        x_ref, a_ref, wo_ref, gmix_ref, mods_ref, gpre_ref, gpost_ref, win_ref, wout_ref, o_ref = refs
    else:
        x_ref, mods_ref, gpre_ref, gpost_ref, win_ref, wout_ref, o_ref = refs
    x = x_ref[...]
    if has_proj:
        gt1 = mods_ref[:, 2 * D_MODEL:3 * D_MODEL]
        y = jnp.dot(a_ref[...], wo_ref[...], preferred_element_type=f32)
        x = x + gt1 * (_rms_scale(y) * gmix_ref[...])
    sh, sc, gt = _mod_slices(mods_ref, 3)
    h = (_rms_scale(x) * (gpre_ref[...] * (1.0 + sc)) + sh).astype(bf16)
    acc = None
    for c in range(D_FF // FF_CHUNK):
        cols = slice(c * FF_CHUNK, (c + 1) * FF_CHUNK)
        u = jnp.maximum(jnp.dot(h, win_ref[:, cols], preferred_element_type=f32), 0.0)
        part = jnp.dot((u * u).astype(bf16), wout_ref[cols, :], preferred_element_type=f32)
        acc = part if acc is None else acc + part
    o_ref[...] = x + gt * (_rms_scale(acc) * gpost_ref[...])


def _mlp_sublayer(x, mods, mod_row, g_pre, g_post, w_in, w_out, tile, proj=None):
    bsz, seq_len, _ = x.shape
    tok = pl.BlockSpec((None, tile, D_MODEL), lambda b, i: (b, i, 0))
    row = pl.BlockSpec((1, D_MODEL), lambda b, i: (0, 0))
    whole = lambda w: pl.BlockSpec(w.shape, lambda b, i: (0, 0), pipeline_mode=pl.Buffered(1))
    mods_spec = pl.BlockSpec((None, 1, N_MOD * D_MODEL), lambda b, i: (mod_row(b), 0, 0))
    if proj is None:
        args = (x, mods, g_pre, g_post, w_in, w_out)
        specs = [tok, mods_spec, row, row, whole(w_in), whole(w_out)]
    else:
        attn_out, w_o, g_mix = proj
        args = (x, attn_out, w_o, g_mix, mods, g_pre, g_post, w_in, w_out)
        specs = [tok, tok, whole(w_o), row, mods_spec, row, row, whole(w_in), whole(w_out)]
    return pl.pallas_call(
        functools.partial(_mlp_kernel, has_proj=proj is not None),
        grid=(bsz, seq_len // tile),
        in_specs=specs,
        out_specs=tok,
        out_shape=jax.ShapeDtypeStruct(x.shape, f32),
        compiler_params=_params("arbitrary", "arbitrary"),
        name="proj_mlp_sublayer" if proj is not None else "mlp_sublayer",
    )(*args)


def _rope_tables(seq_len):
    half = HEAD_DIM // 2
    t = np.arange(seq_len)
    inv_freq = np.power(np.float32(ROPE_THETA), -np.arange(0, half, 2, dtype=np.float32) / np.float32(half))
    ang_r = (t // GRID_W).astype(np.float32)[:, None] * inv_freq
    ang_c = (t % GRID_W).astype(np.float32)[:, None] * inv_freq
    zero = np.zeros_like(ang_r)
    cos = np.concatenate([np.cos(ang_r)] * 2 + [np.cos(ang_c)] * 2, axis=-1)
    sin_a = np.concatenate([-np.sin(ang_r), zero, -np.sin(ang_c), zero], axis=-1)
    sin_b = np.concatenate([zero, np.sin(ang_r), zero, np.sin(ang_c)], axis=-1)
    return tuple(jnp.asarray(a, f32) for a in (cos, sin_a, sin_b))


def _qkv_kernel(*refs, n_q, rope):
    x_ref, mods_ref, gpre_ref, w_ref, gq_ref, gk_ref = refs[:6]
    refs = refs[6:]
    if rope:
        cos_ref, sina_ref, sinb_ref = refs[:3]
        refs = refs[3:]
    if n_q:
        q_ref, k_ref, v_ref = refs
    else:
        k_ref, v_ref = refs
    sh, sc, _ = _mod_slices(mods_ref, 0)
    h = (_rms_scale(x_ref[...]) * (gpre_ref[...] * (1.0 + sc)) + sh).astype(bf16)
    qkv = jnp.dot(h, w_ref[...], preferred_element_type=f32)
    quarter = HEAD_DIM // 4
    for hh in range(n_q + N_KV_HEADS):
        z = _rms_scale(qkv[:, hh * HEAD_DIM:(hh + 1) * HEAD_DIM]) * (gq_ref[...] if hh < n_q else gk_ref[...])
        if rope:
            z = (z * cos_ref[...] + pltpu.roll(z, HEAD_DIM - quarter, 1) * sina_ref[...]
                 + pltpu.roll(z, quarter, 1) * sinb_ref[...])
        if hh < n_q:
            q_ref[hh] = (z * Q_SCALE).astype(bf16)
        else:
            k_ref[hh - n_q] = z.astype(bf16)
    for hh in range(N_KV_HEADS):
        c0 = (n_q + N_KV_HEADS + hh) * HEAD_DIM
        v_ref[hh] = qkv[:, c0:c0 + HEAD_DIM].astype(bf16)


def _qkv_project(x, mods, mod_row, g_pre, w, g_q, g_k, tile, n_q, rope):
    bsz, seq_len, _ = x.shape
    row = lambda d: pl.BlockSpec((1, d), lambda b, i: (0, 0))
    heads = lambda n: pl.BlockSpec((None, n, tile, HEAD_DIM), lambda b, i: (b, 0, i, 0))
    args = [x, mods, g_pre, w, g_q, g_k]
    specs = [
        pl.BlockSpec((None, tile, D_MODEL), lambda b, i: (b, i, 0)),
        pl.BlockSpec((None, 1, N_MOD * D_MODEL), lambda b, i: (mod_row(b), 0, 0)),
        row(D_MODEL),
        pl.BlockSpec(w.shape, lambda b, i: (0, 0), pipeline_mode=pl.Buffered(1)),
        row(HEAD_DIM), row(HEAD_DIM),
    ]
    if rope:
        args += list(_rope_tables(seq_len))
        specs += [pl.BlockSpec((tile, HEAD_DIM), lambda b, i: (i, 0))] * 3
    kv_shape = jax.ShapeDtypeStruct((bsz, N_KV_HEADS, seq_len, HEAD_DIM), bf16)
    out_shape, out_specs = [kv_shape, kv_shape], [heads(N_KV_HEADS), heads(N_KV_HEADS)]
    if n_q:
        out_shape.insert(0, jax.ShapeDtypeStruct((bsz, n_q, seq_len, HEAD_DIM), bf16))
        out_specs.insert(0, heads(n_q))
    return pl.pallas_call(
        functools.partial(_qkv_kernel, n_q=n_q, rope=rope),
        grid=(bsz, seq_len // tile),
        in_specs=specs,
        out_specs=out_specs,
        out_shape=out_shape,
        compiler_params=_params("arbitrary", "arbitrary"),
        name="qkv_project" if n_q else "kv_project",
    )(*args)


def _attn_kernel(q_ref, kc_ref, vc_ref, kl_ref, vl_ref, o_ref, *, tq, tk):
    rows = Q_PER_KV * tq
    q = q_ref[...].reshape(rows, HEAD_DIM)

    def step(k, v, carry):
        m, l, acc = carry
        s = lax.dot_general(q, k, (((1,), (1,)), ((), ())), preferred_element_type=f32)
        m_new = jnp.maximum(m, jnp.max(s, axis=-1, keepdims=True))
        alpha = jnp.exp2(m - m_new)
        p = jnp.exp2(s - m_new)
        l = alpha * l + jnp.sum(p, axis=-1, keepdims=True)
        acc = alpha * acc + jnp.dot(p.astype(bf16), v, preferred_element_type=f32)
        return m_new, l, acc

    carry = (jnp.full((rows, 1), -jnp.inf, f32), jnp.zeros((rows, 1), f32), jnp.zeros((rows, HEAD_DIM), f32))
    carry = step(kc_ref[...], vc_ref[...], carry)

    def body(j, carry):
        off = pl.multiple_of(j * tk, tk)
        return step(kl_ref[pl.ds(off, tk), :], vl_ref[pl.ds(off, tk), :], carry)

    _, l, acc = lax.fori_loop(0, kl_ref.shape[0] // tk, body, carry)
    out = (acc / l).astype(bf16)
    for g in range(Q_PER_KV):
        o_ref[:, g * HEAD_DIM:(g + 1) * HEAD_DIM] = out[g * tq:(g + 1) * tq]


def _attn_bounded_kernel(q_ref, kc_ref, vc_ref, kl_ref, vl_ref, o_ref, *, tq, tk):
    rows = Q_PER_KV * tq
    q = q_ref[...].reshape(rows, HEAD_DIM)
    chunks = [(kc_ref, vc_ref, c0) for c0 in range(0, kc_ref.shape[0], tk)]
    chunks += [(kl_ref, vl_ref, c0) for c0 in range(0, kl_ref.shape[0], tk)]
    acc = None
    lsum = None
    for k_ref, v_ref, c0 in chunks:
        s = lax.dot_general(q, k_ref[c0:c0 + tk, :], (((1,), (1,)), ((), ())), preferred_element_type=f32)
        p = jnp.exp2(s)
        part = p[:, 0:HEAD_DIM]
        for c in range(1, tk // HEAD_DIM):
            part = part + p[:, c * HEAD_DIM:(c + 1) * HEAD_DIM]
        lsum = part if lsum is None else lsum + part
        pv = jnp.dot(p.astype(bf16), v_ref[c0:c0 + tk, :], preferred_element_type=f32)
        acc = pv if acc is None else acc + pv
    out = (acc / jnp.sum(lsum, axis=-1, keepdims=True)).astype(bf16)
    for g in range(Q_PER_KV):
        o_ref[:, g * HEAD_DIM:(g + 1) * HEAD_DIM] = out[g * tq:(g + 1) * tq]


def _attention(q, k_ctx, v_ctx, k_lat, v_lat, bounded):
    bsz, _, seq_len, _ = q.shape
    ctx_len = k_ctx.shape[2]
    kv = lambda n: pl.BlockSpec((None, None, n, HEAD_DIM), lambda b, h, i: (b, h, 0, 0))
    body = (functools.partial(_attn_bounded_kernel, tq=ATTN_TQ, tk=ATTN_TK_BOUNDED) if bounded
            else functools.partial(_attn_kernel, tq=ATTN_TQ, tk=ATTN_TK))
    return pl.pallas_call(
        body,
        grid=(bsz, N_KV_HEADS, seq_len // ATTN_TQ),
        in_specs=[
            pl.BlockSpec((None, Q_PER_KV, ATTN_TQ, HEAD_DIM), lambda b, h, i: (b, h, i, 0)),
            kv(ctx_len), kv(ctx_len), kv(seq_len), kv(seq_len),
        ],
        out_specs=pl.BlockSpec((None, ATTN_TQ, Q_PER_KV * HEAD_DIM), lambda b, h, i: (b, i, h)),
        out_shape=jax.ShapeDtypeStruct((bsz, seq_len, N_Q_HEADS * HEAD_DIM), bf16),
        compiler_params=_params("arbitrary", "arbitrary", "arbitrary"),
        name="attention_bounded" if bounded else "attention",
    )(q, k_ctx, v_ctx, k_lat, v_lat)


def kernel(x, c, ctx, c_ctx, w_ada, b_ada, g_mix_pre, g_mix_post, g_mlp_pre, g_mlp_post, w_pool, pool_scale,
           w_qkv, g_q, g_k, w_o, w_mlp_in, w_mlp_out):
    bsz, seq_len, d = x.shape
    ctx_len = ctx.shape[1]
    assert d == D_MODEL and seq_len % ROW_TILE == 0 and bsz + 1 <= MOD_ROWS
    assert ctx_len % POOL_HALO == 0 and (bsz * ctx_len) % ROW_TILE == 0
    row = lambda v: v.reshape(1, -1)

    cond = jnp.zeros((MOD_ROWS, D_MODEL), f32).at[:bsz].set(c).at[bsz].set(c_ctx)
    mods = _ada_mods(cond, w_ada, b_ada).reshape(DEPTH, MOD_ROWS, 1, N_MOD * D_MODEL)
    lat_row = lambda b: b
    ctx_row = lambda b: bsz

    w_pool0 = w_pool[0].astype(bf16)
    w_in0, w_out0 = w_mlp_in[0].astype(bf16), w_mlp_out[0].astype(bf16)
    pool = functools.partial(_pool_sublayer, mods=mods[0], g_pre=row(g_mix_pre[0]), g_post=row(g_mix_post[0]),
                             pool_scale=row(pool_scale[0]), w_pool=w_pool0)
    mlp0 = functools.partial(_mlp_sublayer, mods=mods[0], g_pre=row(g_mlp_pre[0]), g_post=row(g_mlp_post[0]),
                             w_in=w_in0, w_out=w_out0, tile=ROW_TILE)
    x = mlp0(pool(x, mod_row=lat_row, tile=ROW_TILE), mod_row=lat_row)
    ctx = pool(ctx, mod_row=ctx_row, tile=ctx_len)
    ctx = mlp0(ctx.reshape(1, bsz * ctx_len, D_MODEL), mod_row=ctx_row).reshape(bsz, ctx_len, D_MODEL)

    w_qkv1 = w_qkv[0].astype(bf16)
    qkv = functools.partial(_qkv_project, mods=mods[1], g_pre=row(g_mix_pre[1]), g_q=row(g_q[0]), g_k=row(g_k[0]))
    q, k_lat, v_lat = qkv(x, mod_row=lat_row, w=w_qkv1, tile=ROW_TILE, n_q=N_Q_HEADS, rope=True)
    k_ctx, v_ctx = qkv(ctx, mod_row=ctx_row, w=w_qkv1[:, N_Q_HEADS * HEAD_DIM:], tile=ctx_len, n_q=0, rope=False)
    score_bound = 1.02 * Q_SCALE * HEAD_DIM * jnp.max(jnp.abs(g_q[0])) * jnp.max(jnp.abs(g_k[0]))
    attn_out = lax.cond(score_bound <= SCORE_LOG2_LIMIT,
                        functools.partial(_attention, bounded=True), functools.partial(_attention, bounded=False),
                        q, k_ctx, v_ctx, k_lat, v_lat)
    return _mlp_sublayer(x, mods[1], lat_row, row(g_mlp_pre[1]), row(g_mlp_post[1]),
                         w_mlp_in[1].astype(bf16), w_mlp_out[1].astype(bf16), ROW_TILE,
                         proj=(attn_out, w_o[0].astype(bf16), row(g_mix_post[1])))
```

```python
import functools
import math

import numpy as np
import jax
import jax.numpy as jnp
from jax import lax
from jax.experimental import pallas as pl
from jax.experimental.pallas import tpu as pltpu

D_MODEL = 1024
DEPTH = 2
GRID_W = 64
POOL_WINDOWS = (2, 4, 8, 16)
POOL_GROUP_DIM = D_MODEL // len(POOL_WINDOWS)
POOL_HALO = 8
HEAD_DIM = 128
N_Q_HEADS = D_MODEL // HEAD_DIM
N_KV_HEADS = 2
V_WIDTH = 2 * HEAD_DIM
Q_PER_KV = N_Q_HEADS // N_KV_HEADS
ROPE_THETA = 10000.0
D_FF = 4 * D_MODEL
N_MOD = 6
EPS = 1e-6
MOD_ROWS = 8
Q_SCALE = (HEAD_DIM ** -0.5) * math.log2(math.e)

ROW_TILE = 512
ATTN_TQ = 128
ATTN_TK = 512
ATTN_TQ_BOUNDED = 512
ATTN_TK_BOUNDED = 256
SCORE_LOG2_LIMIT = 100.0
FF_CHUNK = 1024
ADA_TN = 1536
VMEM_LIMIT = 56 * 1024 * 1024

f32 = jnp.float32
bf16 = jnp.bfloat16


def _params(*semantics):
    return pltpu.CompilerParams(dimension_semantics=semantics, vmem_limit_bytes=VMEM_LIMIT)


def _rms_scale(x):
    return x * lax.rsqrt(jnp.mean(x * x, axis=-1, keepdims=True) + EPS)


def _mod_slices(mods_ref, first):
    return [mods_ref[:, (first + j) * D_MODEL:(first + j + 1) * D_MODEL] for j in range(3)]


def _ada_kernel(c_ref, w_ref, b_ref, o_ref):
    c = c_ref[...]
    s = c * jax.nn.sigmoid(c)
    o_ref[...] = jnp.dot(s.astype(bf16), w_ref[...].astype(bf16), preferred_element_type=f32) + b_ref[...]


def _ada_mods(cond, w_ada, b_ada):
    n = N_MOD * D_MODEL
    return pl.pallas_call(
        _ada_kernel,
        grid=(DEPTH, n // ADA_TN),
        in_specs=[
            pl.BlockSpec((MOD_ROWS, D_MODEL), lambda i, j: (0, 0)),
            pl.BlockSpec((None, D_MODEL, ADA_TN), lambda i, j: (i, 0, j)),
            pl.BlockSpec((None, 1, ADA_TN), lambda i, j: (i, 0, j)),
        ],
        out_specs=pl.BlockSpec((None, MOD_ROWS, ADA_TN), lambda i, j: (i, 0, j)),
        out_shape=jax.ShapeDtypeStruct((DEPTH, MOD_ROWS, n), f32),
        compiler_params=_params("arbitrary", "arbitrary"),
        name="ada_mods",
    )(cond, w_ada, b_ada.reshape(DEPTH, 1, n))


def _pool_kernel(xp_ref, x_ref, xn_ref, mods_ref, gpre_ref, gpost_ref, ps_ref, w_ref, o_ref, hbuf,
                 *, tile, seq_len):
    i = pl.program_id(1)
    n_tiles = seq_len // tile
    sh, sc, gt = _mod_slices(mods_ref, 0)
    a = gpre_ref[...] * (1.0 + sc)

    def hmod(xv):
        return _rms_scale(xv) * a + sh

    x = x_ref[...]
    hbuf[POOL_HALO:POOL_HALO + tile, :] = hmod(x)
    hbuf[0:POOL_HALO, :] = jnp.where(i > 0, hmod(xp_ref[...]), 0.0)
    hbuf[POOL_HALO + tile:, :] = jnp.where(i < n_tiles - 1, hmod(xn_ref[...]), 0.0)

    t = i * tile + lax.broadcasted_iota(jnp.int32, (tile, 1), 0)
    ys = []
    for g, w in enumerate(POOL_WINDOWS):
        cols = slice(g * POOL_GROUP_DIM, (g + 1) * POOL_GROUP_DIM)
        acc = None
        for d in range(-(w // 2), w - w // 2):
            v = hbuf[POOL_HALO + d:POOL_HALO + d + tile, cols]
            acc = v if acc is None else acc + v
        cnt = (jnp.minimum(t + (w - w // 2), seq_len) - jnp.maximum(t - w // 2, 0)).astype(f32)
        diff = acc / cnt - hbuf[POOL_HALO:POOL_HALO + tile, cols]
        y = jnp.dot(diff.astype(bf16), w_ref[g], preferred_element_type=f32)
        ys.append(y * ps_ref[:, cols])
    y = jnp.concatenate(ys, axis=-1)
    o_ref[...] = x + gt * (_rms_scale(y) * gpost_ref[...])


def _pool_sublayer(x, mods, mod_row, g_pre, g_post, pool_scale, w_pool, tile):
    bsz, seq_len, _ = x.shape
    hb = tile // POOL_HALO
    last_halo = seq_len // POOL_HALO - 1
    row = lambda d: pl.BlockSpec((1, d), lambda b, i: (0, 0))
    return pl.pallas_call(
        functools.partial(_pool_kernel, tile=tile, seq_len=seq_len),
        grid=(bsz, seq_len // tile),
        in_specs=[
            pl.BlockSpec((None, POOL_HALO, D_MODEL), lambda b, i: (b, jnp.maximum(i * hb - 1, 0), 0)),
            pl.BlockSpec((None, tile, D_MODEL), lambda b, i: (b, i, 0)),
            pl.BlockSpec((None, POOL_HALO, D_MODEL), lambda b, i: (b, jnp.minimum((i + 1) * hb, last_halo), 0)),
            pl.BlockSpec((None, 1, N_MOD * D_MODEL), lambda b, i: (mod_row(b), 0, 0)),
            row(D_MODEL), row(D_MODEL), row(D_MODEL),
            pl.BlockSpec(w_pool.shape, lambda b, i: (0, 0, 0)),
        ],
        out_specs=pl.BlockSpec((None, tile, D_MODEL), lambda b, i: (b, i, 0)),
        out_shape=jax.ShapeDtypeStruct(x.shape, f32),
        scratch_shapes=[pltpu.VMEM((tile + 2 * POOL_HALO, D_MODEL), f32)],
        compiler_params=_params("arbitrary", "arbitrary"),
        name="pool_sublayer",
    )(x, x, x, mods, g_pre, g_post, pool_scale, w_pool)


def _mlp_kernel(*refs, has_proj):
    if has_proj:
        x_ref, a_ref, wo_ref, gmix_ref, mods_ref, gpre_ref, gpost_ref, win_ref, wout_ref, o_ref = refs
    else:
        x_ref, mods_ref, gpre_ref, gpost_ref, win_ref, wout_ref, o_ref = refs
    x = x_ref[...]
    if has_proj:
        gt1 = mods_ref[:, 2 * D_MODEL:3 * D_MODEL]
        y = jnp.dot(a_ref[...], wo_ref[...], preferred_element_type=f32)
        x = x + gt1 * (_rms_scale(y) * gmix_ref[...])
    sh, sc, gt = _mod_slices(mods_ref, 3)
    h = (_rms_scale(x) * (gpre_ref[...] * (1.0 + sc)) + sh).astype(bf16)
    acc = None
    for c in range(D_FF // FF_CHUNK):
        cols = slice(c * FF_CHUNK, (c + 1) * FF_CHUNK)
        u = jnp.maximum(jnp.dot(h, win_ref[:, cols], preferred_element_type=f32), 0.0)
        part = jnp.dot((u * u).astype(bf16), wout_ref[cols, :], preferred_element_type=f32)
        acc = part if acc is None else acc + part
    o_ref[...] = x + gt * (_rms_scale(acc) * gpost_ref[...])


def _mlp_sublayer(x, mods, mod_row, g_pre, g_post, w_in, w_out, tile, proj=None):
    bsz, seq_len, _ = x.shape
    tok = pl.BlockSpec((None, tile, D_MODEL), lambda b, i: (b, i, 0))
    row = pl.BlockSpec((1, D_MODEL), lambda b, i: (0, 0))
    whole = lambda w: pl.BlockSpec(w.shape, lambda b, i: (0, 0), pipeline_mode=pl.Buffered(1))
    mods_spec = pl.BlockSpec((None, 1, N_MOD * D_MODEL), lambda b, i: (mod_row(b), 0, 0))
    if proj is None:
        args = (x, mods, g_pre, g_post, w_in, w_out)
        specs = [tok, mods_spec, row, row, whole(w_in), whole(w_out)]
    else:
        attn_out, w_o, g_mix = proj
        args = (x, attn_out, w_o, g_mix, mods, g_pre, g_post, w_in, w_out)
        specs = [tok, tok, whole(w_o), row, mods_spec, row, row, whole(w_in), whole(w_out)]
    return pl.pallas_call(
        functools.partial(_mlp_kernel, has_proj=proj is not None),
        grid=(bsz, seq_len // tile),
        in_specs=specs,
        out_specs=tok,
        out_shape=jax.ShapeDtypeStruct(x.shape, f32),
        compiler_params=_params("arbitrary", "arbitrary"),
        name="proj_mlp_sublayer" if proj is not None else "mlp_sublayer",
    )(*args)


def _rope_tables(seq_len):
    half = HEAD_DIM // 2
    t = np.arange(seq_len)
    inv_freq = np.power(np.float32(ROPE_THETA), -np.arange(0, half, 2, dtype=np.float32) / np.float32(half))
    ang_r = (t // GRID_W).astype(np.float32)[:, None] * inv_freq
    ang_c = (t % GRID_W).astype(np.float32)[:, None] * inv_freq
    zero = np.zeros_like(ang_r)
    cos = np.concatenate([np.cos(ang_r)] * 2 + [np.cos(ang_c)] * 2, axis=-1)
    sin_a = np.concatenate([-np.sin(ang_r), zero, -np.sin(ang_c), zero], axis=-1)
    sin_b = np.concatenate([zero, np.sin(ang_r), zero, np.sin(ang_c)], axis=-1)
    return tuple(jnp.asarray(a, f32) for a in (cos, sin_a, sin_b))


def _qkv_kernel(*refs, n_q, rope):
    x_ref, mods_ref, gpre_ref, w_ref, gq_ref, gk_ref = refs[:6]
    refs = refs[6:]
    if rope:
        cos_ref, sina_ref, sinb_ref = refs[:3]
        refs = refs[3:]
    if n_q:
        q_ref, k_ref, v_ref = refs
    else:
        k_ref, v_ref = refs
    sh, sc, _ = _mod_slices(mods_ref, 0)
    h = (_rms_scale(x_ref[...]) * (gpre_ref[...] * (1.0 + sc)) + sh).astype(bf16)
    qkv = jnp.dot(h, w_ref[...], preferred_element_type=f32)
    quarter = HEAD_DIM // 4
    for hh in range(n_q + N_KV_HEADS):
        z = _rms_scale(qkv[:, hh * HEAD_DIM:(hh + 1) * HEAD_DIM]) * (gq_ref[...] if hh < n_q else gk_ref[...])
        if rope:
            z = (z * cos_ref[...] + pltpu.roll(z, HEAD_DIM - quarter, 1) * sina_ref[...]
                 + pltpu.roll(z, quarter, 1) * sinb_ref[...])
        if hh < n_q:
            q_ref[hh] = (z * Q_SCALE).astype(bf16)
        else:
            k_ref[hh - n_q] = z.astype(bf16)
    ones_col = (lax.broadcasted_iota(jnp.int32, (qkv.shape[0], HEAD_DIM), 1) == 0).astype(bf16)
    for hh in range(N_KV_HEADS):
        c0 = (n_q + N_KV_HEADS + hh) * HEAD_DIM
        v_ref[hh, :, 0:HEAD_DIM] = qkv[:, c0:c0 + HEAD_DIM].astype(bf16)
        v_ref[hh, :, HEAD_DIM:] = ones_col


def _qkv_project(x, mods, mod_row, g_pre, w, g_q, g_k, tile, n_q, rope):
    bsz, seq_len, _ = x.shape
    row = lambda d: pl.BlockSpec((1, d), lambda b, i: (0, 0))
    heads = lambda n, d=HEAD_DIM: pl.BlockSpec((None, n, tile, d), lambda b, i: (b, 0, i, 0))
    args = [x, mods, g_pre, w, g_q, g_k]
    specs = [
        pl.BlockSpec((None, tile, D_MODEL), lambda b, i: (b, i, 0)),
        pl.BlockSpec((None, 1, N_MOD * D_MODEL), lambda b, i: (mod_row(b), 0, 0)),
        row(D_MODEL),
        pl.BlockSpec(w.shape, lambda b, i: (0, 0), pipeline_mode=pl.Buffered(1)),
        row(HEAD_DIM), row(HEAD_DIM),
    ]
    if rope:
        args += list(_rope_tables(seq_len))
        specs += [pl.BlockSpec((tile, HEAD_DIM), lambda b, i: (i, 0))] * 3
    k_shape = jax.ShapeDtypeStruct((bsz, N_KV_HEADS, seq_len, HEAD_DIM), bf16)
    v_shape = jax.ShapeDtypeStruct((bsz, N_KV_HEADS, seq_len, V_WIDTH), bf16)
    out_shape, out_specs = [k_shape, v_shape], [heads(N_KV_HEADS), heads(N_KV_HEADS, V_WIDTH)]
    if n_q:
        out_shape.insert(0, jax.ShapeDtypeStruct((bsz, n_q, seq_len, HEAD_DIM), bf16))
        out_specs.insert(0, heads(n_q))
    return pl.pallas_call(
        functools.partial(_qkv_kernel, n_q=n_q, rope=rope),
        grid=(bsz, seq_len // tile),
        in_specs=specs,
        out_specs=out_specs,
        out_shape=out_shape,
        compiler_params=_params("arbitrary", "arbitrary"),
        name="qkv_project" if n_q else "kv_project",
    )(*args)


def _attn_kernel(q_ref, kc_ref, vc_ref, kl_ref, vl_ref, o_ref, *, tq, tk):
    rows = Q_PER_KV * tq
    q = q_ref[...].reshape(rows, HEAD_DIM)

    def step(k, v, carry):
        m, l, acc = carry
        s = lax.dot_general(q, k, (((1,), (1,)), ((), ())), preferred_element_type=f32)
        m_new = jnp.maximum(m, jnp.max(s, axis=-1, keepdims=True))
        alpha = jnp.exp2(m - m_new)
        p = jnp.exp2(s - m_new)
        l = alpha * l + jnp.sum(p, axis=-1, keepdims=True)
        acc = alpha * acc + jnp.dot(p.astype(bf16), v, preferred_element_type=f32)
        return m_new, l, acc

    carry = (jnp.full((rows, 1), -jnp.inf, f32), jnp.zeros((rows, 1), f32), jnp.zeros((rows, HEAD_DIM), f32))
    carry = step(kc_ref[...], vc_ref[:, 0:HEAD_DIM], carry)

    def body(j, carry):
        off = pl.multiple_of(j * tk, tk)
        return step(kl_ref[pl.ds(off, tk), :], vl_ref[pl.ds(off, tk), 0:HEAD_DIM], carry)

    _, l, acc = lax.fori_loop(0, kl_ref.shape[0] // tk, body, carry)
    out = (acc / l).astype(bf16)
    for g in range(Q_PER_KV):
        o_ref[:, g * HEAD_DIM:(g + 1) * HEAD_DIM] = out[g * tq:(g + 1) * tq]


def _attn_bounded_kernel(q_ref, kc_ref, vc_ref, kl_ref, vl_ref, o_ref, *, tq, tk):
    rows = Q_PER_KV * tq
    q = q_ref[...].reshape(rows, HEAD_DIM)
    chunks = [(kc_ref, vc_ref, c0) for c0 in range(0, kc_ref.shape[0], tk)]
    chunks += [(kl_ref, vl_ref, c0) for c0 in range(0, kl_ref.shape[0], tk)]
    acc = None
    for k_ref, v_ref, c0 in chunks:
        s = lax.dot_general(q, k_ref[c0:c0 + tk, :], (((1,), (1,)), ((), ())), preferred_element_type=f32)
        pv = jnp.dot(jnp.exp2(s).astype(bf16), v_ref[c0:c0 + tk, :], preferred_element_type=f32)
        acc = pv if acc is None else acc + pv
    out = (acc[:, 0:HEAD_DIM] / acc[:, HEAD_DIM:HEAD_DIM + 1]).astype(bf16)
    for g in range(Q_PER_KV):
        o_ref[:, g * HEAD_DIM:(g + 1) * HEAD_DIM] = out[g * tq:(g + 1) * tq]


def _attention(q, k_ctx, v_ctx, k_lat, v_lat, bounded):
    bsz, _, seq_len, _ = q.shape
    ctx_len = k_ctx.shape[2]
    kv = lambda n, d: pl.BlockSpec((None, None, n, d), lambda b, h, i: (b, h, 0, 0))
    tq = ATTN_TQ_BOUNDED if bounded else ATTN_TQ
    body = (functools.partial(_attn_bounded_kernel, tq=tq, tk=ATTN_TK_BOUNDED) if bounded
            else functools.partial(_attn_kernel, tq=tq, tk=ATTN_TK))
    return pl.pallas_call(
        body,
        grid=(bsz, N_KV_HEADS, seq_len // tq),
        in_specs=[
            pl.BlockSpec((None, Q_PER_KV, tq, HEAD_DIM), lambda b, h, i: (b, h, i, 0)),
            kv(ctx_len, HEAD_DIM), kv(ctx_len, V_WIDTH), kv(seq_len, HEAD_DIM), kv(seq_len, V_WIDTH),
        ],
        out_specs=pl.BlockSpec((None, tq, Q_PER_KV * HEAD_DIM), lambda b, h, i: (b, i, h)),
        out_shape=jax.ShapeDtypeStruct((bsz, seq_len, N_Q_HEADS * HEAD_DIM), bf16),
        compiler_params=_params("arbitrary", "arbitrary", "arbitrary"),
        name="attention_bounded" if bounded else "attention",
    )(q, k_ctx, v_ctx, k_lat, v_lat)


def kernel(x, c, ctx, c_ctx, w_ada, b_ada, g_mix_pre, g_mix_post, g_mlp_pre, g_mlp_post, w_pool, pool_scale,
           w_qkv, g_q, g_k, w_o, w_mlp_in, w_mlp_out):
    bsz, seq_len, d = x.shape
    ctx_len = ctx.shape[1]
    assert d == D_MODEL and seq_len % ROW_TILE == 0 and bsz + 1 <= MOD_ROWS
    assert ctx_len % POOL_HALO == 0 and (bsz * ctx_len) % ROW_TILE == 0
    row = lambda v: v.reshape(1, -1)

    cond = jnp.zeros((MOD_ROWS, D_MODEL), f32).at[:bsz].set(c).at[bsz].set(c_ctx)
    mods = _ada_mods(cond, w_ada, b_ada).reshape(DEPTH, MOD_ROWS, 1, N_MOD * D_MODEL)
    lat_row = lambda b: b
    ctx_row = lambda b: bsz

    w_pool0 = w_pool[0].astype(bf16)
    w_in0, w_out0 = w_mlp_in[0].astype(bf16), w_mlp_out[0].astype(bf16)
    pool = functools.partial(_pool_sublayer, mods=mods[0], g_pre=row(g_mix_pre[0]), g_post=row(g_mix_post[0]),
                             pool_scale=row(pool_scale[0]), w_pool=w_pool0)
    mlp0 = functools.partial(_mlp_sublayer, mods=mods[0], g_pre=row(g_mlp_pre[0]), g_post=row(g_mlp_post[0]),
                             w_in=w_in0, w_out=w_out0, tile=ROW_TILE)
    x = mlp0(pool(x, mod_row=lat_row, tile=ROW_TILE), mod_row=lat_row)
    ctx = pool(ctx, mod_row=ctx_row, tile=ctx_len)
    ctx = mlp0(ctx.reshape(1, bsz * ctx_len, D_MODEL), mod_row=ctx_row).reshape(bsz, ctx_len, D_MODEL)

    w_qkv1 = w_qkv[0].astype(bf16)
    qkv = functools.partial(_qkv_project, mods=mods[1], g_pre=row(g_mix_pre[1]), g_q=row(g_q[0]), g_k=row(g_k[0]))
    q, k_lat, v_lat = qkv(x, mod_row=lat_row, w=w_qkv1, tile=ROW_TILE, n_q=N_Q_HEADS, rope=True)
    k_ctx, v_ctx = qkv(ctx, mod_row=ctx_row, w=w_qkv1[:, N_Q_HEADS * HEAD_DIM:], tile=ctx_len, n_q=0, rope=False)
    score_bound = 1.02 * Q_SCALE * HEAD_DIM * jnp.max(jnp.abs(g_q[0])) * jnp.max(jnp.abs(g_k[0]))
    attn_out = lax.cond(score_bound <= SCORE_LOG2_LIMIT,
                        functools.partial(_attention, bounded=True), functools.partial(_attention, bounded=False),
                        q, k_ctx, v_ctx, k_lat, v_lat)
    return _mlp_sublayer(x, mods[1], lat_row, row(g_mlp_pre[1]), row(g_mlp_post[1]),
                         w_mlp_in[1].astype(bf16), w_mlp_out[1].astype(bf16), ROW_TILE,
                         proj=(attn_out, w_o[0].astype(bf16), row(g_mix_post[1])))
```

```python
import functools
import math

import numpy as np
import jax
import jax.numpy as jnp
from jax import lax
from jax.experimental import pallas as pl
from jax.experimental.pallas import tpu as pltpu

D_MODEL = 1024
DEPTH = 2
GRID_W = 64
POOL_WINDOWS = (2, 4, 8, 16)
POOL_GROUP_DIM = D_MODEL // len(POOL_WINDOWS)
POOL_HALO = 8
HEAD_DIM = 128
N_Q_HEADS = D_MODEL // HEAD_DIM
N_KV_HEADS = 2
V_WIDTH = 2 * HEAD_DIM
Q_PER_KV = N_Q_HEADS // N_KV_HEADS
ROPE_THETA = 10000.0
D_FF = 4 * D_MODEL
N_MOD = 6
EPS = 1e-6
MOD_ROWS = 8
Q_SCALE = (HEAD_DIM ** -0.5) * math.log2(math.e)

ROW_TILE = 512
ATTN_TQ = 128
ATTN_TK = 512
ATTN_TQ_BOUNDED = 512
ATTN_TK_BOUNDED = 256
SCORE_LOG2_LIMIT = 100.0
FF_CHUNK = 1024
ADA_TN = 1536
VMEM_LIMIT = 56 * 1024 * 1024

f32 = jnp.float32
bf16 = jnp.bfloat16


def _params(*semantics):
    return pltpu.CompilerParams(dimension_semantics=semantics, vmem_limit_bytes=VMEM_LIMIT)


def _rms_scale(x):
    return x * lax.rsqrt(jnp.mean(x * x, axis=-1, keepdims=True) + EPS)


def _mod_slices(mods_ref, first):
    return [mods_ref[:, (first + j) * D_MODEL:(first + j + 1) * D_MODEL] for j in range(3)]


def _ada_kernel(c_ref, w_ref, b_ref, o_ref):
    c = c_ref[...]
    s = c * jax.nn.sigmoid(c)
    o_ref[...] = jnp.dot(s.astype(bf16), w_ref[...].astype(bf16), preferred_element_type=f32) + b_ref[...]


def _ada_mods(cond, w_ada, b_ada):
    n = N_MOD * D_MODEL
    return pl.pallas_call(
        _ada_kernel,
        grid=(DEPTH, n // ADA_TN),
        in_specs=[
            pl.BlockSpec((MOD_ROWS, D_MODEL), lambda i, j: (0, 0)),
            pl.BlockSpec((None, D_MODEL, ADA_TN), lambda i, j: (i, 0, j)),
            pl.BlockSpec((None, 1, ADA_TN), lambda i, j: (i, 0, j)),
        ],
        out_specs=pl.BlockSpec((None, MOD_ROWS, ADA_TN), lambda i, j: (i, 0, j)),
        out_shape=jax.ShapeDtypeStruct((DEPTH, MOD_ROWS, n), f32),
        compiler_params=_params("arbitrary", "arbitrary"),
        name="ada_mods",
    )(cond, w_ada, b_ada.reshape(DEPTH, 1, n))


def _pool_kernel(xp_ref, x_ref, xn_ref, mods_ref, gpre_ref, gpost_ref, ps_ref, w_ref, o_ref, hbuf,
                 *, tile, seq_len):
    i = pl.program_id(1)
    n_tiles = seq_len // tile
    sh, sc, gt = _mod_slices(mods_ref, 0)
    a = gpre_ref[...] * (1.0 + sc)

    def hmod(xv):
        return _rms_scale(xv) * a + sh

    x = x_ref[...]
    hbuf[POOL_HALO:POOL_HALO + tile, :] = hmod(x)
    hbuf[0:POOL_HALO, :] = jnp.where(i > 0, hmod(xp_ref[...]), 0.0)
    hbuf[POOL_HALO + tile:, :] = jnp.where(i < n_tiles - 1, hmod(xn_ref[...]), 0.0)

    t = i * tile + lax.broadcasted_iota(jnp.int32, (tile, 1), 0)
    ys = []
    for g, w in enumerate(POOL_WINDOWS):
        cols = slice(g * POOL_GROUP_DIM, (g + 1) * POOL_GROUP_DIM)
        acc = None
        for d in range(-(w // 2), w - w // 2):
            v = hbuf[POOL_HALO + d:POOL_HALO + d + tile, cols]
            acc = v if acc is None else acc + v
        cnt = (jnp.minimum(t + (w - w // 2), seq_len) - jnp.maximum(t - w // 2, 0)).astype(f32)
        diff = acc / cnt - hbuf[POOL_HALO:POOL_HALO + tile, cols]
        y = jnp.dot(diff.astype(bf16), w_ref[g], preferred_element_type=f32)
        ys.append(y * ps_ref[:, cols])
    y = jnp.concatenate(ys, axis=-1)
    o_ref[...] = x + gt * (_rms_scale(y) * gpost_ref[...])


def _pool_sublayer(x, mods, mod_row, g_pre, g_post, pool_scale, w_pool, tile):
    bsz, seq_len, _ = x.shape
    hb = tile // POOL_HALO
    last_halo = seq_len // POOL_HALO - 1
    row = lambda d: pl.BlockSpec((1, d), lambda b, i: (0, 0))
    return pl.pallas_call(
        functools.partial(_pool_kernel, tile=tile, seq_len=seq_len),
        grid=(bsz, seq_len // tile),
        in_specs=[
            pl.BlockSpec((None, POOL_HALO, D_MODEL), lambda b, i: (b, jnp.maximum(i * hb - 1, 0), 0)),
            pl.BlockSpec((None, tile, D_MODEL), lambda b, i: (b, i, 0)),
            pl.BlockSpec((None, POOL_HALO, D_MODEL), lambda b, i: (b, jnp.minimum((i + 1) * hb, last_halo), 0)),
            pl.BlockSpec((None, 1, N_MOD * D_MODEL), lambda b, i: (mod_row(b), 0, 0)),
            row(D_MODEL), row(D_MODEL), row(D_MODEL),
            pl.BlockSpec(w_pool.shape, lambda b, i: (0, 0, 0)),
        ],
        out_specs=pl.BlockSpec((None, tile, D_MODEL), lambda b, i: (b, i, 0)),
        out_shape=jax.ShapeDtypeStruct(x.shape, f32),
        scratch_shapes=[pltpu.VMEM((tile + 2 * POOL_HALO, D_MODEL), f32)],
        compiler_params=_params("arbitrary", "arbitrary"),
        name="pool_sublayer",
    )(x, x, x, mods, g_pre, g_post, pool_scale, w_pool)


def _mlp_kernel(*refs, has_proj):
    if has_proj:
        x_ref, a_ref, wo_ref, gmix_ref, mods_ref, gpre_ref, gpost_ref, win_ref, wout_ref, o_ref = refs
    else:
        x_ref, mods_ref, gpre_ref, gpost_ref, win_ref, wout_ref, o_ref = refs
    x = x_ref[...]
    if has_proj:
        gt1 = mods_ref[:, 2 * D_MODEL:3 * D_MODEL]
        y = jnp.dot(a_ref[...], wo_ref[...], preferred_element_type=f32)
        x = x + gt1 * (_rms_scale(y) * gmix_ref[...])
    sh, sc, gt = _mod_slices(mods_ref, 3)
    h = (_rms_scale(x) * (gpre_ref[...] * (1.0 + sc)) + sh).astype(bf16)
    acc = None
    for c in range(D_FF // FF_CHUNK):
        cols = slice(c * FF_CHUNK, (c + 1) * FF_CHUNK)
        u = jnp.maximum(jnp.dot(h, win_ref[:, cols], preferred_element_type=f32), 0.0)
        part = jnp.dot((u * u).astype(bf16), wout_ref[cols, :], preferred_element_type=f32)
        acc = part if acc is None else acc + part
    o_ref[...] = x + gt * (_rms_scale(acc) * gpost_ref[...])


def _mlp_sublayer(x, mods, mod_row, g_pre, g_post, w_in, w_out, tile, proj=None):
    bsz, seq_len, _ = x.shape
    tok = pl.BlockSpec((None, tile, D_MODEL), lambda b, i: (b, i, 0))
    row = pl.BlockSpec((1, D_MODEL), lambda b, i: (0, 0))
    whole = lambda w: pl.BlockSpec(w.shape, lambda b, i: (0, 0), pipeline_mode=pl.Buffered(1))
    mods_spec = pl.BlockSpec((None, 1, N_MOD * D_MODEL), lambda b, i: (mod_row(b), 0, 0))
    if proj is None:
        args = (x, mods, g_pre, g_post, w_in, w_out)
        specs = [tok, mods_spec, row, row, whole(w_in), whole(w_out)]
    else:
        attn_out, w_o, g_mix = proj
        args = (x, attn_out, w_o, g_mix, mods, g_pre, g_post, w_in, w_out)
        specs = [tok, tok, whole(w_o), row, mods_spec, row, row, whole(w_in), whole(w_out)]
    return pl.pallas_call(
        functools.partial(_mlp_kernel, has_proj=proj is not None),
        grid=(bsz, seq_len // tile),
        in_specs=specs,
        out_specs=tok,
        out_shape=jax.ShapeDtypeStruct(x.shape, f32),
        compiler_params=_params("arbitrary", "arbitrary"),
        name="proj_mlp_sublayer" if proj is not None else "mlp_sublayer",
    )(*args)


def _rope_tables(seq_len):
    half = HEAD_DIM // 2
    t = np.arange(seq_len)
    inv_freq = np.power(np.float32(ROPE_THETA), -np.arange(0, half, 2, dtype=np.float32) / np.float32(half))
    ang_r = (t // GRID_W).astype(np.float32)[:, None] * inv_freq
    ang_c = (t % GRID_W).astype(np.float32)[:, None] * inv_freq
    cos = np.concatenate([np.cos(ang_r), np.cos(ang_c)] * 2, axis=-1)
    sin = np.concatenate([-np.sin(ang_r), -np.sin(ang_c), np.sin(ang_r), np.sin(ang_c)], axis=-1)
    return jnp.asarray(cos, f32), jnp.asarray(sin, f32)


def _rope_lane_order():
    quarter = HEAD_DIM // 4
    return np.arange(HEAD_DIM).reshape(4, quarter)[[0, 2, 1, 3]].reshape(-1)


def _qkv_kernel(*refs, n_q, rope):
    x_ref, mods_ref, gpre_ref, w_ref, gq_ref, gk_ref = refs[:6]
    refs = refs[6:]
    if rope:
        cos_ref, sin_ref = refs[:2]
        refs = refs[2:]
    qk_ref, v_ref, h_buf, slab_buf = refs
    sh, sc, _ = _mod_slices(mods_ref, 0)
    h_buf[...] = (_rms_scale(x_ref[...]) * (gpre_ref[...] * (1.0 + sc)) + sh).astype(bf16)
    n_slabs = (n_q + N_KV_HEADS) // 2
    slab_cols = 2 * HEAD_DIM

    def project(i):
        return jnp.dot(h_buf[...], w_ref[:, i * slab_cols:(i + 1) * slab_cols], preferred_element_type=f32)

    gq = gq_ref[...] * Q_SCALE
    same_head = (lax.broadcasted_iota(jnp.int32, (slab_cols, slab_cols), 0) // HEAD_DIM
                 == lax.broadcasted_iota(jnp.int32, (slab_cols, slab_cols), 1) // HEAD_DIM)
    head_ones = same_head.astype(bf16)
    slab_buf[0] = project(0)
    for i in range(n_slabs):
        slab_buf[(i + 1) % 2] = project(i + 1)
        z = slab_buf[i % 2]
        ssq = jnp.dot((z * z).astype(bf16), head_ones, preferred_element_type=f32)
        zn = z * lax.rsqrt(ssq * (1.0 / HEAD_DIM) + EPS)
        for j in range(2):
            head = 2 * i + j
            y = zn[:, j * HEAD_DIM:(j + 1) * HEAD_DIM] * (gq if head < n_q else gk_ref[...])
            if rope:
                y = y * cos_ref[...] + pltpu.roll(y, HEAD_DIM // 2, 1) * sin_ref[...]
            qk_ref[head] = y.astype(bf16)
    ones_col = (lax.broadcasted_iota(jnp.int32, (h_buf.shape[0], HEAD_DIM), 1) == 0).astype(bf16)
    for j in range(N_KV_HEADS):
        v_ref[j, :, 0:HEAD_DIM] = slab_buf[n_slabs % 2, :, j * HEAD_DIM:(j + 1) * HEAD_DIM].astype(bf16)
        v_ref[j, :, HEAD_DIM:] = ones_col


def _qkv_project(x, mods, mod_row, g_pre, w, g_q, g_k, tile, n_q, rope):
    bsz, seq_len, _ = x.shape
    assert N_KV_HEADS == 2 and n_q % 2 == 0 and w.shape[1] == (n_q + 2 * N_KV_HEADS) * HEAD_DIM
    row = lambda d: pl.BlockSpec((1, d), lambda b, i: (0, 0))
    heads = lambda n, d: pl.BlockSpec((None, n, tile, d), lambda b, i: (b, 0, i, 0))
    args = [x, mods, g_pre, w, g_q, g_k]
    specs = [
        pl.BlockSpec((None, tile, D_MODEL), lambda b, i: (b, i, 0)),
        pl.BlockSpec((None, 1, N_MOD * D_MODEL), lambda b, i: (mod_row(b), 0, 0)),
        row(D_MODEL),
        pl.BlockSpec(w.shape, lambda b, i: (0, 0), pipeline_mode=pl.Buffered(1)),
        row(HEAD_DIM), row(HEAD_DIM),
    ]
    if rope:
        args += list(_rope_tables(seq_len))
        specs += [pl.BlockSpec((tile, HEAD_DIM), lambda b, i: (i, 0))] * 2
    n_qk = n_q + N_KV_HEADS
    return pl.pallas_call(
        functools.partial(_qkv_kernel, n_q=n_q, rope=rope),
        grid=(bsz, seq_len // tile),
        in_specs=specs,
        out_specs=[heads(n_qk, HEAD_DIM), heads(N_KV_HEADS, V_WIDTH)],
        out_shape=[jax.ShapeDtypeStruct((bsz, n_qk, seq_len, HEAD_DIM), bf16),
                   jax.ShapeDtypeStruct((bsz, N_KV_HEADS, seq_len, V_WIDTH), bf16)],
        scratch_shapes=[pltpu.VMEM((tile, D_MODEL), bf16), pltpu.VMEM((2, tile, 2 * HEAD_DIM), f32)],
        compiler_params=_params("arbitrary", "arbitrary"),
        name="qkv_project" if n_q else "kv_project",
    )(*args)


def _attn_kernel(q_ref, kc_ref, vc_ref, kl_ref, vl_ref, o_ref, *, tq, tk):
    rows = Q_PER_KV * tq
    q = q_ref[...].reshape(rows, HEAD_DIM)

    def step(k, v, carry):
        m, l, acc = carry
        s = lax.dot_general(q, k, (((1,), (1,)), ((), ())), preferred_element_type=f32)
        m_new = jnp.maximum(m, jnp.max(s, axis=-1, keepdims=True))
        alpha = jnp.exp2(m - m_new)
        p = jnp.exp2(s - m_new)
        l = alpha * l + jnp.sum(p, axis=-1, keepdims=True)
        acc = alpha * acc + jnp.dot(p.astype(bf16), v, preferred_element_type=f32)
        return m_new, l, acc

    carry = (jnp.full((rows, 1), -jnp.inf, f32), jnp.zeros((rows, 1), f32), jnp.zeros((rows, HEAD_DIM), f32))
    carry = step(kc_ref[...], vc_ref[:, 0:HEAD_DIM], carry)

    def body(j, carry):
        off = pl.multiple_of(j * tk, tk)
        return step(kl_ref[pl.ds(off, tk), :], vl_ref[pl.ds(off, tk), 0:HEAD_DIM], carry)

    _, l, acc = lax.fori_loop(0, kl_ref.shape[0] // tk, body, carry)
    out = (acc / l).astype(bf16)
    for g in range(Q_PER_KV):
        o_ref[:, g * HEAD_DIM:(g + 1) * HEAD_DIM] = out[g * tq:(g + 1) * tq]


def _attn_bounded_kernel(q_ref, kc_ref, vc_ref, kl_ref, vl_ref, o_ref, *, tq, tk):
    rows = Q_PER_KV * tq
    q = q_ref[...].reshape(rows, HEAD_DIM)
    chunks = [(kc_ref, vc_ref, c0) for c0 in range(0, kc_ref.shape[0], tk)]
    chunks += [(kl_ref, vl_ref, c0) for c0 in range(0, kl_ref.shape[0], tk)]
    acc = None
    for k_ref, v_ref, c0 in chunks:
        s = lax.dot_general(q, k_ref[c0:c0 + tk, :], (((1,), (1,)), ((), ())), preferred_element_type=f32)
        pv = jnp.dot(jnp.exp2(s).astype(bf16), v_ref[c0:c0 + tk, :], preferred_element_type=f32)
        acc = pv if acc is None else acc + pv
    out = (acc[:, 0:HEAD_DIM] / acc[:, HEAD_DIM:HEAD_DIM + 1]).astype(bf16)
    for g in range(Q_PER_KV):
        o_ref[:, g * HEAD_DIM:(g + 1) * HEAD_DIM] = out[g * tq:(g + 1) * tq]


def _attention(qk_lat, k_ctx, v_ctx, v_lat, bounded):
    bsz, _, seq_len, _ = qk_lat.shape
    ctx_len = k_ctx.shape[2]
    kv = lambda n, d, h0=0: pl.BlockSpec((None, None, n, d), lambda b, h, i: (b, h0 + h, 0, 0))
    tq = ATTN_TQ_BOUNDED if bounded else ATTN_TQ
    body = (functools.partial(_attn_bounded_kernel, tq=tq, tk=ATTN_TK_BOUNDED) if bounded
            else functools.partial(_attn_kernel, tq=tq, tk=ATTN_TK))
    return pl.pallas_call(
        body,
        grid=(bsz, N_KV_HEADS, seq_len // tq),
        in_specs=[
            pl.BlockSpec((None, Q_PER_KV, tq, HEAD_DIM), lambda b, h, i: (b, h, i, 0)),
            kv(ctx_len, HEAD_DIM), kv(ctx_len, V_WIDTH), kv(seq_len, HEAD_DIM, N_Q_HEADS), kv(seq_len, V_WIDTH),
        ],
        out_specs=pl.BlockSpec((None, tq, Q_PER_KV * HEAD_DIM), lambda b, h, i: (b, i, h)),
        out_shape=jax.ShapeDtypeStruct((bsz, seq_len, N_Q_HEADS * HEAD_DIM), bf16),
        compiler_params=_params("arbitrary", "arbitrary", "arbitrary"),
        name="attention_bounded" if bounded else "attention",
    )(qk_lat, k_ctx, v_ctx, qk_lat, v_lat)


def kernel(x, c, ctx, c_ctx, w_ada, b_ada, g_mix_pre, g_mix_post, g_mlp_pre, g_mlp_post, w_pool, pool_scale,
           w_qkv, g_q, g_k, w_o, w_mlp_in, w_mlp_out):
    bsz, seq_len, d = x.shape
    ctx_len = ctx.shape[1]
    assert d == D_MODEL and seq_len % ROW_TILE == 0 and bsz + 1 <= MOD_ROWS
    assert ctx_len % POOL_HALO == 0 and (bsz * ctx_len) % ROW_TILE == 0
    row = lambda v: v.reshape(1, -1)

    cond = jnp.zeros((MOD_ROWS, D_MODEL), f32).at[:bsz].set(c).at[bsz].set(c_ctx)
    mods = _ada_mods(cond, w_ada, b_ada).reshape(DEPTH, MOD_ROWS, 1, N_MOD * D_MODEL)
    lat_row = lambda b: b
    ctx_row = lambda b: bsz

    w_pool0 = w_pool[0].astype(bf16)
    w_in0, w_out0 = w_mlp_in[0].astype(bf16), w_mlp_out[0].astype(bf16)
    pool = functools.partial(_pool_sublayer, mods=mods[0], g_pre=row(g_mix_pre[0]), g_post=row(g_mix_post[0]),
                             pool_scale=row(pool_scale[0]), w_pool=w_pool0)
    mlp0 = functools.partial(_mlp_sublayer, mods=mods[0], g_pre=row(g_mlp_pre[0]), g_post=row(g_mlp_post[0]),
                             w_in=w_in0, w_out=w_out0, tile=ROW_TILE)
    x = mlp0(pool(x, mod_row=lat_row, tile=ROW_TILE), mod_row=lat_row)
    ctx = pool(ctx, mod_row=ctx_row, tile=ctx_len)
    ctx = mlp0(ctx.reshape(1, bsz * ctx_len, D_MODEL), mod_row=ctx_row).reshape(bsz, ctx_len, D_MODEL)

    lanes = _rope_lane_order()
    n_qk = N_Q_HEADS + N_KV_HEADS
    cols = np.concatenate([h * HEAD_DIM + lanes for h in range(n_qk)]
                          + [np.arange(n_qk * HEAD_DIM, (n_qk + N_KV_HEADS) * HEAD_DIM)])
    w_qkv1 = w_qkv[0][:, cols].astype(bf16)
    qkv = functools.partial(_qkv_project, mods=mods[1], g_pre=row(g_mix_pre[1]),
                            g_q=row(g_q[0][lanes]), g_k=row(g_k[0][lanes]))
    qk_lat, v_lat = qkv(x, mod_row=lat_row, w=w_qkv1, tile=ROW_TILE, n_q=N_Q_HEADS, rope=True)
    k_ctx, v_ctx = qkv(ctx, mod_row=ctx_row, w=w_qkv1[:, N_Q_HEADS * HEAD_DIM:], tile=ctx_len, n_q=0, rope=False)
    score_bound = 1.02 * Q_SCALE * HEAD_DIM * jnp.max(jnp.abs(g_q[0])) * jnp.max(jnp.abs(g_k[0]))
    attn_out = lax.cond(score_bound <= SCORE_LOG2_LIMIT,
                        functools.partial(_attention, bounded=True), functools.partial(_attention, bounded=False),
                        qk_lat, k_ctx, v_ctx, v_lat)
    return _mlp_sublayer(x, mods[1], lat_row, row(g_mlp_pre[1]), row(g_mlp_post[1]),
                         w_mlp_in[1].astype(bf16), w_mlp_out[1].astype(bf16), ROW_TILE,
                         proj=(attn_out, w_o[0].astype(bf16), row(g_mix_post[1])))
```

```python
import functools
import math

import numpy as np
import jax
import jax.numpy as jnp
from jax import lax
from jax.experimental import pallas as pl
from jax.experimental.pallas import tpu as pltpu

D_MODEL = 1024
DEPTH = 2
GRID_W = 64
POOL_WINDOWS = (2, 4, 8, 16)
POOL_GROUP_DIM = D_MODEL // len(POOL_WINDOWS)
POOL_HALO = 8
HEAD_DIM = 128
N_Q_HEADS = D_MODEL // HEAD_DIM
N_KV_HEADS = 2
V_WIDTH = 2 * HEAD_DIM
Q_PER_KV = N_Q_HEADS // N_KV_HEADS
ROPE_THETA = 10000.0
D_FF = 4 * D_MODEL
N_MOD = 6
EPS = 1e-6
MOD_ROWS = 8
Q_SCALE = (HEAD_DIM ** -0.5) * math.log2(math.e)

ROW_TILE = 512
MLP_TILE = 1024
MLP_SUB_ROWS = 512
ATTN_TQ = 128
ATTN_TK = 512
ATTN_TQ_BOUNDED = 512
ATTN_TK_BOUNDED = 256
SCORE_LOG2_LIMIT = 100.0
FF_CHUNK = 1024
ADA_TN = 1536
VMEM_LIMIT = 56 * 1024 * 1024

f32 = jnp.float32
bf16 = jnp.bfloat16


def _params(*semantics):
    return pltpu.CompilerParams(dimension_semantics=semantics, vmem_limit_bytes=VMEM_LIMIT)


def _rms_scale(x):
    return x * lax.rsqrt(jnp.mean(x * x, axis=-1, keepdims=True) + EPS)


def _mod_slices(mods_ref, first):
    return [mods_ref[:, (first + j) * D_MODEL:(first + j + 1) * D_MODEL] for j in range(3)]


def _ada_kernel(c_ref, w_ref, b_ref, o_ref):
    c = c_ref[...]
    s = c * jax.nn.sigmoid(c)
    o_ref[...] = jnp.dot(s.astype(bf16), w_ref[...].astype(bf16), preferred_element_type=f32) + b_ref[...]


def _ada_mods(cond, w_ada, b_ada):
    n = N_MOD * D_MODEL
    return pl.pallas_call(
        _ada_kernel,
        grid=(DEPTH, n // ADA_TN),
        in_specs=[
            pl.BlockSpec((MOD_ROWS, D_MODEL), lambda i, j: (0, 0)),
            pl.BlockSpec((None, D_MODEL, ADA_TN), lambda i, j: (i, 0, j)),
            pl.BlockSpec((None, 1, ADA_TN), lambda i, j: (i, 0, j)),
        ],
        out_specs=pl.BlockSpec((None, MOD_ROWS, ADA_TN), lambda i, j: (i, 0, j)),
        out_shape=jax.ShapeDtypeStruct((DEPTH, MOD_ROWS, n), f32),
        compiler_params=_params("arbitrary", "arbitrary"),
        name="ada_mods",
    )(cond, w_ada, b_ada.reshape(DEPTH, 1, n))


def _pool_kernel(xp_ref, x_ref, xn_ref, mods_ref, gpre_ref, gpost_ref, ps_ref, w_ref, o_ref, hbuf,
                 *, tile, seq_len):
    i = pl.program_id(1)
    n_tiles = seq_len // tile
    sh, sc, gt = _mod_slices(mods_ref, 0)
    a = gpre_ref[...] * (1.0 + sc)

    def hmod(xv):
        return _rms_scale(xv) * a + sh

    x = x_ref[...]
    hbuf[POOL_HALO:POOL_HALO + tile, :] = hmod(x)
    hbuf[0:POOL_HALO, :] = jnp.where(i > 0, hmod(xp_ref[...]), 0.0)
    hbuf[POOL_HALO + tile:, :] = jnp.where(i < n_tiles - 1, hmod(xn_ref[...]), 0.0)

    t = i * tile + lax.broadcasted_iota(jnp.int32, (tile, 1), 0)
    ys = []
    for g, w in enumerate(POOL_WINDOWS):
        cols = slice(g * POOL_GROUP_DIM, (g + 1) * POOL_GROUP_DIM)
        acc = None
        for d in range(-(w // 2), w - w // 2):
            v = hbuf[POOL_HALO + d:POOL_HALO + d + tile, cols]
            acc = v if acc is None else acc + v
        cnt = (jnp.minimum(t + (w - w // 2), seq_len) - jnp.maximum(t - w // 2, 0)).astype(f32)
        diff = acc / cnt - hbuf[POOL_HALO:POOL_HALO + tile, cols]
        y = jnp.dot(diff.astype(bf16), w_ref[g], preferred_element_type=f32)
        ys.append(y * ps_ref[:, cols])
    y = jnp.concatenate(ys, axis=-1)
    o_ref[...] = x + gt * (_rms_scale(y) * gpost_ref[...])


def _pool_sublayer(x, mods, mod_row, g_pre, g_post, pool_scale, w_pool, tile):
    bsz, seq_len, _ = x.shape
    hb = tile // POOL_HALO
    last_halo = seq_len // POOL_HALO - 1
    row = lambda d: pl.BlockSpec((1, d), lambda b, i: (0, 0))
    return pl.pallas_call(
        functools.partial(_pool_kernel, tile=tile, seq_len=seq_len),
        grid=(bsz, seq_len // tile),
        in_specs=[
            pl.BlockSpec((None, POOL_HALO, D_MODEL), lambda b, i: (b, jnp.maximum(i * hb - 1, 0), 0)),
            pl.BlockSpec((None, tile, D_MODEL), lambda b, i: (b, i, 0)),
            pl.BlockSpec((None, POOL_HALO, D_MODEL), lambda b, i: (b, jnp.minimum((i + 1) * hb, last_halo), 0)),
            pl.BlockSpec((None, 1, N_MOD * D_MODEL), lambda b, i: (mod_row(b), 0, 0)),
            row(D_MODEL), row(D_MODEL), row(D_MODEL),
            pl.BlockSpec(w_pool.shape, lambda b, i: (0, 0, 0)),
        ],
        out_specs=pl.BlockSpec((None, tile, D_MODEL), lambda b, i: (b, i, 0)),
        out_shape=jax.ShapeDtypeStruct(x.shape, f32),
        scratch_shapes=[pltpu.VMEM((tile + 2 * POOL_HALO, D_MODEL), f32)],
        compiler_params=_params("arbitrary", "arbitrary"),
        name="pool_sublayer",
    )(x, x, x, mods, g_pre, g_post, pool_scale, w_pool)


def _mlp_kernel(*refs, has_proj):
    if has_proj:
        x_ref, a_ref, wo_ref, gmix_ref, mods_ref, gpre_ref, gpost_ref, win_ref, wout_ref, o_ref = refs
    else:
        x_ref, mods_ref, gpre_ref, gpost_ref, win_ref, wout_ref, o_ref = refs
    sh, sc, gt = _mod_slices(mods_ref, 3)
    pre_gain = gpre_ref[...] * (1.0 + sc)
    post_gain = gt * gpost_ref[...]
    if has_proj:
        mix_gain = mods_ref[:, 2 * D_MODEL:3 * D_MODEL] * gmix_ref[...]
    n_sub = x_ref.shape[0] // MLP_SUB_ROWS
    rows = [slice(r * MLP_SUB_ROWS, (r + 1) * MLP_SUB_ROWS) for r in range(n_sub)]

    def pre(r, y):
        x = x_ref[rows[r], :]
        if has_proj:
            x = x + _rms_scale(y) * mix_gain
        return x, (_rms_scale(x) * pre_gain + sh).astype(bf16)

    def mlp_chunk(h, c, acc):
        cols = slice(c * FF_CHUNK, (c + 1) * FF_CHUNK)
        u = jnp.maximum(jnp.dot(h, win_ref[:, cols], preferred_element_type=f32), 0.0)
        part = jnp.dot((u * u).astype(bf16), wout_ref[cols, :], preferred_element_type=f32)
        return part if acc is None else acc + part

    def post(r, x, acc):
        o_ref[rows[r], :] = x + _rms_scale(acc) * post_gain

    ys = [jnp.dot(a_ref[rows[r], :], wo_ref[...], preferred_element_type=f32) if has_proj else None
          for r in range(n_sub)]
    cur = pre(0, ys[0])
    done = None
    for r in range(n_sub):
        x, h = cur
        acc = mlp_chunk(h, 0, None)
        if r + 1 < n_sub:
            cur = pre(r + 1, ys[r + 1])
        if done is not None:
            post(*done)
        for c in range(1, D_FF // FF_CHUNK):
            acc = mlp_chunk(h, c, acc)
        done = (r, x, acc)
    post(*done)


def _mlp_sublayer(x, mods, mod_row, g_pre, g_post, w_in, w_out, layer, proj=None):
    bsz, seq_len, _ = x.shape
    tile = min(MLP_TILE, seq_len)
    assert seq_len % tile == 0 and tile % MLP_SUB_ROWS == 0
    tok = pl.BlockSpec((None, tile, D_MODEL), lambda b, i: (b, i, 0))
    row = pl.BlockSpec((1, D_MODEL), lambda b, i: (0, 0))
    whole = lambda w, l=0: pl.BlockSpec((None,) + w.shape[1:], lambda b, i: (l, 0, 0), pipeline_mode=pl.Buffered(1))
    mods_spec = pl.BlockSpec((None, 1, N_MOD * D_MODEL), lambda b, i: (mod_row(b), 0, 0))
    if proj is None:
        args = (x, mods, g_pre, g_post, w_in, w_out)
        specs = [tok, mods_spec, row, row, whole(w_in, layer), whole(w_out, layer)]
    else:
        attn_out, w_o, g_mix = proj
        args = (x, attn_out, w_o, g_mix, mods, g_pre, g_post, w_in, w_out)
        specs = [tok, tok, whole(w_o), row, mods_spec, row, row, whole(w_in, layer), whole(w_out, layer)]
    return pl.pallas_call(
        functools.partial(_mlp_kernel, has_proj=proj is not None),
        grid=(bsz, seq_len // tile),
        in_specs=specs,
        out_specs=tok,
        out_shape=jax.ShapeDtypeStruct(x.shape, f32),
        compiler_params=_params("arbitrary", "arbitrary"),
        name="proj_mlp_sublayer" if proj is not None else "mlp_sublayer",
    )(*args)


def _rope_tables(seq_len):
    half = HEAD_DIM // 2
    t = np.arange(seq_len)
    inv_freq = np.power(np.float32(ROPE_THETA), -np.arange(0, half, 2, dtype=np.float32) / np.float32(half))
    ang_r = (t // GRID_W).astype(np.float32)[:, None] * inv_freq
    ang_c = (t % GRID_W).astype(np.float32)[:, None] * inv_freq
    cos = np.concatenate([np.cos(ang_r), np.cos(ang_c)] * 2, axis=-1)
    sin = np.concatenate([-np.sin(ang_r), -np.sin(ang_c), np.sin(ang_r), np.sin(ang_c)], axis=-1)
    return jnp.asarray(cos, f32), jnp.asarray(sin, f32)


def _to_rope_lanes(a):
    quarters = a.reshape(a.shape[:-1] + (-1, 4, HEAD_DIM // 4))
    swapped = jnp.concatenate([quarters[..., 0:1, :], quarters[..., 2:3, :], quarters[..., 1:2, :],
                               quarters[..., 3:4, :]], axis=-2)
    return swapped.reshape(a.shape)


def _qkv_kernel(*refs, n_q, rope):
    x_ref, mods_ref, gpre_ref, w_ref, gq_ref, gk_ref = refs[:6]
    refs = refs[6:]
    if rope:
        cos_ref, sin_ref = refs[:2]
        refs = refs[2:]
    qk_ref, v_ref, h_buf, slab_buf = refs
    sh, sc, _ = _mod_slices(mods_ref, 0)
    h_buf[...] = (_rms_scale(x_ref[...]) * (gpre_ref[...] * (1.0 + sc)) + sh).astype(bf16)
    n_slabs = (n_q + N_KV_HEADS) // 2
    slab_cols = 2 * HEAD_DIM

    def project(i):
        return jnp.dot(h_buf[...], w_ref[:, i * slab_cols:(i + 1) * slab_cols], preferred_element_type=f32)

    gq = gq_ref[...] * Q_SCALE
    same_head = (lax.broadcasted_iota(jnp.int32, (slab_cols, slab_cols), 0) // HEAD_DIM
                 == lax.broadcasted_iota(jnp.int32, (slab_cols, slab_cols), 1) // HEAD_DIM)
    head_ones = same_head.astype(bf16)
    slab_buf[0] = project(0)
    for i in range(n_slabs):
        slab_buf[(i + 1) % 2] = project(i + 1)
        z = slab_buf[i % 2]
        ssq = jnp.dot((z * z).astype(bf16), head_ones, preferred_element_type=f32)
        zn = z * lax.rsqrt(ssq * (1.0 / HEAD_DIM) + EPS)
        for j in range(2):
            head = 2 * i + j
            y = zn[:, j * HEAD_DIM:(j + 1) * HEAD_DIM] * (gq if head < n_q else gk_ref[...])
            if rope:
                y = y * cos_ref[...] + pltpu.roll(y, HEAD_DIM // 2, 1) * sin_ref[...]
            qk_ref[head] = y.astype(bf16)
    ones_col = (lax.broadcasted_iota(jnp.int32, (h_buf.shape[0], HEAD_DIM), 1) == 0).astype(bf16)
    for j in range(N_KV_HEADS):
        v_ref[j, :, 0:HEAD_DIM] = slab_buf[n_slabs % 2, :, j * HEAD_DIM:(j + 1) * HEAD_DIM].astype(bf16)
        v_ref[j, :, HEAD_DIM:] = ones_col


def _qkv_project(x, mods, mod_row, g_pre, w, g_q, g_k, tile, n_q, rope):
    bsz, seq_len, _ = x.shape
    assert N_KV_HEADS == 2 and n_q % 2 == 0 and w.shape[1] == (n_q + 2 * N_KV_HEADS) * HEAD_DIM
    row = lambda d: pl.BlockSpec((1, d), lambda b, i: (0, 0))
    heads = lambda n, d: pl.BlockSpec((None, n, tile, d), lambda b, i: (b, 0, i, 0))
    args = [x, mods, g_pre, w, g_q, g_k]
    specs = [
        pl.BlockSpec((None, tile, D_MODEL), lambda b, i: (b, i, 0)),
        pl.BlockSpec((None, 1, N_MOD * D_MODEL), lambda b, i: (mod_row(b), 0, 0)),
        row(D_MODEL),
        pl.BlockSpec(w.shape, lambda b, i: (0, 0), pipeline_mode=pl.Buffered(1)),
        row(HEAD_DIM), row(HEAD_DIM),
    ]
    if rope:
        args += list(_rope_tables(seq_len))
        specs += [pl.BlockSpec((tile, HEAD_DIM), lambda b, i: (i, 0))] * 2
    n_qk = n_q + N_KV_HEADS
    return pl.pallas_call(
        functools.partial(_qkv_kernel, n_q=n_q, rope=rope),
        grid=(bsz, seq_len // tile),
        in_specs=specs,
        out_specs=[heads(n_qk, HEAD_DIM), heads(N_KV_HEADS, V_WIDTH)],
        out_shape=[jax.ShapeDtypeStruct((bsz, n_qk, seq_len, HEAD_DIM), bf16),
                   jax.ShapeDtypeStruct((bsz, N_KV_HEADS, seq_len, V_WIDTH), bf16)],
        scratch_shapes=[pltpu.VMEM((tile, D_MODEL), bf16), pltpu.VMEM((2, tile, 2 * HEAD_DIM), f32)],
        compiler_params=_params("arbitrary", "arbitrary"),
        name="qkv_project" if n_q else "kv_project",
    )(*args)


def _attn_kernel(q_ref, kc_ref, vc_ref, kl_ref, vl_ref, o_ref, *, tq, tk):
    rows = Q_PER_KV * tq
    q = q_ref[...].reshape(rows, HEAD_DIM)

    def step(k, v, carry):
        m, l, acc = carry
        s = lax.dot_general(q, k, (((1,), (1,)), ((), ())), preferred_element_type=f32)
        m_new = jnp.maximum(m, jnp.max(s, axis=-1, keepdims=True))
        alpha = jnp.exp2(m - m_new)
        p = jnp.exp2(s - m_new)
        l = alpha * l + jnp.sum(p, axis=-1, keepdims=True)
        acc = alpha * acc + jnp.dot(p.astype(bf16), v, preferred_element_type=f32)
        return m_new, l, acc

    carry = (jnp.full((rows, 1), -jnp.inf, f32), jnp.zeros((rows, 1), f32), jnp.zeros((rows, HEAD_DIM), f32))
    carry = step(kc_ref[...], vc_ref[:, 0:HEAD_DIM], carry)

    def body(j, carry):
        off = pl.multiple_of(j * tk, tk)
        return step(kl_ref[pl.ds(off, tk), :], vl_ref[pl.ds(off, tk), 0:HEAD_DIM], carry)

    _, l, acc = lax.fori_loop(0, kl_ref.shape[0] // tk, body, carry)
    out = (acc / l).astype(bf16)
    for g in range(Q_PER_KV):
        o_ref[:, g * HEAD_DIM:(g + 1) * HEAD_DIM] = out[g * tq:(g + 1) * tq]


def _attn_bounded_kernel(q_ref, kc_ref, vc_ref, kl_ref, vl_ref, o_ref, *, tq, tk):
    rows = Q_PER_KV * tq
    q = q_ref[...].reshape(rows, HEAD_DIM)
    chunks = [(kc_ref, vc_ref, c0) for c0 in range(0, kc_ref.shape[0], tk)]
    chunks += [(kl_ref, vl_ref, c0) for c0 in range(0, kl_ref.shape[0], tk)]
    acc = None
    for k_ref, v_ref, c0 in chunks:
        s = lax.dot_general(q, k_ref[c0:c0 + tk, :], (((1,), (1,)), ((), ())), preferred_element_type=f32)
        pv = jnp.dot(jnp.exp2(s).astype(bf16), v_ref[c0:c0 + tk, :], preferred_element_type=f32)
        acc = pv if acc is None else acc + pv
    out = (acc[:, 0:HEAD_DIM] / acc[:, HEAD_DIM:HEAD_DIM + 1]).astype(bf16)
    for g in range(Q_PER_KV):
        o_ref[:, g * HEAD_DIM:(g + 1) * HEAD_DIM] = out[g * tq:(g + 1) * tq]


def _attention(qk_lat, k_ctx, v_ctx, v_lat, bounded):
    bsz, _, seq_len, _ = qk_lat.shape
    ctx_len = k_ctx.shape[2]
    kv = lambda n, d, h0=0: pl.BlockSpec((None, None, n, d), lambda b, h, i: (b, h0 + h, 0, 0))
    tq = ATTN_TQ_BOUNDED if bounded else ATTN_TQ
    body = (functools.partial(_attn_bounded_kernel, tq=tq, tk=ATTN_TK_BOUNDED) if bounded
            else functools.partial(_attn_kernel, tq=tq, tk=ATTN_TK))
    return pl.pallas_call(
        body,
        grid=(bsz, N_KV_HEADS, seq_len // tq),
        in_specs=[
            pl.BlockSpec((None, Q_PER_KV, tq, HEAD_DIM), lambda b, h, i: (b, h, i, 0)),
            kv(ctx_len, HEAD_DIM), kv(ctx_len, V_WIDTH), kv(seq_len, HEAD_DIM, N_Q_HEADS), kv(seq_len, V_WIDTH),
        ],
        out_specs=pl.BlockSpec((None, tq, Q_PER_KV * HEAD_DIM), lambda b, h, i: (b, i, h)),
        out_shape=jax.ShapeDtypeStruct((bsz, seq_len, N_Q_HEADS * HEAD_DIM), bf16),
        compiler_params=_params("arbitrary", "arbitrary", "arbitrary"),
        name="attention_bounded" if bounded else "attention",
    )(qk_lat, k_ctx, v_ctx, qk_lat, v_lat)


def kernel(x, c, ctx, c_ctx, w_ada, b_ada, g_mix_pre, g_mix_post, g_mlp_pre, g_mlp_post, w_pool, pool_scale,
           w_qkv, g_q, g_k, w_o, w_mlp_in, w_mlp_out):
    bsz, seq_len, d = x.shape
    ctx_len = ctx.shape[1]
    assert d == D_MODEL and seq_len % ROW_TILE == 0 and bsz + 1 <= MOD_ROWS
    assert ctx_len % POOL_HALO == 0 and (bsz * ctx_len) % ROW_TILE == 0
    row = lambda v: v.reshape(1, -1)

    cond = jnp.zeros((MOD_ROWS, D_MODEL), f32).at[:bsz].set(c).at[bsz].set(c_ctx)
    mods = _ada_mods(cond, w_ada, b_ada).reshape(DEPTH, MOD_ROWS, 1, N_MOD * D_MODEL)
    lat_row = lambda b: b
    ctx_row = lambda b: bsz

    w_pool0 = w_pool[0].astype(bf16)
    w_in, w_out = w_mlp_in.astype(bf16), w_mlp_out.astype(bf16)
    pool = functools.partial(_pool_sublayer, mods=mods[0], g_pre=row(g_mix_pre[0]), g_post=row(g_mix_post[0]),
                             pool_scale=row(pool_scale[0]), w_pool=w_pool0)
    mlp0 = functools.partial(_mlp_sublayer, mods=mods[0], g_pre=row(g_mlp_pre[0]), g_post=row(g_mlp_post[0]),
                             w_in=w_in, w_out=w_out, layer=0)
    x = mlp0(pool(x, mod_row=lat_row, tile=ROW_TILE), mod_row=lat_row)
    ctx = pool(ctx, mod_row=ctx_row, tile=ctx_len)
    ctx = mlp0(ctx.reshape(1, bsz * ctx_len, D_MODEL), mod_row=ctx_row).reshape(bsz, ctx_len, D_MODEL)

    qk_cols = (N_Q_HEADS + N_KV_HEADS) * HEAD_DIM
    w_qkv1 = jnp.concatenate([_to_rope_lanes(w_qkv[0, :, :qk_cols]), w_qkv[0, :, qk_cols:]], axis=-1).astype(bf16)
    qkv = functools.partial(_qkv_project, mods=mods[1], g_pre=row(g_mix_pre[1]),
                            g_q=row(_to_rope_lanes(g_q[0])), g_k=row(_to_rope_lanes(g_k[0])))
    qk_lat, v_lat = qkv(x, mod_row=lat_row, w=w_qkv1, tile=ROW_TILE, n_q=N_Q_HEADS, rope=True)
    k_ctx, v_ctx = qkv(ctx, mod_row=ctx_row, w=w_qkv1[:, N_Q_HEADS * HEAD_DIM:], tile=ctx_len, n_q=0, rope=False)
    score_bound = 1.02 * Q_SCALE * HEAD_DIM * jnp.max(jnp.abs(g_q[0])) * jnp.max(jnp.abs(g_k[0]))
    attn_out = lax.cond(score_bound <= SCORE_LOG2_LIMIT,
                        functools.partial(_attention, bounded=True), functools.partial(_attention, bounded=False),
                        qk_lat, k_ctx, v_ctx, v_lat)
    return _mlp_sublayer(x, mods[1], lat_row, row(g_mlp_pre[1]), row(g_mlp_post[1]),
                         w_in, w_out, 1, proj=(attn_out, w_o.astype(bf16), row(g_mix_post[1])))
```

```python
import functools
import math

import numpy as np
import jax
import jax.numpy as jnp
from jax import lax
from jax.experimental import pallas as pl
from jax.experimental.pallas import tpu as pltpu

D_MODEL = 1024
DEPTH = 2
GRID_W = 64
POOL_WINDOWS = (2, 4, 8, 16)
POOL_GROUP_DIM = D_MODEL // len(POOL_WINDOWS)
POOL_HALO = 8
HEAD_DIM = 128
N_Q_HEADS = D_MODEL // HEAD_DIM
N_KV_HEADS = 2
V_WIDTH = 2 * HEAD_DIM
Q_PER_KV = N_Q_HEADS // N_KV_HEADS
ROPE_THETA = 10000.0
D_FF = 4 * D_MODEL
N_MOD = 6
EPS = 1e-6
MOD_ROWS = 8
Q_SCALE = (HEAD_DIM ** -0.5) * math.log2(math.e)

ROW_TILE = 512
MLP_TILE = 1024
MLP_SUB_ROWS = 512
ATTN_TQ = 128
ATTN_TK = 512
ATTN_TQ_BOUNDED = 512
ATTN_TK_BOUNDED = 256
SCORE_LOG2_LIMIT = 100.0
FF_CHUNK = 1024
ADA_TN = 1536
VMEM_LIMIT = 56 * 1024 * 1024

f32 = jnp.float32
bf16 = jnp.bfloat16


def _params(*semantics):
    return pltpu.CompilerParams(dimension_semantics=semantics, vmem_limit_bytes=VMEM_LIMIT)


def _rms_scale(x):
    return x * lax.rsqrt(jnp.mean(x * x, axis=-1, keepdims=True) + EPS)


def _mod_slices(mods_ref, first):
    return [mods_ref[:, (first + j) * D_MODEL:(first + j + 1) * D_MODEL] for j in range(3)]


def _ada_kernel(c_ref, w_ref, b_ref, o_ref):
    c = c_ref[...]
    s = c * jax.nn.sigmoid(c)
    o_ref[...] = jnp.dot(s.astype(bf16), w_ref[...].astype(bf16), preferred_element_type=f32) + b_ref[...]


def _ada_mods(cond, w_ada, b_ada):
    n = N_MOD * D_MODEL
    return pl.pallas_call(
        _ada_kernel,
        grid=(DEPTH, n // ADA_TN),
        in_specs=[
            pl.BlockSpec((MOD_ROWS, D_MODEL), lambda i, j: (0, 0)),
            pl.BlockSpec((None, D_MODEL, ADA_TN), lambda i, j: (i, 0, j)),
            pl.BlockSpec((None, 1, ADA_TN), lambda i, j: (i, 0, j)),
        ],
        out_specs=pl.BlockSpec((None, MOD_ROWS, ADA_TN), lambda i, j: (i, 0, j)),
        out_shape=jax.ShapeDtypeStruct((DEPTH, MOD_ROWS, n), f32),
        compiler_params=_params("arbitrary", "arbitrary"),
        name="ada_mods",
    )(cond, w_ada, b_ada.reshape(DEPTH, 1, n))


def _pool_mlp_kernel(xp_ref, x_ref, xn_ref, mods_ref, prev_mods_ref, gmix_pre_ref, gmix_post_ref, ps_ref, wpool_ref,
                     gpre_ref, gpost_ref, win_ref, wout_ref, o_ref, hbuf, x1_buf, h_buf, *, tile, seq_len):
    s = pl.program_id(0)
    n_tiles = seq_len // tile
    i = jnp.minimum(s, pl.num_programs(0) - 2) % n_tiles

    @pl.when(s == 0)
    def _():
        x1_buf[...] = jnp.zeros_like(x1_buf)
        h_buf[...] = jnp.zeros_like(h_buf)

    def mlp_chunk(c, acc):
        cols = slice(c * FF_CHUNK, (c + 1) * FF_CHUNK)
        u = jnp.maximum(jnp.dot(h_buf[...], win_ref[:, cols], preferred_element_type=f32), 0.0)
        part = jnp.dot((u * u).astype(bf16), wout_ref[cols, :], preferred_element_type=f32)
        return part if acc is None else acc + part

    sh1, sc1, gt1 = _mod_slices(mods_ref, 0)
    sh2, sc2, _ = _mod_slices(mods_ref, 3)
    mix_in_gain = gmix_pre_ref[...] * (1.0 + sc1)

    def hmod(xv):
        return _rms_scale(xv) * mix_in_gain + sh1

    def pool_group(g):
        w = POOL_WINDOWS[g]
        cols = slice(g * POOL_GROUP_DIM, (g + 1) * POOL_GROUP_DIM)
        acc = None
        for d in range(-(w // 2), w - w // 2):
            v = hbuf[POOL_HALO + d:POOL_HALO + d + tile, cols]
            acc = v if acc is None else acc + v
        t = i * tile + lax.broadcasted_iota(jnp.int32, (tile, 1), 0)
        cnt = (jnp.minimum(t + (w - w // 2), seq_len) - jnp.maximum(t - w // 2, 0)).astype(f32)
        diff = acc / cnt - hbuf[POOL_HALO:POOL_HALO + tile, cols]
        return jnp.dot(diff.astype(bf16), wpool_ref[g], preferred_element_type=f32) * ps_ref[:, cols]

    acc = mlp_chunk(0, None)
    x = x_ref[...]
    hbuf[POOL_HALO:POOL_HALO + tile, :] = hmod(x)
    hbuf[0:POOL_HALO, :] = jnp.where(i > 0, hmod(xp_ref[...]), 0.0)
    hbuf[POOL_HALO + tile:, :] = jnp.where(i < n_tiles - 1, hmod(xn_ref[...]), 0.0)
    acc = mlp_chunk(1, acc)
    ys = [pool_group(0), pool_group(1)]
    acc = mlp_chunk(2, acc)
    ys += [pool_group(2), pool_group(3)]
    acc = mlp_chunk(3, acc)
    x1 = x + _rms_scale(jnp.concatenate(ys, axis=-1)) * (gt1 * gmix_post_ref[...])
    h = (_rms_scale(x1) * (gpre_ref[...] * (1.0 + sc2)) + sh2).astype(bf16)

    gt2_prev = prev_mods_ref[:, 5 * D_MODEL:6 * D_MODEL]
    o_ref[...] = x1_buf[...] + _rms_scale(acc) * (gt2_prev * gpost_ref[...])
    x1_buf[...] = x1
    h_buf[...] = h


def _pool_mlp_layer(x, mods, mod_row, g_mix_pre, g_mix_post, pool_scale, w_pool, g_mlp_pre, g_mlp_post,
                    w_in, w_out, layer):
    bsz, seq_len, _ = x.shape
    tile = min(ROW_TILE, seq_len)
    assert seq_len % tile == 0 and tile % POOL_HALO == 0 and D_FF // FF_CHUNK == len(POOL_WINDOWS) == 4
    n_tiles = seq_len // tile
    total = bsz * n_tiles
    hb = tile // POOL_HALO
    last_halo = seq_len // POOL_HALO - 1
    cur = lambda s: jnp.minimum(s, total - 1)
    prev = lambda s: jnp.maximum(s - 1, 0)
    row = pl.BlockSpec((1, D_MODEL), lambda s: (0, 0))
    mods_spec = lambda step: pl.BlockSpec((None, 1, N_MOD * D_MODEL), lambda s: (mod_row(step(s) // n_tiles), 0, 0))
    tok = lambda step: pl.BlockSpec((None, tile, D_MODEL), lambda s: (step(s) // n_tiles, step(s) % n_tiles, 0))
    whole = lambda w, l: pl.BlockSpec((None,) + w.shape[1:], lambda s: (l, 0, 0), pipeline_mode=pl.Buffered(1))
    return pl.pallas_call(
        functools.partial(_pool_mlp_kernel, tile=tile, seq_len=seq_len),
        grid=(total + 1,),
        in_specs=[
            pl.BlockSpec((None, POOL_HALO, D_MODEL),
                         lambda s: (cur(s) // n_tiles, jnp.maximum(cur(s) % n_tiles * hb - 1, 0), 0)),
            tok(cur),
            pl.BlockSpec((None, POOL_HALO, D_MODEL),
                         lambda s: (cur(s) // n_tiles, jnp.minimum((cur(s) % n_tiles + 1) * hb, last_halo), 0)),
            mods_spec(cur), mods_spec(prev),
            row, row, row,
            pl.BlockSpec(w_pool.shape, lambda s: (0, 0, 0)),
            row, row,
            whole(w_in, layer), whole(w_out, layer),
        ],
        out_specs=tok(prev),
        out_shape=jax.ShapeDtypeStruct(x.shape, f32),
        scratch_shapes=[pltpu.VMEM((tile + 2 * POOL_HALO, D_MODEL), f32),
                        pltpu.VMEM((tile, D_MODEL), f32), pltpu.VMEM((tile, D_MODEL), bf16)],
        compiler_params=_params("arbitrary"),
        name="pool_mlp_layer",
    )(x, x, x, mods, mods, g_mix_pre, g_mix_post, pool_scale, w_pool, g_mlp_pre, g_mlp_post, w_in, w_out)


def _mlp_kernel(*refs, has_proj):
    if has_proj:
        x_ref, a_ref, wo_ref, gmix_ref, mods_ref, gpre_ref, gpost_ref, win_ref, wout_ref, o_ref = refs
    else:
        x_ref, mods_ref, gpre_ref, gpost_ref, win_ref, wout_ref, o_ref = refs
    sh, sc, gt = _mod_slices(mods_ref, 3)
    pre_gain = gpre_ref[...] * (1.0 + sc)
    post_gain = gt * gpost_ref[...]
    if has_proj:
        mix_gain = mods_ref[:, 2 * D_MODEL:3 * D_MODEL] * gmix_ref[...]
    n_sub = x_ref.shape[0] // MLP_SUB_ROWS
    rows = [slice(r * MLP_SUB_ROWS, (r + 1) * MLP_SUB_ROWS) for r in range(n_sub)]

    def pre(r, y):
        x = x_ref[rows[r], :]
        if has_proj:
            x = x + _rms_scale(y) * mix_gain
        return x, (_rms_scale(x) * pre_gain + sh).astype(bf16)

    def mlp_chunk(h, c, acc):
        cols = slice(c * FF_CHUNK, (c + 1) * FF_CHUNK)
        u = jnp.maximum(jnp.dot(h, win_ref[:, cols], preferred_element_type=f32), 0.0)
        part = jnp.dot((u * u).astype(bf16), wout_ref[cols, :], preferred_element_type=f32)
        return part if acc is None else acc + part

    def post(r, x, acc):
        o_ref[rows[r], :] = x + _rms_scale(acc) * post_gain

    ys = [jnp.dot(a_ref[rows[r], :], wo_ref[...], preferred_element_type=f32) if has_proj else None
          for r in range(n_sub)]
    cur = pre(0, ys[0])
    done = None
    for r in range(n_sub):
        x, h = cur
        acc = mlp_chunk(h, 0, None)
        if r + 1 < n_sub:
            cur = pre(r + 1, ys[r + 1])
        if done is not None:
            post(*done)
        for c in range(1, D_FF // FF_CHUNK):
            acc = mlp_chunk(h, c, acc)
        done = (r, x, acc)
    post(*done)


def _mlp_sublayer(x, mods, mod_row, g_pre, g_post, w_in, w_out, layer, proj=None):
    bsz, seq_len, _ = x.shape
    tile = min(MLP_TILE, seq_len)
    assert seq_len % tile == 0 and tile % MLP_SUB_ROWS == 0
    tok = pl.BlockSpec((None, tile, D_MODEL), lambda b, i: (b, i, 0))
    row = pl.BlockSpec((1, D_MODEL), lambda b, i: (0, 0))
    whole = lambda w, l=0: pl.BlockSpec((None,) + w.shape[1:], lambda b, i: (l, 0, 0), pipeline_mode=pl.Buffered(1))
    mods_spec = pl.BlockSpec((None, 1, N_MOD * D_MODEL), lambda b, i: (mod_row(b), 0, 0))
    if proj is None:
        args = (x, mods, g_pre, g_post, w_in, w_out)
        specs = [tok, mods_spec, row, row, whole(w_in, layer), whole(w_out, layer)]
    else:
        attn_out, w_o, g_mix = proj
        args = (x, attn_out, w_o, g_mix, mods, g_pre, g_post, w_in, w_out)
        specs = [tok, tok, whole(w_o), row, mods_spec, row, row, whole(w_in, layer), whole(w_out, layer)]
    return pl.pallas_call(
        functools.partial(_mlp_kernel, has_proj=proj is not None),
        grid=(bsz, seq_len // tile),
        in_specs=specs,
        out_specs=tok,
        out_shape=jax.ShapeDtypeStruct(x.shape, f32),
        compiler_params=_params("arbitrary", "arbitrary"),
        name="proj_mlp_sublayer" if proj is not None else "mlp_sublayer",
    )(*args)


def _rope_tables(seq_len):
    half = HEAD_DIM // 2
    t = np.arange(seq_len)
    inv_freq = np.power(np.float32(ROPE_THETA), -np.arange(0, half, 2, dtype=np.float32) / np.float32(half))
    ang_r = (t // GRID_W).astype(np.float32)[:, None] * inv_freq
    ang_c = (t % GRID_W).astype(np.float32)[:, None] * inv_freq
    cos = np.concatenate([np.cos(ang_r), np.cos(ang_c)] * 2, axis=-1)
    sin = np.concatenate([-np.sin(ang_r), -np.sin(ang_c), np.sin(ang_r), np.sin(ang_c)], axis=-1)
    return jnp.asarray(cos, f32), jnp.asarray(sin, f32)


def _to_rope_lanes(a):
    quarters = a.reshape(a.shape[:-1] + (-1, 4, HEAD_DIM // 4))
    swapped = jnp.concatenate([quarters[..., 0:1, :], quarters[..., 2:3, :], quarters[..., 1:2, :],
                               quarters[..., 3:4, :]], axis=-2)
    return swapped.reshape(a.shape)


def _qkv_kernel(*refs, n_q, rope):
    x_ref, mods_ref, gpre_ref, w_ref, gq_ref, gk_ref = refs[:6]
    refs = refs[6:]
    if rope:
        cos_ref, sin_ref = refs[:2]
        refs = refs[2:]
    qk_ref, v_ref, h_buf, slab_buf = refs
    sh, sc, _ = _mod_slices(mods_ref, 0)
    h_buf[...] = (_rms_scale(x_ref[...]) * (gpre_ref[...] * (1.0 + sc)) + sh).astype(bf16)
    n_slabs = (n_q + N_KV_HEADS) // 2
    slab_cols = 2 * HEAD_DIM

    def project(i):
        return jnp.dot(h_buf[...], w_ref[:, i * slab_cols:(i + 1) * slab_cols], preferred_element_type=f32)

    gq = gq_ref[...] * Q_SCALE
    same_head = (lax.broadcasted_iota(jnp.int32, (slab_cols, slab_cols), 0) // HEAD_DIM
                 == lax.broadcasted_iota(jnp.int32, (slab_cols, slab_cols), 1) // HEAD_DIM)
    head_ones = same_head.astype(bf16)
    slab_buf[0] = project(0)
    for i in range(n_slabs):
        slab_buf[(i + 1) % 2] = project(i + 1)
        z = slab_buf[i % 2]
        ssq = jnp.dot((z * z).astype(bf16), head_ones, preferred_element_type=f32)
        zn = z * lax.rsqrt(ssq * (1.0 / HEAD_DIM) + EPS)
        for j in range(2):
            head = 2 * i + j
            y = zn[:, j * HEAD_DIM:(j + 1) * HEAD_DIM] * (gq if head < n_q else gk_ref[...])
            if rope:
                y = y * cos_ref[...] + pltpu.roll(y, HEAD_DIM // 2, 1) * sin_ref[...]
            qk_ref[head] = y.astype(bf16)
    ones_col = (lax.broadcasted_iota(jnp.int32, (h_buf.shape[0], HEAD_DIM), 1) == 0).astype(bf16)
    for j in range(N_KV_HEADS):
        v_ref[j, :, 0:HEAD_DIM] = slab_buf[n_slabs % 2, :, j * HEAD_DIM:(j + 1) * HEAD_DIM].astype(bf16)
        v_ref[j, :, HEAD_DIM:] = ones_col


def _qkv_project(x, mods, mod_row, g_pre, w, g_q, g_k, tile, n_q, rope):
    bsz, seq_len, _ = x.shape
    assert N_KV_HEADS == 2 and n_q % 2 == 0 and w.shape[1] == (n_q + 2 * N_KV_HEADS) * HEAD_DIM
    row = lambda d: pl.BlockSpec((1, d), lambda b, i: (0, 0))
    heads = lambda n, d: pl.BlockSpec((None, n, tile, d), lambda b, i: (b, 0, i, 0))
    args = [x, mods, g_pre, w, g_q, g_k]
    specs = [
        pl.BlockSpec((None, tile, D_MODEL), lambda b, i: (b, i, 0)),
        pl.BlockSpec((None, 1, N_MOD * D_MODEL), lambda b, i: (mod_row(b), 0, 0)),
        row(D_MODEL),
        pl.BlockSpec(w.shape, lambda b, i: (0, 0), pipeline_mode=pl.Buffered(1)),
        row(HEAD_DIM), row(HEAD_DIM),
    ]
    if rope:
        args += list(_rope_tables(seq_len))
        specs += [pl.BlockSpec((tile, HEAD_DIM), lambda b, i: (i, 0))] * 2
    n_qk = n_q + N_KV_HEADS
    return pl.pallas_call(
        functools.partial(_qkv_kernel, n_q=n_q, rope=rope),
        grid=(bsz, seq_len // tile),
        in_specs=specs,
        out_specs=[heads(n_qk, HEAD_DIM), heads(N_KV_HEADS, V_WIDTH)],
        out_shape=[jax.ShapeDtypeStruct((bsz, n_qk, seq_len, HEAD_DIM), bf16),
                   jax.ShapeDtypeStruct((bsz, N_KV_HEADS, seq_len, V_WIDTH), bf16)],
        scratch_shapes=[pltpu.VMEM((tile, D_MODEL), bf16), pltpu.VMEM((2, tile, 2 * HEAD_DIM), f32)],
        compiler_params=_params("arbitrary", "arbitrary"),
        name="qkv_project" if n_q else "kv_project",
    )(*args)


def _attn_kernel(q_ref, kc_ref, vc_ref, kl_ref, vl_ref, o_ref, *, tq, tk):
    rows = Q_PER_KV * tq
    q = q_ref[...].reshape(rows, HEAD_DIM)

    def step(k, v, carry):
        m, l, acc = carry
        s = lax.dot_general(q, k, (((1,), (1,)), ((), ())), preferred_element_type=f32)
        m_new = jnp.maximum(m, jnp.max(s, axis=-1, keepdims=True))
        alpha = jnp.exp2(m - m_new)
        p = jnp.exp2(s - m_new)
        l = alpha * l + jnp.sum(p, axis=-1, keepdims=True)
        acc = alpha * acc + jnp.dot(p.astype(bf16), v, preferred_element_type=f32)
        return m_new, l, acc

    carry = (jnp.full((rows, 1), -jnp.inf, f32), jnp.zeros((rows, 1), f32), jnp.zeros((rows, HEAD_DIM), f32))
    carry = step(kc_ref[...], vc_ref[:, 0:HEAD_DIM], carry)

    def body(j, carry):
        off = pl.multiple_of(j * tk, tk)
        return step(kl_ref[pl.ds(off, tk), :], vl_ref[pl.ds(off, tk), 0:HEAD_DIM], carry)

    _, l, acc = lax.fori_loop(0, kl_ref.shape[0] // tk, body, carry)
    out = (acc / l).astype(bf16)
    for g in range(Q_PER_KV):
        o_ref[:, g * HEAD_DIM:(g + 1) * HEAD_DIM] = out[g * tq:(g + 1) * tq]


def _attn_bounded_kernel(q_ref, kc_ref, vc_ref, kl_ref, vl_ref, o_ref, *, tq, tk):
    rows = Q_PER_KV * tq
    q = q_ref[...].reshape(rows, HEAD_DIM)
    chunks = [(kc_ref, vc_ref, c0) for c0 in range(0, kc_ref.shape[0], tk)]
    chunks += [(kl_ref, vl_ref, c0) for c0 in range(0, kl_ref.shape[0], tk)]
    acc = None
    for k_ref, v_ref, c0 in chunks:
        s = lax.dot_general(q, k_ref[c0:c0 + tk, :], (((1,), (1,)), ((), ())), preferred_element_type=f32)
        pv = jnp.dot(jnp.exp2(s).astype(bf16), v_ref[c0:c0 + tk, :], preferred_element_type=f32)
        acc = pv if acc is None else acc + pv
    out = (acc[:, 0:HEAD_DIM] / acc[:, HEAD_DIM:HEAD_DIM + 1]).astype(bf16)
    for g in range(Q_PER_KV):
        o_ref[:, g * HEAD_DIM:(g + 1) * HEAD_DIM] = out[g * tq:(g + 1) * tq]


def _attention(qk_lat, k_ctx, v_ctx, v_lat, bounded):
    bsz, _, seq_len, _ = qk_lat.shape
    ctx_len = k_ctx.shape[2]
    kv = lambda n, d, h0=0: pl.BlockSpec((None, None, n, d), lambda b, h, i: (b, h0 + h, 0, 0))
    tq = ATTN_TQ_BOUNDED if bounded else ATTN_TQ
    body = (functools.partial(_attn_bounded_kernel, tq=tq, tk=ATTN_TK_BOUNDED) if bounded
            else functools.partial(_attn_kernel, tq=tq, tk=ATTN_TK))
    return pl.pallas_call(
        body,
        grid=(bsz, N_KV_HEADS, seq_len // tq),
        in_specs=[
            pl.BlockSpec((None, Q_PER_KV, tq, HEAD_DIM), lambda b, h, i: (b, h, i, 0)),
            kv(ctx_len, HEAD_DIM), kv(ctx_len, V_WIDTH), kv(seq_len, HEAD_DIM, N_Q_HEADS), kv(seq_len, V_WIDTH),
        ],
        out_specs=pl.BlockSpec((None, tq, Q_PER_KV * HEAD_DIM), lambda b, h, i: (b, i, h)),
        out_shape=jax.ShapeDtypeStruct((bsz, seq_len, N_Q_HEADS * HEAD_DIM), bf16),
        compiler_params=_params("arbitrary", "arbitrary", "arbitrary"),
        name="attention_bounded" if bounded else "attention",
    )(qk_lat, k_ctx, v_ctx, qk_lat, v_lat)


def kernel(x, c, ctx, c_ctx, w_ada, b_ada, g_mix_pre, g_mix_post, g_mlp_pre, g_mlp_post, w_pool, pool_scale,
           w_qkv, g_q, g_k, w_o, w_mlp_in, w_mlp_out):
    bsz, seq_len, d = x.shape
    ctx_len = ctx.shape[1]
    assert d == D_MODEL and seq_len % ROW_TILE == 0 and bsz + 1 <= MOD_ROWS
    assert ctx_len % POOL_HALO == 0 and (bsz * ctx_len) % ROW_TILE == 0
    row = lambda v: v.reshape(1, -1)

    cond = jnp.zeros((MOD_ROWS, D_MODEL), f32).at[:bsz].set(c).at[bsz].set(c_ctx)
    mods = _ada_mods(cond, w_ada, b_ada).reshape(DEPTH, MOD_ROWS, 1, N_MOD * D_MODEL)
    lat_row = lambda b: b
    ctx_row = lambda b: bsz

    w_pool0 = w_pool[0].astype(bf16)
    w_in, w_out = w_mlp_in.astype(bf16), w_mlp_out.astype(bf16)
    layer0 = functools.partial(_pool_mlp_layer, mods=mods[0], g_mix_pre=row(g_mix_pre[0]),
                               g_mix_post=row(g_mix_post[0]), pool_scale=row(pool_scale[0]), w_pool=w_pool0,
                               g_mlp_pre=row(g_mlp_pre[0]), g_mlp_post=row(g_mlp_post[0]),
                               w_in=w_in, w_out=w_out, layer=0)
    x = layer0(x, mod_row=lat_row)
    ctx = layer0(ctx, mod_row=ctx_row)

    qk_cols = (N_Q_HEADS + N_KV_HEADS) * HEAD_DIM
    w_qkv1 = jnp.concatenate([_to_rope_lanes(w_qkv[0, :, :qk_cols]), w_qkv[0, :, qk_cols:]], axis=-1).astype(bf16)
    qkv = functools.partial(_qkv_project, mods=mods[1], g_pre=row(g_mix_pre[1]),
                            g_q=row(_to_rope_lanes(g_q[0])), g_k=row(_to_rope_lanes(g_k[0])))
    qk_lat, v_lat = qkv(x, mod_row=lat_row, w=w_qkv1, tile=ROW_TILE, n_q=N_Q_HEADS, rope=True)
    k_ctx, v_ctx = qkv(ctx, mod_row=ctx_row, w=w_qkv1[:, N_Q_HEADS * HEAD_DIM:], tile=ctx_len, n_q=0, rope=False)
    score_bound = 1.02 * Q_SCALE * HEAD_DIM * jnp.max(jnp.abs(g_q[0])) * jnp.max(jnp.abs(g_k[0]))
    attn_out = lax.cond(score_bound <= SCORE_LOG2_LIMIT,
                        functools.partial(_attention, bounded=True), functools.partial(_attention, bounded=False),
                        qk_lat, k_ctx, v_ctx, v_lat)
    return _mlp_sublayer(x, mods[1], lat_row, row(g_mlp_pre[1]), row(g_mlp_post[1]),
                         w_in, w_out, 1, proj=(attn_out, w_o.astype(bf16), row(g_mix_post[1])))
```

```python
import functools
import math

import numpy as np
import jax
import jax.numpy as jnp
from jax import lax
from jax.experimental import pallas as pl
from jax.experimental.pallas import tpu as pltpu

D_MODEL = 1024
DEPTH = 2
GRID_W = 64
POOL_WINDOWS = (2, 4, 8, 16)
POOL_GROUP_DIM = D_MODEL // len(POOL_WINDOWS)
POOL_HALO = 8
HEAD_DIM = 128
N_Q_HEADS = D_MODEL // HEAD_DIM
N_KV_HEADS = 2
V_WIDTH = 2 * HEAD_DIM
Q_PER_KV = N_Q_HEADS // N_KV_HEADS
ROPE_THETA = 10000.0
D_FF = 4 * D_MODEL
N_MOD = 6
EPS = 1e-6
MOD_ROWS = 8
Q_SCALE = (HEAD_DIM ** -0.5) * math.log2(math.e)

ROW_TILE = 512
MLP_TILE = 1024
MLP_SUB_ROWS = 512
ATTN_TQ = 128
ATTN_TK = 512
ATTN_TQ_BOUNDED = 512
ATTN_TK_BOUNDED = 256
SCORE_LOG2_LIMIT = 100.0
FF_CHUNK = 1024
ADA_TN = 1536
VMEM_LIMIT = 56 * 1024 * 1024

f32 = jnp.float32
bf16 = jnp.bfloat16


def _params(*semantics, flags=None):
    return pltpu.CompilerParams(dimension_semantics=semantics, vmem_limit_bytes=VMEM_LIMIT, flags=flags)


def _rms_scale(x):
    return x * lax.rsqrt(jnp.mean(x * x, axis=-1, keepdims=True) + EPS)


def _sq_relu(u):
    ub = jnp.maximum(u.astype(bf16), 0.0)
    return ub * ub


def _mod_slices(mods_ref, first):
    return [mods_ref[:, (first + j) * D_MODEL:(first + j + 1) * D_MODEL] for j in range(3)]


def _ada_kernel(c_ref, w_ref, b_ref, o_ref):
    c = c_ref[...]
    s = c * jax.nn.sigmoid(c)
    o_ref[...] = jnp.dot(s.astype(bf16), w_ref[...].astype(bf16), preferred_element_type=f32) + b_ref[...]


def _ada_mods(cond, w_ada, b_ada):
    n = N_MOD * D_MODEL
    return pl.pallas_call(
        _ada_kernel,
        grid=(DEPTH, n // ADA_TN),
        in_specs=[
            pl.BlockSpec((MOD_ROWS, D_MODEL), lambda i, j: (0, 0)),
            pl.BlockSpec((None, D_MODEL, ADA_TN), lambda i, j: (i, 0, j)),
            pl.BlockSpec((None, 1, ADA_TN), lambda i, j: (i, 0, j)),
        ],
        out_specs=pl.BlockSpec((None, MOD_ROWS, ADA_TN), lambda i, j: (i, 0, j)),
        out_shape=jax.ShapeDtypeStruct((DEPTH, MOD_ROWS, n), f32),
        compiler_params=_params("arbitrary", "arbitrary"),
        name="ada_mods",
    )(cond, w_ada, b_ada.reshape(DEPTH, 1, n))


def _pool_mlp_kernel(xp_ref, x_ref, xn_ref, mods_ref, prev_mods_ref, gmix_pre_ref, gmix_post_ref, ps_ref, wpool_ref,
                     gpre_ref, gpost_ref, win_ref, wout_ref, o_ref, hbuf, x1_buf, h_buf, *, tile, seq_len):
    s = pl.program_id(0)
    n_tiles = seq_len // tile
    i = jnp.minimum(s, pl.num_programs(0) - 2) % n_tiles

    def mlp_chunk(c, acc):
        cols = slice(c * FF_CHUNK, (c + 1) * FF_CHUNK)
        u = jnp.dot(h_buf[...], win_ref[:, cols], preferred_element_type=f32)
        part = jnp.dot(_sq_relu(u), wout_ref[cols, :], preferred_element_type=f32)
        return part if acc is None else acc + part

    sh1, sc1, gt1 = _mod_slices(mods_ref, 0)
    sh2, sc2, _ = _mod_slices(mods_ref, 3)
    mix_in_gain = gmix_pre_ref[...] * (1.0 + sc1)

    def hmod(xv):
        return _rms_scale(xv) * mix_in_gain + sh1

    def pool_group(g):
        w = POOL_WINDOWS[g]
        cols = slice(g * POOL_GROUP_DIM, (g + 1) * POOL_GROUP_DIM)
        acc = None
        for d in range(-(w // 2), w - w // 2):
            v = hbuf[POOL_HALO + d:POOL_HALO + d + tile, cols]
            acc = v if acc is None else acc + v
        def clipped_mean(r0):
            t = i * tile + r0 + lax.broadcasted_iota(jnp.int32, (POOL_HALO, 1), 0)
            cnt = jnp.minimum(t + (w - w // 2), seq_len) - jnp.maximum(t - w // 2, 0)
            return acc[r0:r0 + POOL_HALO] / cnt.astype(f32)

        mean = jnp.concatenate([clipped_mean(0), acc[POOL_HALO:tile - POOL_HALO] * (1.0 / w),
                                clipped_mean(tile - POOL_HALO)], axis=0)
        diff = mean - hbuf[POOL_HALO:POOL_HALO + tile, cols]
        return jnp.dot(diff.astype(bf16), wpool_ref[g], preferred_element_type=f32) * ps_ref[:, cols]

    def fill_hbuf():
        hbuf[POOL_HALO:POOL_HALO + tile, :] = hmod(x_ref[...])
        hbuf[0:POOL_HALO, :] = jnp.where(i > 0, hmod(xp_ref[...]), 0.0)
        hbuf[POOL_HALO + tile:, :] = jnp.where(i < n_tiles - 1, hmod(xn_ref[...]), 0.0)

    def step(with_mlp, with_pool):
        acc, ys = None, []
        vpu_pieces = [fill_hbuf, lambda: ys.extend([pool_group(0), pool_group(1)]),
                      lambda: ys.extend([pool_group(2), pool_group(3)])]
        for c in range(D_FF // FF_CHUNK):
            if with_mlp:
                acc = mlp_chunk(c, acc)
            if with_pool and c < len(vpu_pieces):
                vpu_pieces[c]()
        if with_pool:
            x1 = x_ref[...] + _rms_scale(jnp.concatenate(ys, axis=-1)) * (gt1 * gmix_post_ref[...])
            h = (_rms_scale(x1) * (gpre_ref[...] * (1.0 + sc2)) + sh2).astype(bf16)
        if with_mlp:
            gt2_prev = prev_mods_ref[:, 5 * D_MODEL:6 * D_MODEL]
            o_ref[...] = x1_buf[...] + _rms_scale(acc) * (gt2_prev * gpost_ref[...])
        if with_pool:
            x1_buf[...] = x1
            h_buf[...] = h

    last = pl.num_programs(0) - 1
    pl.when(s == 0)(functools.partial(step, False, True))
    pl.when(jnp.logical_and(s > 0, s < last))(functools.partial(step, True, True))
    pl.when(s == last)(functools.partial(step, True, False))


def _pool_mlp_layer(x, mods, mod_row, g_mix_pre, g_mix_post, pool_scale, w_pool, g_mlp_pre, g_mlp_post,
                    w_in, w_out, layer):
    bsz, seq_len, _ = x.shape
    tile = min(ROW_TILE, seq_len)
    assert seq_len % tile == 0 and tile % POOL_HALO == 0 and D_FF // FF_CHUNK == len(POOL_WINDOWS) == 4
    n_tiles = seq_len // tile
    total = bsz * n_tiles
    hb = tile // POOL_HALO
    last_halo = seq_len // POOL_HALO - 1
    cur = lambda s: jnp.minimum(s, total - 1)
    prev = lambda s: jnp.maximum(s - 1, 0)
    row = pl.BlockSpec((1, D_MODEL), lambda s: (0, 0))
    mods_spec = lambda step: pl.BlockSpec((None, 1, N_MOD * D_MODEL), lambda s: (mod_row(step(s) // n_tiles), 0, 0))
    tok = lambda step: pl.BlockSpec((None, tile, D_MODEL), lambda s: (step(s) // n_tiles, step(s) % n_tiles, 0))
    whole = lambda w, l: pl.BlockSpec((None,) + w.shape[1:], lambda s: (l, 0, 0), pipeline_mode=pl.Buffered(1))
    return pl.pallas_call(
        functools.partial(_pool_mlp_kernel, tile=tile, seq_len=seq_len),
        grid=(total + 1,),
        in_specs=[
            pl.BlockSpec((None, POOL_HALO, D_MODEL),
                         lambda s: (cur(s) // n_tiles, jnp.maximum(cur(s) % n_tiles * hb - 1, 0), 0)),
            tok(cur),
            pl.BlockSpec((None, POOL_HALO, D_MODEL),
                         lambda s: (cur(s) // n_tiles, jnp.minimum((cur(s) % n_tiles + 1) * hb, last_halo), 0)),
            mods_spec(cur), mods_spec(prev),
            row, row, row,
            pl.BlockSpec(w_pool.shape, lambda s: (0, 0, 0)),
            row, row,
            whole(w_in, layer), whole(w_out, layer),
        ],
        out_specs=tok(prev),
        out_shape=jax.ShapeDtypeStruct(x.shape, f32),
        scratch_shapes=[pltpu.VMEM((tile + 2 * POOL_HALO, D_MODEL), f32),
                        pltpu.VMEM((tile, D_MODEL), f32), pltpu.VMEM((tile, D_MODEL), bf16)],
        compiler_params=_params("arbitrary"),
        name="pool_mlp_layer",
    )(x, x, x, mods, mods, g_mix_pre, g_mix_post, pool_scale, w_pool, g_mlp_pre, g_mlp_post, w_in, w_out)


def _mlp_kernel(*refs, has_proj):
    if has_proj:
        x_ref, a_ref, wo_ref, gmix_ref, mods_ref, gpre_ref, gpost_ref, win_ref, wout_ref, o_ref = refs
    else:
        x_ref, mods_ref, gpre_ref, gpost_ref, win_ref, wout_ref, o_ref = refs
    sh, sc, gt = _mod_slices(mods_ref, 3)
    pre_gain = gpre_ref[...] * (1.0 + sc)
    post_gain = gt * gpost_ref[...]
    if has_proj:
        mix_gain = mods_ref[:, 2 * D_MODEL:3 * D_MODEL] * gmix_ref[...]
    n_sub = x_ref.shape[0] // MLP_SUB_ROWS
    rows = [slice(r * MLP_SUB_ROWS, (r + 1) * MLP_SUB_ROWS) for r in range(n_sub)]

    def pre(r, y):
        x = x_ref[rows[r], :]
        if has_proj:
            x = x + _rms_scale(y) * mix_gain
        return x, (_rms_scale(x) * pre_gain + sh).astype(bf16)

    def mlp_chunk(h, c, acc):
        cols = slice(c * FF_CHUNK, (c + 1) * FF_CHUNK)
        u = jnp.dot(h, win_ref[:, cols], preferred_element_type=f32)
        part = jnp.dot(_sq_relu(u), wout_ref[cols, :], preferred_element_type=f32)
        return part if acc is None else acc + part

    def post(r, x, acc):
        o_ref[rows[r], :] = x + _rms_scale(acc) * post_gain

    ys = [jnp.dot(a_ref[rows[r], :], wo_ref[...], preferred_element_type=f32) if has_proj else None
          for r in range(n_sub)]
    cur = pre(0, ys[0])
    done = None
    for r in range(n_sub):
        x, h = cur
        acc = mlp_chunk(h, 0, None)
        if r + 1 < n_sub:
            cur = pre(r + 1, ys[r + 1])
        if done is not None:
            post(*done)
        for c in range(1, D_FF // FF_CHUNK):
            acc = mlp_chunk(h, c, acc)
        done = (r, x, acc)
    post(*done)


def _mlp_sublayer(x, mods, mod_row, g_pre, g_post, w_in, w_out, layer, proj=None):
    bsz, seq_len, _ = x.shape
    tile = min(MLP_TILE, seq_len)
    assert seq_len % tile == 0 and tile % MLP_SUB_ROWS == 0
    tok = pl.BlockSpec((None, tile, D_MODEL), lambda b, i: (b, i, 0))
    row = pl.BlockSpec((1, D_MODEL), lambda b, i: (0, 0))
    whole = lambda w, l=0: pl.BlockSpec((None,) + w.shape[1:], lambda b, i: (l, 0, 0), pipeline_mode=pl.Buffered(1))
    mods_spec = pl.BlockSpec((None, 1, N_MOD * D_MODEL), lambda b, i: (mod_row(b), 0, 0))
    if proj is None:
        args = (x, mods, g_pre, g_post, w_in, w_out)
        specs = [tok, mods_spec, row, row, whole(w_in, layer), whole(w_out, layer)]
    else:
        attn_out, w_o, g_mix = proj
        args = (x, attn_out, w_o, g_mix, mods, g_pre, g_post, w_in, w_out)
        specs = [tok, tok, whole(w_o), row, mods_spec, row, row, whole(w_in, layer), whole(w_out, layer)]
    return pl.pallas_call(
        functools.partial(_mlp_kernel, has_proj=proj is not None),
        grid=(bsz, seq_len // tile),
        in_specs=specs,
        out_specs=tok,
        out_shape=jax.ShapeDtypeStruct(x.shape, f32),
        compiler_params=_params("arbitrary", "arbitrary"),
        name="proj_mlp_sublayer" if proj is not None else "mlp_sublayer",
    )(*args)


def _rope_tables(seq_len):
    half = HEAD_DIM // 2
    t = np.arange(seq_len)
    inv_freq = np.power(np.float32(ROPE_THETA), -np.arange(0, half, 2, dtype=np.float32) / np.float32(half))
    ang_r = (t // GRID_W).astype(np.float32)[:, None] * inv_freq
    ang_c = (t % GRID_W).astype(np.float32)[:, None] * inv_freq
    cos = np.concatenate([np.cos(ang_r), np.cos(ang_c)] * 2, axis=-1)
    sin = np.concatenate([-np.sin(ang_r), -np.sin(ang_c), np.sin(ang_r), np.sin(ang_c)], axis=-1)
    return jnp.asarray(cos, f32), jnp.asarray(sin, f32)


def _to_rope_lanes(a):
    quarters = a.reshape(a.shape[:-1] + (-1, 4, HEAD_DIM // 4))
    swapped = jnp.concatenate([quarters[..., 0:1, :], quarters[..., 2:3, :], quarters[..., 1:2, :],
                               quarters[..., 3:4, :]], axis=-2)
    return swapped.reshape(a.shape)


def _qkv_kernel(*refs, n_q, rope):
    x_ref, mods_ref, gpre_ref, w_ref, gq_ref, gk_ref = refs[:6]
    refs = refs[6:]
    if rope:
        cos_ref, sin_ref = refs[:2]
        refs = refs[2:]
    qk_ref, v_ref, h_buf, slab_buf = refs
    sh, sc, _ = _mod_slices(mods_ref, 0)
    h_buf[...] = (_rms_scale(x_ref[...]) * (gpre_ref[...] * (1.0 + sc)) + sh).astype(bf16)
    n_slabs = (n_q + N_KV_HEADS) // 2
    slab_cols = 2 * HEAD_DIM

    def project(i):
        return jnp.dot(h_buf[...], w_ref[:, i * slab_cols:(i + 1) * slab_cols], preferred_element_type=f32)

    gq = gq_ref[...] * Q_SCALE
    same_head = (lax.broadcasted_iota(jnp.int32, (slab_cols, slab_cols), 0) // HEAD_DIM
                 == lax.broadcasted_iota(jnp.int32, (slab_cols, slab_cols), 1) // HEAD_DIM)
    head_ones = same_head.astype(bf16)
    slab_buf[0] = project(0)
    for i in range(n_slabs):
        slab_buf[(i + 1) % 2] = project(i + 1)
        z = slab_buf[i % 2]
        ssq = jnp.dot((z * z).astype(bf16), head_ones, preferred_element_type=f32)
        zn = z * lax.rsqrt(ssq * (1.0 / HEAD_DIM) + EPS)
        for j in range(2):
            head = 2 * i + j
            y = zn[:, j * HEAD_DIM:(j + 1) * HEAD_DIM] * (gq if head < n_q else gk_ref[...])
            if rope:
                y = y * cos_ref[...] + pltpu.roll(y, HEAD_DIM // 2, 1) * sin_ref[...]
            qk_ref[head] = y.astype(bf16)
    ones_col = (lax.broadcasted_iota(jnp.int32, (h_buf.shape[0], HEAD_DIM), 1) == 0).astype(bf16)
    for j in range(N_KV_HEADS):
        v_ref[j, :, 0:HEAD_DIM] = slab_buf[n_slabs % 2, :, j * HEAD_DIM:(j + 1) * HEAD_DIM].astype(bf16)
        v_ref[j, :, HEAD_DIM:] = ones_col


def _qkv_project(x, mods, mod_row, g_pre, w, g_q, g_k, tile, n_q, rope):
    bsz, seq_len, _ = x.shape
    assert N_KV_HEADS == 2 and n_q % 2 == 0 and w.shape[1] == (n_q + 2 * N_KV_HEADS) * HEAD_DIM
    row = lambda d: pl.BlockSpec((1, d), lambda b, i: (0, 0))
    heads = lambda n, d: pl.BlockSpec((None, n, tile, d), lambda b, i: (b, 0, i, 0))
    args = [x, mods, g_pre, w, g_q, g_k]
    specs = [
        pl.BlockSpec((None, tile, D_MODEL), lambda b, i: (b, i, 0)),
        pl.BlockSpec((None, 1, N_MOD * D_MODEL), lambda b, i: (mod_row(b), 0, 0)),
        row(D_MODEL),
        pl.BlockSpec(w.shape, lambda b, i: (0, 0), pipeline_mode=pl.Buffered(1)),
        row(HEAD_DIM), row(HEAD_DIM),
    ]
    if rope:
        args += list(_rope_tables(seq_len))
        specs += [pl.BlockSpec((tile, HEAD_DIM), lambda b, i: (i, 0))] * 2
    n_qk = n_q + N_KV_HEADS
    return pl.pallas_call(
        functools.partial(_qkv_kernel, n_q=n_q, rope=rope),
        grid=(bsz, seq_len // tile),
        in_specs=specs,
        out_specs=[heads(n_qk, HEAD_DIM), heads(N_KV_HEADS, V_WIDTH)],
        out_shape=[jax.ShapeDtypeStruct((bsz, n_qk, seq_len, HEAD_DIM), bf16),
                   jax.ShapeDtypeStruct((bsz, N_KV_HEADS, seq_len, V_WIDTH), bf16)],
        scratch_shapes=[pltpu.VMEM((tile, D_MODEL), bf16), pltpu.VMEM((2, tile, 2 * HEAD_DIM), f32)],
        compiler_params=_params("arbitrary", "arbitrary"),
        name="qkv_project" if n_q else "kv_project",
    )(*args)


def _attn_kernel(q_ref, kc_ref, vc_ref, kl_ref, vl_ref, o_ref, *, tq, tk):
    rows = Q_PER_KV * tq
    q = q_ref[...].reshape(rows, HEAD_DIM)

    def step(k, v, carry):
        m, l, acc = carry
        s = lax.dot_general(q, k, (((1,), (1,)), ((), ())), preferred_element_type=f32)
        m_new = jnp.maximum(m, jnp.max(s, axis=-1, keepdims=True))
        alpha = jnp.exp2(m - m_new)
        p = jnp.exp2(s - m_new)
        l = alpha * l + jnp.sum(p, axis=-1, keepdims=True)
        acc = alpha * acc + jnp.dot(p.astype(bf16), v, preferred_element_type=f32)
        return m_new, l, acc

    carry = (jnp.full((rows, 1), -jnp.inf, f32), jnp.zeros((rows, 1), f32), jnp.zeros((rows, HEAD_DIM), f32))
    carry = step(kc_ref[...], vc_ref[:, 0:HEAD_DIM], carry)

    def body(j, carry):
        off = pl.multiple_of(j * tk, tk)
        return step(kl_ref[pl.ds(off, tk), :], vl_ref[pl.ds(off, tk), 0:HEAD_DIM], carry)

    _, l, acc = lax.fori_loop(0, kl_ref.shape[0] // tk, body, carry)
    out = (acc / l).astype(bf16)
    for g in range(Q_PER_KV):
        o_ref[:, g * HEAD_DIM:(g + 1) * HEAD_DIM] = out[g * tq:(g + 1) * tq]


def _attn_bounded_kernel(q_ref, kc_ref, vc_ref, kl_ref, vl_ref, o_ref, *, tq, tk):
    rows = Q_PER_KV * tq
    q = q_ref[...].reshape(rows, HEAD_DIM)
    chunks = [(kc_ref, vc_ref, c0) for c0 in range(0, kc_ref.shape[0], tk)]
    chunks += [(kl_ref, vl_ref, c0) for c0 in range(0, kl_ref.shape[0], tk)]
    acc = None
    for k_ref, v_ref, c0 in chunks:
        s = lax.dot_general(q, k_ref[c0:c0 + tk, :], (((1,), (1,)), ((), ())), preferred_element_type=f32)
        pv = jnp.dot(jnp.exp2(s).astype(bf16), v_ref[c0:c0 + tk, :], preferred_element_type=f32)
        acc = pv if acc is None else acc + pv
    out = (acc[:, 0:HEAD_DIM] / acc[:, HEAD_DIM:HEAD_DIM + 1]).astype(bf16)
    for g in range(Q_PER_KV):
        o_ref[:, g * HEAD_DIM:(g + 1) * HEAD_DIM] = out[g * tq:(g + 1) * tq]


def _attention(qk_lat, k_ctx, v_ctx, v_lat, bounded):
    bsz, _, seq_len, _ = qk_lat.shape
    ctx_len = k_ctx.shape[2]
    kv = lambda n, d, h0=0: pl.BlockSpec((None, None, n, d), lambda b, h, i: (b, h0 + h, 0, 0))
    tq = ATTN_TQ_BOUNDED if bounded else ATTN_TQ
    body = (functools.partial(_attn_bounded_kernel, tq=tq, tk=ATTN_TK_BOUNDED) if bounded
            else functools.partial(_attn_kernel, tq=tq, tk=ATTN_TK))
    return pl.pallas_call(
        body,
        grid=(bsz, N_KV_HEADS, seq_len // tq),
        in_specs=[
            pl.BlockSpec((None, Q_PER_KV, tq, HEAD_DIM), lambda b, h, i: (b, h, i, 0)),
            kv(ctx_len, HEAD_DIM), kv(ctx_len, V_WIDTH), kv(seq_len, HEAD_DIM, N_Q_HEADS), kv(seq_len, V_WIDTH),
        ],
        out_specs=pl.BlockSpec((None, tq, Q_PER_KV * HEAD_DIM), lambda b, h, i: (b, i, h)),
        out_shape=jax.ShapeDtypeStruct((bsz, seq_len, N_Q_HEADS * HEAD_DIM), bf16),
        compiler_params=_params("arbitrary", "arbitrary", "arbitrary"),
        name="attention_bounded" if bounded else "attention",
    )(qk_lat, k_ctx, v_ctx, qk_lat, v_lat)


def kernel(x, c, ctx, c_ctx, w_ada, b_ada, g_mix_pre, g_mix_post, g_mlp_pre, g_mlp_post, w_pool, pool_scale,
           w_qkv, g_q, g_k, w_o, w_mlp_in, w_mlp_out):
    bsz, seq_len, d = x.shape
    ctx_len = ctx.shape[1]
    assert d == D_MODEL and seq_len % ROW_TILE == 0 and bsz + 1 <= MOD_ROWS
    assert ctx_len % POOL_HALO == 0 and (bsz * ctx_len) % ROW_TILE == 0
    row = lambda v: v.reshape(1, -1)

    cond = jnp.zeros((MOD_ROWS, D_MODEL), f32).at[:bsz].set(c).at[bsz].set(c_ctx)
    mods = _ada_mods(cond, w_ada, b_ada).reshape(DEPTH, MOD_ROWS, 1, N_MOD * D_MODEL)
    lat_row = lambda b: b
    ctx_row = lambda b: bsz

    w_pool0 = w_pool[0].astype(bf16)
    w_in, w_out = w_mlp_in.astype(bf16), w_mlp_out.astype(bf16)
    layer0 = functools.partial(_pool_mlp_layer, mods=mods[0], g_mix_pre=row(g_mix_pre[0]),
                               g_mix_post=row(g_mix_post[0]), pool_scale=row(pool_scale[0]), w_pool=w_pool0,
                               g_mlp_pre=row(g_mlp_pre[0]), g_mlp_post=row(g_mlp_post[0]),
                               w_in=w_in, w_out=w_out, layer=0)
    x = layer0(x, mod_row=lat_row)
    ctx = layer0(ctx, mod_row=ctx_row)

    qk_cols = (N_Q_HEADS + N_KV_HEADS) * HEAD_DIM
    w_qkv1 = jnp.concatenate([_to_rope_lanes(w_qkv[0, :, :qk_cols]), w_qkv[0, :, qk_cols:]], axis=-1).astype(bf16)
    qkv = functools.partial(_qkv_project, mods=mods[1], g_pre=row(g_mix_pre[1]),
                            g_q=row(_to_rope_lanes(g_q[0])), g_k=row(_to_rope_lanes(g_k[0])))
    qk_lat, v_lat = qkv(x, mod_row=lat_row, w=w_qkv1, tile=ROW_TILE, n_q=N_Q_HEADS, rope=True)
    k_ctx, v_ctx = qkv(ctx, mod_row=ctx_row, w=w_qkv1[:, N_Q_HEADS * HEAD_DIM:], tile=ctx_len, n_q=0, rope=False)
    score_bound = 1.02 * Q_SCALE * HEAD_DIM * jnp.max(jnp.abs(g_q[0])) * jnp.max(jnp.abs(g_k[0]))
    attn_out = lax.cond(score_bound <= SCORE_LOG2_LIMIT,
                        functools.partial(_attention, bounded=True), functools.partial(_attention, bounded=False),
                        qk_lat, k_ctx, v_ctx, v_lat)
    return _mlp_sublayer(x, mods[1], lat_row, row(g_mlp_pre[1]), row(g_mlp_post[1]),
                         w_in, w_out, 1, proj=(attn_out, w_o.astype(bf16), row(g_mix_post[1])))
```

```python
import functools
import math

import numpy as np
import jax
import jax.numpy as jnp
from jax import lax
from jax.experimental import pallas as pl
from jax.experimental.pallas import tpu as pltpu

D_MODEL = 1024
DEPTH = 2
GRID_W = 64
POOL_WINDOWS = (2, 4, 8, 16)
POOL_GROUP_DIM = D_MODEL // len(POOL_WINDOWS)
POOL_HALO = 8
HEAD_DIM = 128
N_Q_HEADS = D_MODEL // HEAD_DIM
N_KV_HEADS = 2
V_WIDTH = 2 * HEAD_DIM
Q_PER_KV = N_Q_HEADS // N_KV_HEADS
ROPE_THETA = 10000.0
D_FF = 4 * D_MODEL
N_MOD = 6
EPS = 1e-6
MOD_ROWS = 8
Q_SCALE = (HEAD_DIM ** -0.5) * math.log2(math.e)

ROW_TILE = 512
QKV_TILE = 1024
MLP_TILE = 1024
MLP_SUB_ROWS = 512
ATTN_TQ = 128
ATTN_TK = 512
ATTN_TQ_BOUNDED = 512
ATTN_TK_BOUNDED = 256
SCORE_LOG2_LIMIT = 100.0
FF_CHUNK = 1024
ADA_TN = 1536
VMEM_LIMIT = 56 * 1024 * 1024

f32 = jnp.float32
bf16 = jnp.bfloat16


def _params(*semantics, flags=None):
    return pltpu.CompilerParams(dimension_semantics=semantics, vmem_limit_bytes=VMEM_LIMIT, flags=flags)


def _rms_scale(x):
    return x * lax.rsqrt(jnp.mean(x * x, axis=-1, keepdims=True) + EPS)


def _sq_relu(u):
    ub = jnp.maximum(u.astype(bf16), 0.0)
    return ub * ub


def _mod_slices(mods_ref, first):
    return [mods_ref[:, (first + j) * D_MODEL:(first + j + 1) * D_MODEL] for j in range(3)]


def _ada_kernel(c_ref, w_ref, b_ref, o_ref):
    c = c_ref[...]
    s = c * jax.nn.sigmoid(c)
    o_ref[...] = jnp.dot(s.astype(bf16), w_ref[...].astype(bf16), preferred_element_type=f32) + b_ref[...]


def _ada_mods(cond, w_ada, b_ada):
    n = N_MOD * D_MODEL
    return pl.pallas_call(
        _ada_kernel,
        grid=(DEPTH, n // ADA_TN),
        in_specs=[
            pl.BlockSpec((MOD_ROWS, D_MODEL), lambda i, j: (0, 0)),
            pl.BlockSpec((None, D_MODEL, ADA_TN), lambda i, j: (i, 0, j)),
            pl.BlockSpec((None, 1, ADA_TN), lambda i, j: (i, 0, j)),
        ],
        out_specs=pl.BlockSpec((None, MOD_ROWS, ADA_TN), lambda i, j: (i, 0, j)),
        out_shape=jax.ShapeDtypeStruct((DEPTH, MOD_ROWS, n), f32),
        compiler_params=_params("arbitrary", "arbitrary"),
        name="ada_mods",
    )(cond, w_ada, b_ada.reshape(DEPTH, 1, n))


def _pool_mlp_kernel(xp_ref, x_ref, xn_ref, mods_ref, prev_mods_ref, gmix_pre_ref, gmix_post_ref, ps_ref, wpool_ref,
                     gpre_ref, gpost_ref, win_ref, wout_ref, o_ref, hbuf, x1_buf, h_buf, *, tile, seq_len):
    s = pl.program_id(0)
    n_tiles = seq_len // tile
    i = jnp.minimum(s, pl.num_programs(0) - 2) % n_tiles

    def mlp_chunk(c, acc):
        cols = slice(c * FF_CHUNK, (c + 1) * FF_CHUNK)
        u = jnp.dot(h_buf[...], win_ref[:, cols], preferred_element_type=f32)
        part = jnp.dot(_sq_relu(u), wout_ref[cols, :], preferred_element_type=f32)
        return part if acc is None else acc + part

    sh1, sc1, gt1 = _mod_slices(mods_ref, 0)
    sh2, sc2, _ = _mod_slices(mods_ref, 3)
    mix_in_gain = gmix_pre_ref[...] * (1.0 + sc1)

    def hmod(xv):
        return _rms_scale(xv) * mix_in_gain + sh1

    def pool_group(g):
        w = POOL_WINDOWS[g]
        cols = slice(g * POOL_GROUP_DIM, (g + 1) * POOL_GROUP_DIM)
        acc = None
        for d in range(-(w // 2), w - w // 2):
            v = hbuf[POOL_HALO + d:POOL_HALO + d + tile, cols]
            acc = v if acc is None else acc + v
        def clipped_mean(r0):
            t = i * tile + r0 + lax.broadcasted_iota(jnp.int32, (POOL_HALO, 1), 0)
            cnt = jnp.minimum(t + (w - w // 2), seq_len) - jnp.maximum(t - w // 2, 0)
            return acc[r0:r0 + POOL_HALO] / cnt.astype(f32)

        mean = jnp.concatenate([clipped_mean(0), acc[POOL_HALO:tile - POOL_HALO] * (1.0 / w),
                                clipped_mean(tile - POOL_HALO)], axis=0)
        diff = mean - hbuf[POOL_HALO:POOL_HALO + tile, cols]
        return jnp.dot(diff.astype(bf16), wpool_ref[g], preferred_element_type=f32) * ps_ref[:, cols]

    def fill_hbuf():
        hbuf[POOL_HALO:POOL_HALO + tile, :] = hmod(x_ref[...])
        hbuf[0:POOL_HALO, :] = jnp.where(i > 0, hmod(xp_ref[...]), 0.0)
        hbuf[POOL_HALO + tile:, :] = jnp.where(i < n_tiles - 1, hmod(xn_ref[...]), 0.0)

    def step(with_mlp, with_pool):
        acc, ys = None, []
        vpu_pieces = [fill_hbuf, lambda: ys.extend([pool_group(0), pool_group(1)]),
                      lambda: ys.extend([pool_group(2), pool_group(3)])]
        for c in range(D_FF // FF_CHUNK):
            if with_mlp:
                acc = mlp_chunk(c, acc)
            if with_pool and c < len(vpu_pieces):
                vpu_pieces[c]()
        if with_pool:
            x1 = x_ref[...] + _rms_scale(jnp.concatenate(ys, axis=-1)) * (gt1 * gmix_post_ref[...])
            h = (_rms_scale(x1) * (gpre_ref[...] * (1.0 + sc2)) + sh2).astype(bf16)
        if with_mlp:
            gt2_prev = prev_mods_ref[:, 5 * D_MODEL:6 * D_MODEL]
            o_ref[...] = x1_buf[...] + _rms_scale(acc) * (gt2_prev * gpost_ref[...])
        if with_pool:
            x1_buf[...] = x1
            h_buf[...] = h

    last = pl.num_programs(0) - 1
    pl.when(s == 0)(functools.partial(step, False, True))
    pl.when(jnp.logical_and(s > 0, s < last))(functools.partial(step, True, True))
    pl.when(s == last)(functools.partial(step, True, False))


def _pool_mlp_layer(x, mods, mod_row, g_mix_pre, g_mix_post, pool_scale, w_pool, g_mlp_pre, g_mlp_post,
                    w_in, w_out, layer):
    bsz, seq_len, _ = x.shape
    tile = min(ROW_TILE, seq_len)
    assert seq_len % tile == 0 and tile % POOL_HALO == 0 and D_FF // FF_CHUNK == len(POOL_WINDOWS) == 4
    n_tiles = seq_len // tile
    total = bsz * n_tiles
    hb = tile // POOL_HALO
    last_halo = seq_len // POOL_HALO - 1
    cur = lambda s: jnp.minimum(s, total - 1)
    prev = lambda s: jnp.maximum(s - 1, 0)
    row = pl.BlockSpec((1, D_MODEL), lambda s: (0, 0))
    mods_spec = lambda step: pl.BlockSpec((None, 1, N_MOD * D_MODEL), lambda s: (mod_row(step(s) // n_tiles), 0, 0))
    tok = lambda step: pl.BlockSpec((None, tile, D_MODEL), lambda s: (step(s) // n_tiles, step(s) % n_tiles, 0))
    whole = lambda w, l: pl.BlockSpec((None,) + w.shape[1:], lambda s: (l, 0, 0), pipeline_mode=pl.Buffered(1))
    return pl.pallas_call(
        functools.partial(_pool_mlp_kernel, tile=tile, seq_len=seq_len),
        grid=(total + 1,),
        in_specs=[
            pl.BlockSpec((None, POOL_HALO, D_MODEL),
                         lambda s: (cur(s) // n_tiles, jnp.maximum(cur(s) % n_tiles * hb - 1, 0), 0)),
            tok(cur),
            pl.BlockSpec((None, POOL_HALO, D_MODEL),
                         lambda s: (cur(s) // n_tiles, jnp.minimum((cur(s) % n_tiles + 1) * hb, last_halo), 0)),
            mods_spec(cur), mods_spec(prev),
            row, row, row,
            pl.BlockSpec(w_pool.shape, lambda s: (0, 0, 0)),
            row, row,
            whole(w_in, layer), whole(w_out, layer),
        ],
        out_specs=tok(prev),
        out_shape=jax.ShapeDtypeStruct(x.shape, f32),
        scratch_shapes=[pltpu.VMEM((tile + 2 * POOL_HALO, D_MODEL), f32),
                        pltpu.VMEM((tile, D_MODEL), f32), pltpu.VMEM((tile, D_MODEL), bf16)],
        compiler_params=_params("arbitrary"),
        name="pool_mlp_layer",
    )(x, x, x, mods, mods, g_mix_pre, g_mix_post, pool_scale, w_pool, g_mlp_pre, g_mlp_post, w_in, w_out)


def _mlp_kernel(*refs, has_proj):
    if has_proj:
        x_ref, a_ref, wo_ref, gmix_ref, mods_ref, gpre_ref, gpost_ref, win_ref, wout_ref, o_ref = refs
    else:
        x_ref, mods_ref, gpre_ref, gpost_ref, win_ref, wout_ref, o_ref = refs
    sh, sc, gt = _mod_slices(mods_ref, 3)
    pre_gain = gpre_ref[...] * (1.0 + sc)
    post_gain = gt * gpost_ref[...]
    if has_proj:
        mix_gain = mods_ref[:, 2 * D_MODEL:3 * D_MODEL] * gmix_ref[...]
    n_sub = x_ref.shape[0] // MLP_SUB_ROWS
    rows = [slice(r * MLP_SUB_ROWS, (r + 1) * MLP_SUB_ROWS) for r in range(n_sub)]

    def pre(r, y):
        x = x_ref[rows[r], :]
        if has_proj:
            x = x + _rms_scale(y) * mix_gain
        return x, (_rms_scale(x) * pre_gain + sh).astype(bf16)

    def mlp_chunk(h, c, acc):
        cols = slice(c * FF_CHUNK, (c + 1) * FF_CHUNK)
        u = jnp.dot(h, win_ref[:, cols], preferred_element_type=f32)
        part = jnp.dot(_sq_relu(u), wout_ref[cols, :], preferred_element_type=f32)
        return part if acc is None else acc + part

    def post(r, x, acc):
        o_ref[rows[r], :] = x + _rms_scale(acc) * post_gain

    ys = [jnp.dot(a_ref[rows[r], :], wo_ref[...], preferred_element_type=f32) if has_proj else None
          for r in range(n_sub)]
    cur = pre(0, ys[0])
    done = None
    for r in range(n_sub):
        x, h = cur
        acc = mlp_chunk(h, 0, None)
        if r + 1 < n_sub:
            cur = pre(r + 1, ys[r + 1])
        if done is not None:
            post(*done)
        for c in range(1, D_FF // FF_CHUNK):
            acc = mlp_chunk(h, c, acc)
        done = (r, x, acc)
    post(*done)


def _mlp_sublayer(x, mods, mod_row, g_pre, g_post, w_in, w_out, layer, proj=None):
    bsz, seq_len, _ = x.shape
    tile = min(MLP_TILE, seq_len)
    assert seq_len % tile == 0 and tile % MLP_SUB_ROWS == 0
    tok = pl.BlockSpec((None, tile, D_MODEL), lambda b, i: (b, i, 0))
    row = pl.BlockSpec((1, D_MODEL), lambda b, i: (0, 0))
    whole = lambda w, l=0: pl.BlockSpec((None,) + w.shape[1:], lambda b, i: (l, 0, 0), pipeline_mode=pl.Buffered(1))
    mods_spec = pl.BlockSpec((None, 1, N_MOD * D_MODEL), lambda b, i: (mod_row(b), 0, 0))
    if proj is None:
        args = (x, mods, g_pre, g_post, w_in, w_out)
        specs = [tok, mods_spec, row, row, whole(w_in, layer), whole(w_out, layer)]
    else:
        attn_out, w_o, g_mix = proj
        args = (x, attn_out, w_o, g_mix, mods, g_pre, g_post, w_in, w_out)
        specs = [tok, tok, whole(w_o), row, mods_spec, row, row, whole(w_in, layer), whole(w_out, layer)]
    return pl.pallas_call(
        functools.partial(_mlp_kernel, has_proj=proj is not None),
        grid=(bsz, seq_len // tile),
        in_specs=specs,
        out_specs=tok,
        out_shape=jax.ShapeDtypeStruct(x.shape, f32),
        compiler_params=_params("arbitrary", "arbitrary"),
        name="proj_mlp_sublayer" if proj is not None else "mlp_sublayer",
    )(*args)


def _rope_tables(seq_len):
    half = HEAD_DIM // 2
    t = np.arange(seq_len)
    inv_freq = np.power(np.float32(ROPE_THETA), -np.arange(0, half, 2, dtype=np.float32) / np.float32(half))
    ang_r = (t // GRID_W).astype(np.float32)[:, None] * inv_freq
    ang_c = (t % GRID_W).astype(np.float32)[:, None] * inv_freq
    cos = np.concatenate([np.cos(ang_r), np.cos(ang_c)] * 2, axis=-1)
    sin = np.concatenate([-np.sin(ang_r), -np.sin(ang_c), np.sin(ang_r), np.sin(ang_c)], axis=-1)
    return jnp.asarray(cos, f32), jnp.asarray(sin, f32)


def _to_rope_lanes(a):
    quarters = a.reshape(a.shape[:-1] + (-1, 4, HEAD_DIM // 4))
    swapped = jnp.concatenate([quarters[..., 0:1, :], quarters[..., 2:3, :], quarters[..., 1:2, :],
                               quarters[..., 3:4, :]], axis=-2)
    return swapped.reshape(a.shape)


def _qkv_kernel(*refs, n_q, rope, n_casts):
    x_ref, mods_ref, gpre_ref, w_ref, gq_ref, gk_ref = refs[:6]
    refs = refs[6:]
    if rope:
        cos_ref, sin_ref = refs[:2]
        refs = refs[2:]
    cast_in, refs = refs[:n_casts], refs[n_casts:]
    qk_ref, v_ref = refs[:2]
    cast_out, (h_buf, slab_buf) = refs[2:2 + n_casts], refs[2 + n_casts:]
    for src, dst in zip(cast_in, cast_out):
        dst[...] = src[...].astype(bf16)
    sh, sc, _ = _mod_slices(mods_ref, 0)
    pre_gain = gpre_ref[...] * (1.0 + sc)
    sub = slab_buf.shape[1]
    n_sub = x_ref.shape[0] // sub
    for r in range(n_sub):
        rows = slice(r * sub, (r + 1) * sub)
        h_buf[rows, :] = (_rms_scale(x_ref[rows, :]) * pre_gain + sh).astype(bf16)
    n_slabs = (n_q + N_KV_HEADS) // 2
    slab_cols = 2 * HEAD_DIM
    gq = gq_ref[...] * Q_SCALE
    same_head = (lax.broadcasted_iota(jnp.int32, (slab_cols, slab_cols), 0) // HEAD_DIM
                 == lax.broadcasted_iota(jnp.int32, (slab_cols, slab_cols), 1) // HEAD_DIM)
    head_ones = same_head.astype(bf16)
    ones_col = (lax.broadcasted_iota(jnp.int32, (sub, HEAD_DIM), 1) == 0).astype(bf16)

    for r in range(n_sub):
        rows = slice(r * sub, (r + 1) * sub)

        def project(i):
            return jnp.dot(h_buf[rows, :], w_ref[:, i * slab_cols:(i + 1) * slab_cols], preferred_element_type=f32)

        slab_buf[2 * r] = project(0)
        for i in range(n_slabs):
            slab_buf[2 * r + (i + 1) % 2] = project(i + 1)
            z = slab_buf[2 * r + i % 2]
            ssq = jnp.dot((z * z).astype(bf16), head_ones, preferred_element_type=f32)
            zn = z * lax.rsqrt(ssq * (1.0 / HEAD_DIM) + EPS)
            for j in range(2):
                head = 2 * i + j
                y = zn[:, j * HEAD_DIM:(j + 1) * HEAD_DIM] * (gq if head < n_q else gk_ref[...])
                if rope:
                    y = y * cos_ref[rows, :] + pltpu.roll(y, HEAD_DIM // 2, 1) * sin_ref[rows, :]
                qk_ref[head, rows, :] = y.astype(bf16)
        for j in range(N_KV_HEADS):
            v_ref[j, rows, 0:HEAD_DIM] = slab_buf[2 * r + n_slabs % 2, :, j * HEAD_DIM:(j + 1) * HEAD_DIM].astype(bf16)
            v_ref[j, rows, HEAD_DIM:] = ones_col


def _qkv_project(x, mods, mod_row, g_pre, w, w_col0, g_q, g_k, tile, n_q, rope, casts=()):
    bsz, seq_len, _ = x.shape
    w_cols = (n_q + 2 * N_KV_HEADS) * HEAD_DIM
    sub = min(tile, ROW_TILE)
    assert N_KV_HEADS == 2 and n_q % 2 == 0 and w_col0 % w_cols == 0 and tile % sub == 0
    n_tiles = seq_len // tile
    steps = bsz * n_tiles
    row = lambda d: pl.BlockSpec((1, d), lambda b, i: (0, 0))
    heads = lambda n, d: pl.BlockSpec((None, n, tile, d), lambda b, i: (b, 0, i, 0))
    args = [x, mods, g_pre, w, g_q, g_k]
    specs = [
        pl.BlockSpec((None, tile, D_MODEL), lambda b, i: (b, i, 0)),
        pl.BlockSpec((None, 1, N_MOD * D_MODEL), lambda b, i: (mod_row(b), 0, 0)),
        row(D_MODEL),
        pl.BlockSpec((w.shape[0], w_cols), lambda b, i: (0, w_col0 // w_cols), pipeline_mode=pl.Buffered(1)),
        row(HEAD_DIM), row(HEAD_DIM),
    ]
    if rope:
        args += list(_rope_tables(seq_len))
        specs += [pl.BlockSpec((tile, HEAD_DIM), lambda b, i: (i, 0))] * 2
    n_qk = n_q + N_KV_HEADS
    out_specs = [heads(n_qk, HEAD_DIM), heads(N_KV_HEADS, V_WIDTH)]
    out_shape = [jax.ShapeDtypeStruct((bsz, n_qk, seq_len, HEAD_DIM), bf16),
                 jax.ShapeDtypeStruct((bsz, N_KV_HEADS, seq_len, V_WIDTH), bf16)]
    for weight, index in casts:
        _, rows, cols = weight.shape
        assert rows % (steps * 16) == 0
        args.append(weight)
        specs.append(pl.BlockSpec((None, rows // steps, cols), lambda b, i, l=index: (l, b * n_tiles + i, 0)))
        out_specs.append(pl.BlockSpec((None, rows // steps, cols), lambda b, i: (0, b * n_tiles + i, 0)))
        out_shape.append(jax.ShapeDtypeStruct((1, rows, cols), bf16))
    return pl.pallas_call(
        functools.partial(_qkv_kernel, n_q=n_q, rope=rope, n_casts=len(casts)),
        grid=(bsz, n_tiles),
        in_specs=specs,
        out_specs=out_specs,
        out_shape=out_shape,
        scratch_shapes=[pltpu.VMEM((tile, D_MODEL), bf16), pltpu.VMEM((2 * (tile // sub), sub, 2 * HEAD_DIM), f32)],
        compiler_params=_params("arbitrary", "arbitrary"),
        name="qkv_project" if n_q else "kv_project",
    )(*args)


def _attn_kernel(q_ref, kc_ref, vc_ref, kl_ref, vl_ref, o_ref, *, tq, tk):
    rows = Q_PER_KV * tq
    q = q_ref[...].reshape(rows, HEAD_DIM)

    def step(k, v, carry):
        m, l, acc = carry
        s = lax.dot_general(q, k, (((1,), (1,)), ((), ())), preferred_element_type=f32)
        m_new = jnp.maximum(m, jnp.max(s, axis=-1, keepdims=True))
        alpha = jnp.exp2(m - m_new)
        p = jnp.exp2(s - m_new)
        l = alpha * l + jnp.sum(p, axis=-1, keepdims=True)
        acc = alpha * acc + jnp.dot(p.astype(bf16), v, preferred_element_type=f32)
        return m_new, l, acc

    carry = (jnp.full((rows, 1), -jnp.inf, f32), jnp.zeros((rows, 1), f32), jnp.zeros((rows, HEAD_DIM), f32))
    carry = step(kc_ref[...], vc_ref[:, 0:HEAD_DIM], carry)

    def body(j, carry):
        off = pl.multiple_of(j * tk, tk)
        return step(kl_ref[pl.ds(off, tk), :], vl_ref[pl.ds(off, tk), 0:HEAD_DIM], carry)

    _, l, acc = lax.fori_loop(0, kl_ref.shape[0] // tk, body, carry)
    out = (acc / l).astype(bf16)
    for g in range(Q_PER_KV):
        o_ref[:, g * HEAD_DIM:(g + 1) * HEAD_DIM] = out[g * tq:(g + 1) * tq]


def _attn_bounded_kernel(q_ref, kc_ref, vc_ref, kl_ref, vl_ref, o_ref, *, tq, tk):
    rows = Q_PER_KV * tq
    q = q_ref[...].reshape(rows, HEAD_DIM)
    chunks = [(kc_ref, vc_ref, c0) for c0 in range(0, kc_ref.shape[0], tk)]
    chunks += [(kl_ref, vl_ref, c0) for c0 in range(0, kl_ref.shape[0], tk)]
    acc = None
    for k_ref, v_ref, c0 in chunks:
        s = lax.dot_general(q, k_ref[c0:c0 + tk, :], (((1,), (1,)), ((), ())), preferred_element_type=f32)
        pv = jnp.dot(jnp.exp2(s).astype(bf16), v_ref[c0:c0 + tk, :], preferred_element_type=f32)
        acc = pv if acc is None else acc + pv
    out = (acc[:, 0:HEAD_DIM] / acc[:, HEAD_DIM:HEAD_DIM + 1]).astype(bf16)
    for g in range(Q_PER_KV):
        o_ref[:, g * HEAD_DIM:(g + 1) * HEAD_DIM] = out[g * tq:(g + 1) * tq]


def _attention(qk_lat, k_ctx, v_ctx, v_lat, bounded):
    bsz, _, seq_len, _ = qk_lat.shape
    ctx_len = k_ctx.shape[2]
    kv = lambda n, d, h0=0: pl.BlockSpec((None, None, n, d), lambda b, h, i: (b, h0 + h, 0, 0))
    tq = ATTN_TQ_BOUNDED if bounded else ATTN_TQ
    body = (functools.partial(_attn_bounded_kernel, tq=tq, tk=ATTN_TK_BOUNDED) if bounded
            else functools.partial(_attn_kernel, tq=tq, tk=ATTN_TK))
    return pl.pallas_call(
        body,
        grid=(bsz, N_KV_HEADS, seq_len // tq),
        in_specs=[
            pl.BlockSpec((None, Q_PER_KV, tq, HEAD_DIM), lambda b, h, i: (b, h, i, 0)),
            kv(ctx_len, HEAD_DIM), kv(ctx_len, V_WIDTH), kv(seq_len, HEAD_DIM, N_Q_HEADS), kv(seq_len, V_WIDTH),
        ],
        out_specs=pl.BlockSpec((None, tq, Q_PER_KV * HEAD_DIM), lambda b, h, i: (b, i, h)),
        out_shape=jax.ShapeDtypeStruct((bsz, seq_len, N_Q_HEADS * HEAD_DIM), bf16),
        compiler_params=_params("arbitrary", "arbitrary", "arbitrary"),
        name="attention_bounded" if bounded else "attention",
    )(qk_lat, k_ctx, v_ctx, qk_lat, v_lat)


def kernel(x, c, ctx, c_ctx, w_ada, b_ada, g_mix_pre, g_mix_post, g_mlp_pre, g_mlp_post, w_pool, pool_scale,
           w_qkv, g_q, g_k, w_o, w_mlp_in, w_mlp_out):
    bsz, seq_len, d = x.shape
    ctx_len = ctx.shape[1]
    assert d == D_MODEL and seq_len % ROW_TILE == 0 and bsz + 1 <= MOD_ROWS
    assert ctx_len % POOL_HALO == 0 and (bsz * ctx_len) % ROW_TILE == 0
    row = lambda v: v.reshape(1, -1)

    cond = jnp.zeros((MOD_ROWS, D_MODEL), f32).at[:bsz].set(c).at[bsz].set(c_ctx)
    mods = _ada_mods(cond, w_ada, b_ada).reshape(DEPTH, MOD_ROWS, 1, N_MOD * D_MODEL)
    lat_row = lambda b: b
    ctx_row = lambda b: bsz

    w_pool0 = w_pool[0].astype(bf16)
    layer0 = functools.partial(_pool_mlp_layer, mods=mods[0], g_mix_pre=row(g_mix_pre[0]),
                               g_mix_post=row(g_mix_post[0]), pool_scale=row(pool_scale[0]), w_pool=w_pool0,
                               g_mlp_pre=row(g_mlp_pre[0]), g_mlp_post=row(g_mlp_post[0]),
                               w_in=w_mlp_in[0:1].astype(bf16), w_out=w_mlp_out[0:1].astype(bf16), layer=0)
    x = layer0(x, mod_row=lat_row)
    ctx = layer0(ctx, mod_row=ctx_row)

    qk_cols = (N_Q_HEADS + N_KV_HEADS) * HEAD_DIM
    w_qkv1 = jnp.concatenate([_to_rope_lanes(w_qkv[0, :, :qk_cols]), w_qkv[0, :, qk_cols:]], axis=-1).astype(bf16)
    qkv = functools.partial(_qkv_project, mods=mods[1], g_pre=row(g_mix_pre[1]),
                            g_q=row(_to_rope_lanes(g_q[0])), g_k=row(_to_rope_lanes(g_k[0])))
    qk_lat, v_lat, w_in1, w_out1, w_o1 = qkv(x, mod_row=lat_row, w=w_qkv1, w_col0=0, tile=QKV_TILE, n_q=N_Q_HEADS,
                                             rope=True, casts=((w_mlp_in, 1), (w_mlp_out, 1), (w_o, 0)))
    k_ctx, v_ctx = qkv(ctx, mod_row=ctx_row, w=w_qkv1, w_col0=N_Q_HEADS * HEAD_DIM, tile=ctx_len, n_q=0, rope=False)
    score_bound = 1.02 * Q_SCALE * HEAD_DIM * jnp.max(jnp.abs(g_q[0])) * jnp.max(jnp.abs(g_k[0]))
    attn_out = lax.cond(score_bound <= SCORE_LOG2_LIMIT,
                        functools.partial(_attention, bounded=True), functools.partial(_attention, bounded=False),
                        qk_lat, k_ctx, v_ctx, v_lat)
    return _mlp_sublayer(x, mods[1], lat_row, row(g_mlp_pre[1]), row(g_mlp_post[1]),
                         w_in1, w_out1, 0, proj=(attn_out, w_o1, row(g_mix_post[1])))
```

```python
import functools
import math

import numpy as np
import jax
import jax.numpy as jnp
from jax import lax
from jax.experimental import pallas as pl
from jax.experimental.pallas import tpu as pltpu

D_MODEL = 1024
DEPTH = 2
GRID_W = 64
POOL_WINDOWS = (2, 4, 8, 16)
POOL_GROUP_DIM = D_MODEL // len(POOL_WINDOWS)
POOL_HALO = 8
HEAD_DIM = 128
N_Q_HEADS = D_MODEL // HEAD_DIM
N_KV_HEADS = 2
V_WIDTH = 2 * HEAD_DIM
Q_PER_KV = N_Q_HEADS // N_KV_HEADS
ROPE_THETA = 10000.0
D_FF = 4 * D_MODEL
N_MOD = 6
EPS = 1e-6
MOD_ROWS = 8
Q_SCALE = (HEAD_DIM ** -0.5) * math.log2(math.e)

ROW_TILE = 512
QKV_TILE = 1024
MLP_TILE = 1024
MLP_SUB_ROWS = 512
ATTN_TQ = 128
ATTN_TK = 512
ATTN_TQ_BOUNDED = 512
ATTN_TK_BOUNDED = 256
SCORE_LOG2_LIMIT = 100.0
FF_CHUNK = 1024
ADA_TN = 1536
VMEM_LIMIT = 56 * 1024 * 1024

f32 = jnp.float32
bf16 = jnp.bfloat16


def _params(*semantics, flags=None):
    return pltpu.CompilerParams(dimension_semantics=semantics, vmem_limit_bytes=VMEM_LIMIT, flags=flags)


def _rms_scale(x):
    return x * lax.rsqrt(jnp.mean(x * x, axis=-1, keepdims=True) + EPS)


def _sq_relu(u):
    ub = jnp.maximum(u.astype(bf16), 0.0)
    return ub * ub


def _mod_slices(mods_ref, first):
    return [mods_ref[:, (first + j) * D_MODEL:(first + j + 1) * D_MODEL] for j in range(3)]


def _ada_kernel(c_ref, w_ref, b_ref, o_ref):
    c = c_ref[...]
    s = c * jax.nn.sigmoid(c)
    o_ref[...] = jnp.dot(s.astype(bf16), w_ref[...].astype(bf16), preferred_element_type=f32) + b_ref[...]


def _ada_mods(cond, w_ada, b_ada):
    n = N_MOD * D_MODEL
    return pl.pallas_call(
        _ada_kernel,
        grid=(DEPTH, n // ADA_TN),
        in_specs=[
            pl.BlockSpec((MOD_ROWS, D_MODEL), lambda i, j: (0, 0)),
            pl.BlockSpec((None, D_MODEL, ADA_TN), lambda i, j: (i, 0, j)),
            pl.BlockSpec((None, 1, ADA_TN), lambda i, j: (i, 0, j)),
        ],
        out_specs=pl.BlockSpec((None, MOD_ROWS, ADA_TN), lambda i, j: (i, 0, j)),
        out_shape=jax.ShapeDtypeStruct((DEPTH, MOD_ROWS, n), f32),
        compiler_params=_params("arbitrary", "arbitrary"),
        name="ada_mods",
    )(cond, w_ada, b_ada.reshape(DEPTH, 1, n))


def _pool_mlp_kernel(xp_ref, x_ref, xn_ref, mods_ref, prev_mods_ref, gmix_pre_ref, gmix_post_ref, ps_ref, wpool_ref,
                     gpre_ref, gpost_ref, win_ref, wout_ref, o_ref, hbuf, x1_buf, h_buf, *level_bufs, tile, seq_len):
    s = pl.program_id(0)
    n_tiles = seq_len // tile
    i = jnp.minimum(s, pl.num_programs(0) - 2) % n_tiles

    def mlp_chunk(c, acc):
        cols = slice(c * FF_CHUNK, (c + 1) * FF_CHUNK)
        u = jnp.dot(h_buf[...], win_ref[:, cols], preferred_element_type=f32)
        part = jnp.dot(_sq_relu(u), wout_ref[cols, :], preferred_element_type=f32)
        return part if acc is None else acc + part

    sh1, sc1, gt1 = _mod_slices(mods_ref, 0)
    sh2, sc2, _ = _mod_slices(mods_ref, 3)
    mix_in_gain = gmix_pre_ref[...] * (1.0 + sc1)

    def hmod(xv):
        return _rms_scale(xv) * mix_in_gain + sh1

    ext = tile + 2 * POOL_HALO
    n_groups = len(POOL_WINDOWS)

    def zero_level_padding():
        for buf in (hbuf,) + tuple(level_bufs):
            buf[ext:, :] = jnp.zeros((POOL_HALO, buf.shape[1]), f32)

    def build_level(k):
        src = hbuf if k == 1 else level_bufs[k - 2]
        lane0 = 0 if k == 1 else POOL_GROUP_DIM
        rows = ext if k < n_groups else tile
        summed = src[0:rows, lane0:] + src[2 ** (k - 1):2 ** (k - 1) + rows, lane0:]
        if k == n_groups:
            return summed
        level_bufs[k - 1][0:ext, :] = summed

    def pool_group(g, top_level=None):
        w = POOL_WINDOWS[g]
        cols = slice(g * POOL_GROUP_DIM, (g + 1) * POOL_GROUP_DIM)
        acc = (top_level if g == n_groups - 1 else
               level_bufs[g][POOL_HALO - w // 2:POOL_HALO - w // 2 + tile, 0:POOL_GROUP_DIM])
        def clipped_mean(r0):
            t = i * tile + r0 + lax.broadcasted_iota(jnp.int32, (POOL_HALO, 1), 0)
            cnt = jnp.minimum(t + (w - w // 2), seq_len) - jnp.maximum(t - w // 2, 0)
            return acc[r0:r0 + POOL_HALO] / cnt.astype(f32)

        mean = jnp.concatenate([clipped_mean(0), acc[POOL_HALO:tile - POOL_HALO] * (1.0 / w),
                                clipped_mean(tile - POOL_HALO)], axis=0)
        diff = mean - hbuf[POOL_HALO:POOL_HALO + tile, cols]
        return jnp.dot(diff.astype(bf16), wpool_ref[g], preferred_element_type=f32) * ps_ref[:, cols]

    def fill_hbuf():
        hbuf[POOL_HALO:POOL_HALO + tile, :] = hmod(x_ref[...])
        hbuf[0:POOL_HALO, :] = jnp.where(i > 0, hmod(xp_ref[...]), 0.0)
        hbuf[POOL_HALO + tile:ext, :] = jnp.where(i < n_tiles - 1, hmod(xn_ref[...]), 0.0)

    def step(with_mlp, with_pool):
        acc, ys = None, []

        def first_groups():
            build_level(2)
            ys.extend([pool_group(0), pool_group(1)])

        def last_groups():
            build_level(3)
            ys.extend([pool_group(2), pool_group(3, build_level(4))])

        vpu_pieces = [lambda: (fill_hbuf(), build_level(1)), first_groups, last_groups]
        if not with_mlp:
            zero_level_padding()
        for c in range(D_FF // FF_CHUNK):
            if with_mlp:
                acc = mlp_chunk(c, acc)
            if with_pool and c < len(vpu_pieces):
                vpu_pieces[c]()
        if with_pool:
            x1 = x_ref[...] + _rms_scale(jnp.concatenate(ys, axis=-1)) * (gt1 * gmix_post_ref[...])
            h = (_rms_scale(x1) * (gpre_ref[...] * (1.0 + sc2)) + sh2).astype(bf16)
        if with_mlp:
            gt2_prev = prev_mods_ref[:, 5 * D_MODEL:6 * D_MODEL]
            o_ref[...] = x1_buf[...] + _rms_scale(acc) * (gt2_prev * gpost_ref[...])
        if with_pool:
            x1_buf[...] = x1
            h_buf[...] = h

    last = pl.num_programs(0) - 1
    pl.when(s == 0)(functools.partial(step, False, True))
    pl.when(jnp.logical_and(s > 0, s < last))(functools.partial(step, True, True))
    pl.when(s == last)(functools.partial(step, True, False))


def _pool_mlp_layer(x, mods, mod_row, g_mix_pre, g_mix_post, pool_scale, w_pool, g_mlp_pre, g_mlp_post,
                    w_in, w_out, layer):
    bsz, seq_len, _ = x.shape
    tile = min(ROW_TILE, seq_len)
    n_groups = len(POOL_WINDOWS)
    assert seq_len % tile == 0 and tile % POOL_HALO == 0 and D_FF // FF_CHUNK == n_groups == 4
    assert POOL_WINDOWS == tuple(2 ** (g + 1) for g in range(n_groups)) and POOL_WINDOWS[-1] == 2 * POOL_HALO
    n_tiles = seq_len // tile
    total = bsz * n_tiles
    hb = tile // POOL_HALO
    last_halo = seq_len // POOL_HALO - 1
    cur = lambda s: jnp.minimum(s, total - 1)
    prev = lambda s: jnp.maximum(s - 1, 0)
    row = pl.BlockSpec((1, D_MODEL), lambda s: (0, 0))
    mods_spec = lambda step: pl.BlockSpec((None, 1, N_MOD * D_MODEL), lambda s: (mod_row(step(s) // n_tiles), 0, 0))
    tok = lambda step: pl.BlockSpec((None, tile, D_MODEL), lambda s: (step(s) // n_tiles, step(s) % n_tiles, 0))
    whole = lambda w, l: pl.BlockSpec((None,) + w.shape[1:], lambda s: (l, 0, 0), pipeline_mode=pl.Buffered(1))
    return pl.pallas_call(
        functools.partial(_pool_mlp_kernel, tile=tile, seq_len=seq_len),
        grid=(total + 1,),
        in_specs=[
            pl.BlockSpec((None, POOL_HALO, D_MODEL),
                         lambda s: (cur(s) // n_tiles, jnp.maximum(cur(s) % n_tiles * hb - 1, 0), 0)),
            tok(cur),
            pl.BlockSpec((None, POOL_HALO, D_MODEL),
                         lambda s: (cur(s) // n_tiles, jnp.minimum((cur(s) % n_tiles + 1) * hb, last_halo), 0)),
            mods_spec(cur), mods_spec(prev),
            row, row, row,
            pl.BlockSpec(w_pool.shape, lambda s: (0, 0, 0)),
            row, row,
            whole(w_in, layer), whole(w_out, layer),
        ],
        out_specs=tok(prev),
        out_shape=jax.ShapeDtypeStruct(x.shape, f32),
        scratch_shapes=[pltpu.VMEM((tile + 3 * POOL_HALO, D_MODEL), f32),
                        pltpu.VMEM((tile, D_MODEL), f32), pltpu.VMEM((tile, D_MODEL), bf16)]
        + [pltpu.VMEM((tile + 3 * POOL_HALO, D_MODEL - k * POOL_GROUP_DIM), f32) for k in range(n_groups - 1)],
        compiler_params=_params("arbitrary"),
        name="pool_mlp_layer",
    )(x, x, x, mods, mods, g_mix_pre, g_mix_post, pool_scale, w_pool, g_mlp_pre, g_mlp_post, w_in, w_out)


def _mlp_kernel(*refs, has_proj):
    if has_proj:
        x_ref, a_ref, wo_ref, gmix_ref, mods_ref, gpre_ref, gpost_ref, win_ref, wout_ref, o_ref = refs
    else:
        x_ref, mods_ref, gpre_ref, gpost_ref, win_ref, wout_ref, o_ref = refs
    sh, sc, gt = _mod_slices(mods_ref, 3)
    pre_gain = gpre_ref[...] * (1.0 + sc)
    post_gain = gt * gpost_ref[...]
    if has_proj:
        mix_gain = mods_ref[:, 2 * D_MODEL:3 * D_MODEL] * gmix_ref[...]
    n_sub = x_ref.shape[0] // MLP_SUB_ROWS
    rows = [slice(r * MLP_SUB_ROWS, (r + 1) * MLP_SUB_ROWS) for r in range(n_sub)]

    def pre(r, y):
        x = x_ref[rows[r], :]
        if has_proj:
            x = x + _rms_scale(y) * mix_gain
        return x, (_rms_scale(x) * pre_gain + sh).astype(bf16)

    def mlp_chunk(h, c, acc):
        cols = slice(c * FF_CHUNK, (c + 1) * FF_CHUNK)
        u = jnp.dot(h, win_ref[:, cols], preferred_element_type=f32)
        part = jnp.dot(_sq_relu(u), wout_ref[cols, :], preferred_element_type=f32)
        return part if acc is None else acc + part

    def post(r, x, acc):
        o_ref[rows[r], :] = x + _rms_scale(acc) * post_gain

    ys = [jnp.dot(a_ref[rows[r], :], wo_ref[...], preferred_element_type=f32) if has_proj else None
          for r in range(n_sub)]
    cur = pre(0, ys[0])
    done = None
    for r in range(n_sub):
        x, h = cur
        acc = mlp_chunk(h, 0, None)
        if r + 1 < n_sub:
            cur = pre(r + 1, ys[r + 1])
        if done is not None:
            post(*done)
        for c in range(1, D_FF // FF_CHUNK):
            acc = mlp_chunk(h, c, acc)
        done = (r, x, acc)
    post(*done)


def _mlp_sublayer(x, mods, mod_row, g_pre, g_post, w_in, w_out, layer, proj=None):
    bsz, seq_len, _ = x.shape
    tile = min(MLP_TILE, seq_len)
    assert seq_len % tile == 0 and tile % MLP_SUB_ROWS == 0
    tok = pl.BlockSpec((None, tile, D_MODEL), lambda b, i: (b, i, 0))
    row = pl.BlockSpec((1, D_MODEL), lambda b, i: (0, 0))
    whole = lambda w, l=0: pl.BlockSpec((None,) + w.shape[1:], lambda b, i: (l, 0, 0), pipeline_mode=pl.Buffered(1))
    mods_spec = pl.BlockSpec((None, 1, N_MOD * D_MODEL), lambda b, i: (mod_row(b), 0, 0))
    if proj is None:
        args = (x, mods, g_pre, g_post, w_in, w_out)
        specs = [tok, mods_spec, row, row, whole(w_in, layer), whole(w_out, layer)]
    else:
        attn_out, w_o, g_mix = proj
        args = (x, attn_out, w_o, g_mix, mods, g_pre, g_post, w_in, w_out)
        specs = [tok, tok, whole(w_o), row, mods_spec, row, row, whole(w_in, layer), whole(w_out, layer)]
    return pl.pallas_call(
        functools.partial(_mlp_kernel, has_proj=proj is not None),
        grid=(bsz, seq_len // tile),
        in_specs=specs,
        out_specs=tok,
        out_shape=jax.ShapeDtypeStruct(x.shape, f32),
        compiler_params=_params("arbitrary", "arbitrary"),
        name="proj_mlp_sublayer" if proj is not None else "mlp_sublayer",
    )(*args)


def _rope_tables(seq_len):
    half = HEAD_DIM // 2
    t = np.arange(seq_len)
    inv_freq = np.power(np.float32(ROPE_THETA), -np.arange(0, half, 2, dtype=np.float32) / np.float32(half))
    ang_r = (t // GRID_W).astype(np.float32)[:, None] * inv_freq
    ang_c = (t % GRID_W).astype(np.float32)[:, None] * inv_freq
    cos = np.concatenate([np.cos(ang_r), np.cos(ang_c)] * 2, axis=-1)
    sin = np.concatenate([-np.sin(ang_r), -np.sin(ang_c), np.sin(ang_r), np.sin(ang_c)], axis=-1)
    return jnp.asarray(cos, f32), jnp.asarray(sin, f32)


def _to_rope_lanes(a):
    quarters = a.reshape(a.shape[:-1] + (-1, 4, HEAD_DIM // 4))
    swapped = jnp.concatenate([quarters[..., 0:1, :], quarters[..., 2:3, :], quarters[..., 1:2, :],
                               quarters[..., 3:4, :]], axis=-2)
    return swapped.reshape(a.shape)


def _qkv_kernel(*refs, n_q, rope, n_casts):
    x_ref, mods_ref, gpre_ref, w_ref, gq_ref, gk_ref = refs[:6]
    refs = refs[6:]
    if rope:
        cos_ref, sin_ref = refs[:2]
        refs = refs[2:]
    cast_in, refs = refs[:n_casts], refs[n_casts:]
    qk_ref, v_ref = refs[:2]
    cast_out, (h_buf, slab_buf) = refs[2:2 + n_casts], refs[2 + n_casts:]
    for src, dst in zip(cast_in, cast_out):
        dst[...] = src[...].astype(bf16)
    sh, sc, _ = _mod_slices(mods_ref, 0)
    pre_gain = gpre_ref[...] * (1.0 + sc)
    sub = slab_buf.shape[1]
    n_sub = x_ref.shape[0] // sub
    for r in range(n_sub):
        rows = slice(r * sub, (r + 1) * sub)
        h_buf[rows, :] = (_rms_scale(x_ref[rows, :]) * pre_gain + sh).astype(bf16)
    n_slabs = (n_q + N_KV_HEADS) // 2
    slab_cols = 2 * HEAD_DIM
    gq = gq_ref[...] * Q_SCALE
    same_head = (lax.broadcasted_iota(jnp.int32, (slab_cols, slab_cols), 0) // HEAD_DIM
                 == lax.broadcasted_iota(jnp.int32, (slab_cols, slab_cols), 1) // HEAD_DIM)
    head_ones = same_head.astype(bf16)
    ones_col = (lax.broadcasted_iota(jnp.int32, (sub, HEAD_DIM), 1) == 0).astype(bf16)

    for r in range(n_sub):
        rows = slice(r * sub, (r + 1) * sub)

        def project(i):
            return jnp.dot(h_buf[rows, :], w_ref[:, i * slab_cols:(i + 1) * slab_cols], preferred_element_type=f32)

        slab_buf[2 * r] = project(0)
        for i in range(n_slabs):
            slab_buf[2 * r + (i + 1) % 2] = project(i + 1)
            z = slab_buf[2 * r + i % 2]
            ssq = jnp.dot((z * z).astype(bf16), head_ones, preferred_element_type=f32)
            zn = z * lax.rsqrt(ssq * (1.0 / HEAD_DIM) + EPS)
            for j in range(2):
                head = 2 * i + j
                y = zn[:, j * HEAD_DIM:(j + 1) * HEAD_DIM] * (gq if head < n_q else gk_ref[...])
                if rope:
                    y = y * cos_ref[rows, :] + pltpu.roll(y, HEAD_DIM // 2, 1) * sin_ref[rows, :]
                qk_ref[head, rows, :] = y.astype(bf16)
        for j in range(N_KV_HEADS):
            v_ref[j, rows, 0:HEAD_DIM] = slab_buf[2 * r + n_slabs % 2, :, j * HEAD_DIM:(j + 1) * HEAD_DIM].astype(bf16)
            v_ref[j, rows, HEAD_DIM:] = ones_col


def _qkv_project(x, mods, mod_row, g_pre, w, w_col0, g_q, g_k, tile, n_q, rope, casts=()):
    bsz, seq_len, _ = x.shape
    w_cols = (n_q + 2 * N_KV_HEADS) * HEAD_DIM
    sub = min(tile, ROW_TILE)
    assert N_KV_HEADS == 2 and n_q % 2 == 0 and w_col0 % w_cols == 0 and tile % sub == 0
    n_tiles = seq_len // tile
    steps = bsz * n_tiles
    row = lambda d: pl.BlockSpec((1, d), lambda b, i: (0, 0))
    heads = lambda n, d: pl.BlockSpec((None, n, tile, d), lambda b, i: (b, 0, i, 0))
    args = [x, mods, g_pre, w, g_q, g_k]
    specs = [
        pl.BlockSpec((None, tile, D_MODEL), lambda b, i: (b, i, 0)),
        pl.BlockSpec((None, 1, N_MOD * D_MODEL), lambda b, i: (mod_row(b), 0, 0)),
        row(D_MODEL),
        pl.BlockSpec((w.shape[0], w_cols), lambda b, i: (0, w_col0 // w_cols), pipeline_mode=pl.Buffered(1)),
        row(HEAD_DIM), row(HEAD_DIM),
    ]
    if rope:
        args += list(_rope_tables(seq_len))
        specs += [pl.BlockSpec((tile, HEAD_DIM), lambda b, i: (i, 0))] * 2
    n_qk = n_q + N_KV_HEADS
    out_specs = [heads(n_qk, HEAD_DIM), heads(N_KV_HEADS, V_WIDTH)]
    out_shape = [jax.ShapeDtypeStruct((bsz, n_qk, seq_len, HEAD_DIM), bf16),
                 jax.ShapeDtypeStruct((bsz, N_KV_HEADS, seq_len, V_WIDTH), bf16)]
    for weight, index in casts:
        _, rows, cols = weight.shape
        assert rows % (steps * 16) == 0
        args.append(weight)
        specs.append(pl.BlockSpec((None, rows // steps, cols), lambda b, i, l=index: (l, b * n_tiles + i, 0)))
        out_specs.append(pl.BlockSpec((None, rows // steps, cols), lambda b, i: (0, b * n_tiles + i, 0)))
        out_shape.append(jax.ShapeDtypeStruct((1, rows, cols), bf16))
    return pl.pallas_call(
        functools.partial(_qkv_kernel, n_q=n_q, rope=rope, n_casts=len(casts)),
        grid=(bsz, n_tiles),
        in_specs=specs,
        out_specs=out_specs,
        out_shape=out_shape,
        scratch_shapes=[pltpu.VMEM((tile, D_MODEL), bf16), pltpu.VMEM((2 * (tile // sub), sub, 2 * HEAD_DIM), f32)],
        compiler_params=_params("arbitrary", "arbitrary"),
        name="qkv_project" if n_q else "kv_project",
    )(*args)


def _attn_kernel(q_ref, kc_ref, vc_ref, kl_ref, vl_ref, o_ref, *, tq, tk):
    rows = Q_PER_KV * tq
    q = q_ref[...].reshape(rows, HEAD_DIM)

    def step(k, v, carry):
        m, l, acc = carry
        s = lax.dot_general(q, k, (((1,), (1,)), ((), ())), preferred_element_type=f32)
        m_new = jnp.maximum(m, jnp.max(s, axis=-1, keepdims=True))
        alpha = jnp.exp2(m - m_new)
        p = jnp.exp2(s - m_new)
        l = alpha * l + jnp.sum(p, axis=-1, keepdims=True)
        acc = alpha * acc + jnp.dot(p.astype(bf16), v, preferred_element_type=f32)
        return m_new, l, acc

    carry = (jnp.full((rows, 1), -jnp.inf, f32), jnp.zeros((rows, 1), f32), jnp.zeros((rows, HEAD_DIM), f32))
    carry = step(kc_ref[...], vc_ref[:, 0:HEAD_DIM], carry)

    def body(j, carry):
        off = pl.multiple_of(j * tk, tk)
        return step(kl_ref[pl.ds(off, tk), :], vl_ref[pl.ds(off, tk), 0:HEAD_DIM], carry)

    _, l, acc = lax.fori_loop(0, kl_ref.shape[0] // tk, body, carry)
    out = (acc / l).astype(bf16)
    for g in range(Q_PER_KV):
        o_ref[:, g * HEAD_DIM:(g + 1) * HEAD_DIM] = out[g * tq:(g + 1) * tq]


def _attn_bounded_kernel(q_ref, kc_ref, vc_ref, kl_ref, vl_ref, o_ref, *, tq, tk):
    rows = Q_PER_KV * tq
    q = q_ref[...].reshape(rows, HEAD_DIM)
    chunks = [(kc_ref, vc_ref, c0) for c0 in range(0, kc_ref.shape[0], tk)]
    chunks += [(kl_ref, vl_ref, c0) for c0 in range(0, kl_ref.shape[0], tk)]
    acc = None
    for k_ref, v_ref, c0 in chunks:
        s = lax.dot_general(q, k_ref[c0:c0 + tk, :], (((1,), (1,)), ((), ())), preferred_element_type=f32)
        pv = jnp.dot(jnp.exp2(s).astype(bf16), v_ref[c0:c0 + tk, :], preferred_element_type=f32)
        acc = pv if acc is None else acc + pv
    out = (acc[:, 0:HEAD_DIM] / acc[:, HEAD_DIM:HEAD_DIM + 1]).astype(bf16)
    for g in range(Q_PER_KV):
        o_ref[:, g * HEAD_DIM:(g + 1) * HEAD_DIM] = out[g * tq:(g + 1) * tq]


def _attention(qk_lat, k_ctx, v_ctx, v_lat, bounded):
    bsz, _, seq_len, _ = qk_lat.shape
    ctx_len = k_ctx.shape[2]
    kv = lambda n, d, h0=0: pl.BlockSpec((None, None, n, d), lambda b, h, i: (b, h0 + h, 0, 0))
    tq = ATTN_TQ_BOUNDED if bounded else ATTN_TQ
    body = (functools.partial(_attn_bounded_kernel, tq=tq, tk=ATTN_TK_BOUNDED) if bounded
            else functools.partial(_attn_kernel, tq=tq, tk=ATTN_TK))
    return pl.pallas_call(
        body,
        grid=(bsz, N_KV_HEADS, seq_len // tq),
        in_specs=[
            pl.BlockSpec((None, Q_PER_KV, tq, HEAD_DIM), lambda b, h, i: (b, h, i, 0)),
            kv(ctx_len, HEAD_DIM), kv(ctx_len, V_WIDTH), kv(seq_len, HEAD_DIM, N_Q_HEADS), kv(seq_len, V_WIDTH),
        ],
        out_specs=pl.BlockSpec((None, tq, Q_PER_KV * HEAD_DIM), lambda b, h, i: (b, i, h)),
        out_shape=jax.ShapeDtypeStruct((bsz, seq_len, N_Q_HEADS * HEAD_DIM), bf16),
        compiler_params=_params("arbitrary", "arbitrary", "arbitrary"),
        name="attention_bounded" if bounded else "attention",
    )(qk_lat, k_ctx, v_ctx, qk_lat, v_lat)


def kernel(x, c, ctx, c_ctx, w_ada, b_ada, g_mix_pre, g_mix_post, g_mlp_pre, g_mlp_post, w_pool, pool_scale,
           w_qkv, g_q, g_k, w_o, w_mlp_in, w_mlp_out):
    bsz, seq_len, d = x.shape
    ctx_len = ctx.shape[1]
    assert d == D_MODEL and seq_len % ROW_TILE == 0 and bsz + 1 <= MOD_ROWS
    assert ctx_len % POOL_HALO == 0 and (bsz * ctx_len) % ROW_TILE == 0
    row = lambda v: v.reshape(1, -1)

    cond = jnp.zeros((MOD_ROWS, D_MODEL), f32).at[:bsz].set(c).at[bsz].set(c_ctx)
    mods = _ada_mods(cond, w_ada, b_ada).reshape(DEPTH, MOD_ROWS, 1, N_MOD * D_MODEL)
    lat_row = lambda b: b
    ctx_row = lambda b: bsz

    w_pool0 = w_pool[0].astype(bf16)
    layer0 = functools.partial(_pool_mlp_layer, mods=mods[0], g_mix_pre=row(g_mix_pre[0]),
                               g_mix_post=row(g_mix_post[0]), pool_scale=row(pool_scale[0]), w_pool=w_pool0,
                               g_mlp_pre=row(g_mlp_pre[0]), g_mlp_post=row(g_mlp_post[0]),
                               w_in=w_mlp_in[0:1].astype(bf16), w_out=w_mlp_out[0:1].astype(bf16), layer=0)
    x = layer0(x, mod_row=lat_row)
    ctx = layer0(ctx, mod_row=ctx_row)

    qk_cols = (N_Q_HEADS + N_KV_HEADS) * HEAD_DIM
    w_qkv1 = jnp.concatenate([_to_rope_lanes(w_qkv[0, :, :qk_cols]), w_qkv[0, :, qk_cols:]], axis=-1).astype(bf16)
    qkv = functools.partial(_qkv_project, mods=mods[1], g_pre=row(g_mix_pre[1]),
                            g_q=row(_to_rope_lanes(g_q[0])), g_k=row(_to_rope_lanes(g_k[0])))
    qk_lat, v_lat, w_in1, w_out1, w_o1 = qkv(x, mod_row=lat_row, w=w_qkv1, w_col0=0, tile=QKV_TILE, n_q=N_Q_HEADS,
                                             rope=True, casts=((w_mlp_in, 1), (w_mlp_out, 1), (w_o, 0)))
    k_ctx, v_ctx = qkv(ctx, mod_row=ctx_row, w=w_qkv1, w_col0=N_Q_HEADS * HEAD_DIM, tile=ctx_len, n_q=0, rope=False)
    score_bound = 1.02 * Q_SCALE * HEAD_DIM * jnp.max(jnp.abs(g_q[0])) * jnp.max(jnp.abs(g_k[0]))
    attn_out = lax.cond(score_bound <= SCORE_LOG2_LIMIT,
                        functools.partial(_attention, bounded=True), functools.partial(_attention, bounded=False),
                        qk_lat, k_ctx, v_ctx, v_lat)
    return _mlp_sublayer(x, mods[1], lat_row, row(g_mlp_pre[1]), row(g_mlp_post[1]),
                         w_in1, w_out1, 0, proj=(attn_out, w_o1, row(g_mix_post[1])))
```

```python
import functools
import math

import numpy as np
import jax
import jax.numpy as jnp
from jax import lax
from jax.experimental import pallas as pl
from jax.experimental.pallas import tpu as pltpu

D_MODEL = 1024
DEPTH = 2
GRID_W = 64
POOL_WINDOWS = (2, 4, 8, 16)
POOL_GROUP_DIM = D_MODEL // len(POOL_WINDOWS)
POOL_HALO = 8
HEAD_DIM = 128
N_Q_HEADS = D_MODEL // HEAD_DIM
N_KV_HEADS = 2
V_WIDTH = 2 * HEAD_DIM
Q_PER_KV = N_Q_HEADS // N_KV_HEADS
ROPE_THETA = 10000.0
D_FF = 4 * D_MODEL
N_MOD = 6
EPS = 1e-6
MOD_ROWS = 8
Q_SCALE = (HEAD_DIM ** -0.5) * math.log2(math.e)

ROW_TILE = 512
QKV_TILE = 1024
MLP_TILE = 1024
MLP_SUB_ROWS = 512
ATTN_TQ = 128
ATTN_TK = 512
ATTN_TQ_BOUNDED = 512
ATTN_TK_BOUNDED = 256
SCORE_LOG2_LIMIT = 100.0
FF_CHUNK = 1024
ADA_TN = 1536
VMEM_LIMIT = 56 * 1024 * 1024

f32 = jnp.float32
bf16 = jnp.bfloat16


def _params(*semantics, flags=None):
    return pltpu.CompilerParams(dimension_semantics=semantics, vmem_limit_bytes=VMEM_LIMIT, flags=flags)


def _rms_scale(x):
    return x * lax.rsqrt(jnp.mean(x * x, axis=-1, keepdims=True) + EPS)


def _sq_relu(u):
    ub = jnp.maximum(u.astype(bf16), 0.0)
    return ub * ub


def _mod_slices(mods_ref, first):
    return [mods_ref[:, (first + j) * D_MODEL:(first + j + 1) * D_MODEL] for j in range(3)]


def _ada_kernel(c_ref, w_ref, b_ref, o_ref):
    c = c_ref[...]
    s = c * jax.nn.sigmoid(c)
    o_ref[...] = jnp.dot(s.astype(bf16), w_ref[...].astype(bf16), preferred_element_type=f32) + b_ref[...]


def _ada_mods(cond, w_ada, b_ada):
    n = N_MOD * D_MODEL
    return pl.pallas_call(
        _ada_kernel,
        grid=(DEPTH, n // ADA_TN),
        in_specs=[
            pl.BlockSpec((MOD_ROWS, D_MODEL), lambda i, j: (0, 0)),
            pl.BlockSpec((None, D_MODEL, ADA_TN), lambda i, j: (i, 0, j)),
            pl.BlockSpec((None, 1, ADA_TN), lambda i, j: (i, 0, j)),
        ],
        out_specs=pl.BlockSpec((None, MOD_ROWS, ADA_TN), lambda i, j: (i, 0, j)),
        out_shape=jax.ShapeDtypeStruct((DEPTH, MOD_ROWS, n), f32),
        compiler_params=_params("arbitrary", "arbitrary"),
        name="ada_mods",
    )(cond, w_ada, b_ada.reshape(DEPTH, 1, n))


def _pool_mlp_kernel(xp_ref, x_ref, xn_ref, mods_ref, prev_mods_ref, gmix_pre_ref, gmix_post_ref, ps_ref, wpool_ref,
                     gpre_ref, gpost_ref, win_ref, wout_ref, o_ref, hbuf, x1_buf, h_buf, *level_bufs, tile, seq_len):
    s = pl.program_id(0)
    n_tiles = seq_len // tile
    i = jnp.minimum(s, pl.num_programs(0) - 2) % n_tiles

    def mlp_chunk(c, acc):
        cols = slice(c * FF_CHUNK, (c + 1) * FF_CHUNK)
        u = jnp.dot(h_buf[...], win_ref[:, cols], preferred_element_type=f32)
        part = jnp.dot(_sq_relu(u), wout_ref[cols, :], preferred_element_type=f32)
        return part if acc is None else acc + part

    sh1, sc1, gt1 = _mod_slices(mods_ref, 0)
    sh2, sc2, _ = _mod_slices(mods_ref, 3)
    mix_in_gain = gmix_pre_ref[...] * (1.0 + sc1)

    def hmod(xv):
        return _rms_scale(xv) * mix_in_gain + sh1

    ext = tile + 2 * POOL_HALO
    n_groups = len(POOL_WINDOWS)

    def zero_level_padding():
        for buf in (hbuf,) + tuple(level_bufs):
            buf[ext:, :] = jnp.zeros((POOL_HALO, buf.shape[1]), f32)

    def build_level(k):
        src = hbuf if k == 1 else level_bufs[k - 2]
        lane0 = 0 if k == 1 else POOL_GROUP_DIM
        rows = ext if k < n_groups else tile
        summed = src[0:rows, lane0:] + src[2 ** (k - 1):2 ** (k - 1) + rows, lane0:]
        if k == n_groups:
            return summed
        level_bufs[k - 1][0:ext, :] = summed

    def pool_group(g, top_level=None):
        w = POOL_WINDOWS[g]
        cols = slice(g * POOL_GROUP_DIM, (g + 1) * POOL_GROUP_DIM)
        acc = (top_level if g == n_groups - 1 else
               level_bufs[g][POOL_HALO - w // 2:POOL_HALO - w // 2 + tile, 0:POOL_GROUP_DIM])
        def clipped_mean(r0):
            t = i * tile + r0 + lax.broadcasted_iota(jnp.int32, (POOL_HALO, 1), 0)
            cnt = jnp.minimum(t + (w - w // 2), seq_len) - jnp.maximum(t - w // 2, 0)
            return acc[r0:r0 + POOL_HALO] / cnt.astype(f32)

        mean = jnp.concatenate([clipped_mean(0), acc[POOL_HALO:tile - POOL_HALO] * (1.0 / w),
                                clipped_mean(tile - POOL_HALO)], axis=0)
        diff = mean - hbuf[POOL_HALO:POOL_HALO + tile, cols]
        return jnp.dot(diff.astype(bf16), wpool_ref[g], preferred_element_type=f32) * ps_ref[:, cols]

    def fill_hbuf():
        hbuf[POOL_HALO:POOL_HALO + tile, :] = hmod(x_ref[...])
        hbuf[0:POOL_HALO, :] = jnp.where(i > 0, hmod(xp_ref[...]), 0.0)
        hbuf[POOL_HALO + tile:ext, :] = jnp.where(i < n_tiles - 1, hmod(xn_ref[...]), 0.0)

    def step(with_mlp, with_pool):
        acc, ys = None, []

        def first_groups():
            build_level(2)
            ys.extend([pool_group(0), pool_group(1)])

        def last_groups():
            build_level(3)
            ys.extend([pool_group(2), pool_group(3, build_level(4))])

        vpu_pieces = [lambda: (fill_hbuf(), build_level(1)), first_groups, last_groups]
        if not with_mlp:
            zero_level_padding()
        for c in range(D_FF // FF_CHUNK):
            if with_mlp:
                acc = mlp_chunk(c, acc)
            if with_pool and c < len(vpu_pieces):
                vpu_pieces[c]()
        if with_pool:
            x1 = x_ref[...] + _rms_scale(jnp.concatenate(ys, axis=-1)) * (gt1 * gmix_post_ref[...])
            h = (_rms_scale(x1) * (gpre_ref[...] * (1.0 + sc2)) + sh2).astype(bf16)
        if with_mlp:
            gt2_prev = prev_mods_ref[:, 5 * D_MODEL:6 * D_MODEL]
            o_ref[...] = x1_buf[...] + _rms_scale(acc) * (gt2_prev * gpost_ref[...])
        if with_pool:
            x1_buf[...] = x1
            h_buf[...] = h

    last = pl.num_programs(0) - 1
    pl.when(s == 0)(functools.partial(step, False, True))
    pl.when(jnp.logical_and(s > 0, s < last))(functools.partial(step, True, True))
    pl.when(s == last)(functools.partial(step, True, False))


def _pool_mlp_layer(x, mods, mod_row, g_mix_pre, g_mix_post, pool_scale, w_pool, g_mlp_pre, g_mlp_post,
                    w_in, w_out, layer):
    bsz, seq_len, _ = x.shape
    tile = min(ROW_TILE, seq_len)
    n_groups = len(POOL_WINDOWS)
    assert seq_len % tile == 0 and tile % POOL_HALO == 0 and D_FF // FF_CHUNK == n_groups == 4
    assert POOL_WINDOWS == tuple(2 ** (g + 1) for g in range(n_groups)) and POOL_WINDOWS[-1] == 2 * POOL_HALO
    n_tiles = seq_len // tile
    total = bsz * n_tiles
    hb = tile // POOL_HALO
    last_halo = seq_len // POOL_HALO - 1
    cur = lambda s: jnp.minimum(s, total - 1)
    prev = lambda s: jnp.maximum(s - 1, 0)
    row = pl.BlockSpec((1, D_MODEL), lambda s: (0, 0))
    mods_spec = lambda step: pl.BlockSpec((None, 1, N_MOD * D_MODEL), lambda s: (mod_row(step(s) // n_tiles), 0, 0))
    tok = lambda step: pl.BlockSpec((None, tile, D_MODEL), lambda s: (step(s) // n_tiles, step(s) % n_tiles, 0))
    whole = lambda w, l: pl.BlockSpec((None,) + w.shape[1:], lambda s: (l, 0, 0), pipeline_mode=pl.Buffered(1))
    return pl.pallas_call(
        functools.partial(_pool_mlp_kernel, tile=tile, seq_len=seq_len),
        grid=(total + 1,),
        in_specs=[
            pl.BlockSpec((None, POOL_HALO, D_MODEL),
                         lambda s: (cur(s) // n_tiles, jnp.maximum(cur(s) % n_tiles * hb - 1, 0), 0)),
            tok(cur),
            pl.BlockSpec((None, POOL_HALO, D_MODEL),
                         lambda s: (cur(s) // n_tiles, jnp.minimum((cur(s) % n_tiles + 1) * hb, last_halo), 0)),
            mods_spec(cur), mods_spec(prev),
            row, row, row,
            pl.BlockSpec(w_pool.shape, lambda s: (0, 0, 0)),
            row, row,
            whole(w_in, layer), whole(w_out, layer),
        ],
        out_specs=tok(prev),
        out_shape=jax.ShapeDtypeStruct(x.shape, f32),
        scratch_shapes=[pltpu.VMEM((tile + 3 * POOL_HALO, D_MODEL), f32),
                        pltpu.VMEM((tile, D_MODEL), f32), pltpu.VMEM((tile, D_MODEL), bf16)]
        + [pltpu.VMEM((tile + 3 * POOL_HALO, D_MODEL - k * POOL_GROUP_DIM), f32) for k in range(n_groups - 1)],
        compiler_params=_params("arbitrary"),
        name="pool_mlp_layer",
    )(x, x, x, mods, mods, g_mix_pre, g_mix_post, pool_scale, w_pool, g_mlp_pre, g_mlp_post, w_in, w_out)


def _mlp_kernel(*refs, has_proj):
    if has_proj:
        x_ref, a_ref, wo_ref, gmix_ref, mods_ref, gpre_ref, gpost_ref, win_ref, wout_ref, o_ref = refs
    else:
        x_ref, mods_ref, gpre_ref, gpost_ref, win_ref, wout_ref, o_ref = refs
    sh, sc, gt = _mod_slices(mods_ref, 3)
    pre_gain = gpre_ref[...] * (1.0 + sc)
    post_gain = gt * gpost_ref[...]
    if has_proj:
        mix_gain = mods_ref[:, 2 * D_MODEL:3 * D_MODEL] * gmix_ref[...]
    n_sub = x_ref.shape[0] // MLP_SUB_ROWS
    rows = [slice(r * MLP_SUB_ROWS, (r + 1) * MLP_SUB_ROWS) for r in range(n_sub)]

    def pre(r, y):
        x = x_ref[rows[r], :]
        if has_proj:
            x = x + _rms_scale(y) * mix_gain
        return x, (_rms_scale(x) * pre_gain + sh).astype(bf16)

    def mlp_chunk(h, c, acc):
        cols = slice(c * FF_CHUNK, (c + 1) * FF_CHUNK)
        u = jnp.dot(h, win_ref[:, cols], preferred_element_type=f32)
        part = jnp.dot(_sq_relu(u), wout_ref[cols, :], preferred_element_type=f32)
        return part if acc is None else acc + part

    def post(r, x, acc):
        o_ref[rows[r], :] = x + _rms_scale(acc) * post_gain

    ys = [jnp.dot(a_ref[rows[r], :], wo_ref[...], preferred_element_type=f32) if has_proj else None
          for r in range(n_sub)]
    cur = pre(0, ys[0])
    done = None
    for r in range(n_sub):
        x, h = cur
        acc = mlp_chunk(h, 0, None)
        if r + 1 < n_sub:
            cur = pre(r + 1, ys[r + 1])
        if done is not None:
            post(*done)
        for c in range(1, D_FF // FF_CHUNK):
            acc = mlp_chunk(h, c, acc)
        done = (r, x, acc)
    post(*done)


def _mlp_sublayer(x, mods, mod_row, g_pre, g_post, w_in, w_out, layer, proj=None):
    bsz, seq_len, _ = x.shape
    tile = min(MLP_TILE, seq_len)
    assert seq_len % tile == 0 and tile % MLP_SUB_ROWS == 0
    tok = pl.BlockSpec((None, tile, D_MODEL), lambda b, i: (b, i, 0))
    row = pl.BlockSpec((1, D_MODEL), lambda b, i: (0, 0))
    whole = lambda w, l=0: pl.BlockSpec((None,) + w.shape[1:], lambda b, i: (l, 0, 0), pipeline_mode=pl.Buffered(1))
    mods_spec = pl.BlockSpec((None, 1, N_MOD * D_MODEL), lambda b, i: (mod_row(b), 0, 0))
    if proj is None:
        args = (x, mods, g_pre, g_post, w_in, w_out)
        specs = [tok, mods_spec, row, row, whole(w_in, layer), whole(w_out, layer)]
    else:
        attn_out, w_o, g_mix = proj
        args = (x, attn_out, w_o, g_mix, mods, g_pre, g_post, w_in, w_out)
        specs = [tok, tok, whole(w_o), row, mods_spec, row, row, whole(w_in, layer), whole(w_out, layer)]
    return pl.pallas_call(
        functools.partial(_mlp_kernel, has_proj=proj is not None),
        grid=(bsz, seq_len // tile),
        in_specs=specs,
        out_specs=tok,
        out_shape=jax.ShapeDtypeStruct(x.shape, f32),
        compiler_params=_params("arbitrary", "arbitrary"),
        name="proj_mlp_sublayer" if proj is not None else "mlp_sublayer",
    )(*args)


def _rope_tables(seq_len):
    half = HEAD_DIM // 2
    t = np.arange(seq_len)
    inv_freq = np.power(np.float32(ROPE_THETA), -np.arange(0, half, 2, dtype=np.float32) / np.float32(half))
    ang_r = (t // GRID_W).astype(np.float32)[:, None] * inv_freq
    ang_c = (t % GRID_W).astype(np.float32)[:, None] * inv_freq
    cos = np.concatenate([np.cos(ang_r), np.cos(ang_c)] * 2, axis=-1)
    sin = np.concatenate([-np.sin(ang_r), -np.sin(ang_c), np.sin(ang_r), np.sin(ang_c)], axis=-1)
    return jnp.asarray(cos, f32), jnp.asarray(sin, f32)


def _to_rope_lanes(a):
    quarters = a.reshape(a.shape[:-1] + (-1, 4, HEAD_DIM // 4))
    swapped = jnp.concatenate([quarters[..., 0:1, :], quarters[..., 2:3, :], quarters[..., 1:2, :],
                               quarters[..., 3:4, :]], axis=-2)
    return swapped.reshape(a.shape)


def _qkv_kernel(*refs, n_q, rope, n_casts):
    x_ref, mods_ref, gpre_ref, w_ref, gq_ref, gk_ref = refs[:6]
    refs = refs[6:]
    if rope:
        cos_ref, sin_ref = refs[:2]
        refs = refs[2:]
    cast_in, refs = refs[:n_casts], refs[n_casts:]
    qk_ref, v_ref = refs[:2]
    cast_out, (h_buf, slab_buf) = refs[2:2 + n_casts], refs[2 + n_casts:]
    for src, dst in zip(cast_in, cast_out):
        dst[...] = src[...].astype(bf16)
    sh, sc, _ = _mod_slices(mods_ref, 0)
    pre_gain = gpre_ref[...] * (1.0 + sc)
    sub = slab_buf.shape[1]
    n_sub = x_ref.shape[0] // sub
    for r in range(n_sub):
        rows = slice(r * sub, (r + 1) * sub)
        h_buf[rows, :] = (_rms_scale(x_ref[rows, :]) * pre_gain + sh).astype(bf16)
    n_slabs = (n_q + N_KV_HEADS) // 2
    slab_cols = 2 * HEAD_DIM
    gq = gq_ref[...] * Q_SCALE
    same_head = (lax.broadcasted_iota(jnp.int32, (slab_cols, slab_cols), 0) // HEAD_DIM
                 == lax.broadcasted_iota(jnp.int32, (slab_cols, slab_cols), 1) // HEAD_DIM)
    head_ones = same_head.astype(bf16)
    ones_col = (lax.broadcasted_iota(jnp.int32, (sub, HEAD_DIM), 1) == 0).astype(bf16)

    for r in range(n_sub):
        rows = slice(r * sub, (r + 1) * sub)

        def project(i):
            return jnp.dot(h_buf[rows, :], w_ref[:, i * slab_cols:(i + 1) * slab_cols], preferred_element_type=f32)

        slab_buf[2 * r] = project(0)
        for i in range(n_slabs):
            slab_buf[2 * r + (i + 1) % 2] = project(i + 1)
            z = slab_buf[2 * r + i % 2]
            ssq = jnp.dot((z * z).astype(bf16), head_ones, preferred_element_type=f32)
            zn = z * lax.rsqrt(ssq * (1.0 / HEAD_DIM) + EPS)
            for j in range(2):
                head = 2 * i + j
                y = zn[:, j * HEAD_DIM:(j + 1) * HEAD_DIM] * (gq if head < n_q else gk_ref[...])
                if rope:
                    y = y * cos_ref[rows, :] + pltpu.roll(y, HEAD_DIM // 2, 1) * sin_ref[rows, :]
                qk_ref[head, rows, :] = y.astype(bf16)
        for j in range(N_KV_HEADS):
            v_ref[j, rows, 0:HEAD_DIM] = slab_buf[2 * r + n_slabs % 2, :, j * HEAD_DIM:(j + 1) * HEAD_DIM].astype(bf16)
            v_ref[j, rows, HEAD_DIM:] = ones_col


def _qkv_project(x, mods, mod_row, g_pre, w, w_col0, g_q, g_k, tile, n_q, rope, casts=()):
    bsz, seq_len, _ = x.shape
    w_cols = (n_q + 2 * N_KV_HEADS) * HEAD_DIM
    sub = min(tile, ROW_TILE)
    assert N_KV_HEADS == 2 and n_q % 2 == 0 and w_col0 % w_cols == 0 and tile % sub == 0
    n_tiles = seq_len // tile
    steps = bsz * n_tiles
    row = lambda d: pl.BlockSpec((1, d), lambda b, i: (0, 0))
    heads = lambda n, d: pl.BlockSpec((None, n, tile, d), lambda b, i: (b, 0, i, 0))
    args = [x, mods, g_pre, w, g_q, g_k]
    specs = [
        pl.BlockSpec((None, tile, D_MODEL), lambda b, i: (b, i, 0)),
        pl.BlockSpec((None, 1, N_MOD * D_MODEL), lambda b, i: (mod_row(b), 0, 0)),
        row(D_MODEL),
        pl.BlockSpec((w.shape[0], w_cols), lambda b, i: (0, w_col0 // w_cols), pipeline_mode=pl.Buffered(1)),
        row(HEAD_DIM), row(HEAD_DIM),
    ]
    if rope:
        args += list(_rope_tables(seq_len))
        specs += [pl.BlockSpec((tile, HEAD_DIM), lambda b, i: (i, 0))] * 2
    n_qk = n_q + N_KV_HEADS
    out_specs = [heads(n_qk, HEAD_DIM), heads(N_KV_HEADS, V_WIDTH)]
    out_shape = [jax.ShapeDtypeStruct((bsz, n_qk, seq_len, HEAD_DIM), bf16),
                 jax.ShapeDtypeStruct((bsz, N_KV_HEADS, seq_len, V_WIDTH), bf16)]
    for weight, index in casts:
        _, rows, cols = weight.shape
        assert rows % (steps * 16) == 0
        args.append(weight)
        specs.append(pl.BlockSpec((None, rows // steps, cols), lambda b, i, l=index: (l, b * n_tiles + i, 0)))
        out_specs.append(pl.BlockSpec((None, rows // steps, cols), lambda b, i: (0, b * n_tiles + i, 0)))
        out_shape.append(jax.ShapeDtypeStruct((1, rows, cols), bf16))
    return pl.pallas_call(
        functools.partial(_qkv_kernel, n_q=n_q, rope=rope, n_casts=len(casts)),
        grid=(bsz, n_tiles),
        in_specs=specs,
        out_specs=out_specs,
        out_shape=out_shape,
        scratch_shapes=[pltpu.VMEM((tile, D_MODEL), bf16), pltpu.VMEM((2 * (tile // sub), sub, 2 * HEAD_DIM), f32)],
        compiler_params=_params("arbitrary", "arbitrary"),
        name="qkv_project" if n_q else "kv_project",
    )(*args)


def _attn_kernel(q_ref, kc_ref, vc_ref, kl_ref, vl_ref, o_ref, *, tq, tk):
    rows = Q_PER_KV * tq
    q = q_ref[...].reshape(rows, HEAD_DIM)

    def step(k, v, carry):
        m, l, acc = carry
        s = lax.dot_general(q, k, (((1,), (1,)), ((), ())), preferred_element_type=f32)
        m_new = jnp.maximum(m, jnp.max(s, axis=-1, keepdims=True))
        alpha = jnp.exp2(m - m_new)
        p = jnp.exp2(s - m_new)
        l = alpha * l + jnp.sum(p, axis=-1, keepdims=True)
        acc = alpha * acc + jnp.dot(p.astype(bf16), v, preferred_element_type=f32)
        return m_new, l, acc

    carry = (jnp.full((rows, 1), -jnp.inf, f32), jnp.zeros((rows, 1), f32), jnp.zeros((rows, HEAD_DIM), f32))
    carry = step(kc_ref[...], vc_ref[:, 0:HEAD_DIM], carry)

    def body(j, carry):
        off = pl.multiple_of(j * tk, tk)
        return step(kl_ref[pl.ds(off, tk), :], vl_ref[pl.ds(off, tk), 0:HEAD_DIM], carry)

    _, l, acc = lax.fori_loop(0, kl_ref.shape[0] // tk, body, carry)
    out = (acc / l).astype(bf16)
    for g in range(Q_PER_KV):
        o_ref[:, g * HEAD_DIM:(g + 1) * HEAD_DIM] = out[g * tq:(g + 1) * tq]


def _attn_bounded_kernel(q_ref, kc_ref, vc_ref, kl_ref, vl_ref, o_ref, *, tq, tk):
    chunks = [(kc_ref, vc_ref, c0) for c0 in range(0, kc_ref.shape[0], tk)]
    chunks += [(kl_ref, vl_ref, c0) for c0 in range(0, kl_ref.shape[0], tk)]
    accs = [None] * Q_PER_KV
    for k_ref, v_ref, c0 in chunks:
        for g in range(Q_PER_KV):
            s = lax.dot_general(q_ref[g], k_ref[c0:c0 + tk, :], (((1,), (1,)), ((), ())),
                                preferred_element_type=f32)
            pv = jnp.dot(jnp.exp2(s).astype(bf16), v_ref[c0:c0 + tk, :], preferred_element_type=f32)
            accs[g] = pv if accs[g] is None else accs[g] + pv
    for g, acc in enumerate(accs):
        o_ref[:, g * HEAD_DIM:(g + 1) * HEAD_DIM] = (acc[:, 0:HEAD_DIM] / acc[:, HEAD_DIM:HEAD_DIM + 1]).astype(bf16)


def _attention(qk_lat, k_ctx, v_ctx, v_lat, bounded):
    bsz, _, seq_len, _ = qk_lat.shape
    ctx_len = k_ctx.shape[2]
    kv = lambda n, d, h0=0: pl.BlockSpec((None, None, n, d), lambda b, h, i: (b, h0 + h, 0, 0))
    tq = ATTN_TQ_BOUNDED if bounded else ATTN_TQ
    body = (functools.partial(_attn_bounded_kernel, tq=tq, tk=ATTN_TK_BOUNDED) if bounded
            else functools.partial(_attn_kernel, tq=tq, tk=ATTN_TK))
    return pl.pallas_call(
        body,
        grid=(bsz, N_KV_HEADS, seq_len // tq),
        in_specs=[
            pl.BlockSpec((None, Q_PER_KV, tq, HEAD_DIM), lambda b, h, i: (b, h, i, 0)),
            kv(ctx_len, HEAD_DIM), kv(ctx_len, V_WIDTH), kv(seq_len, HEAD_DIM, N_Q_HEADS), kv(seq_len, V_WIDTH),
        ],
        out_specs=pl.BlockSpec((None, tq, Q_PER_KV * HEAD_DIM), lambda b, h, i: (b, i, h)),
        out_shape=jax.ShapeDtypeStruct((bsz, seq_len, N_Q_HEADS * HEAD_DIM), bf16),
        compiler_params=_params("arbitrary", "arbitrary", "arbitrary"),
        name="attention_bounded" if bounded else "attention",
    )(qk_lat, k_ctx, v_ctx, qk_lat, v_lat)


def kernel(x, c, ctx, c_ctx, w_ada, b_ada, g_mix_pre, g_mix_post, g_mlp_pre, g_mlp_post, w_pool, pool_scale,
           w_qkv, g_q, g_k, w_o, w_mlp_in, w_mlp_out):
    bsz, seq_len, d = x.shape
    ctx_len = ctx.shape[1]
    assert d == D_MODEL and seq_len % ROW_TILE == 0 and bsz + 1 <= MOD_ROWS
    assert ctx_len % POOL_HALO == 0 and (bsz * ctx_len) % ROW_TILE == 0
    row = lambda v: v.reshape(1, -1)

    cond = jnp.zeros((MOD_ROWS, D_MODEL), f32).at[:bsz].set(c).at[bsz].set(c_ctx)
    mods = _ada_mods(cond, w_ada, b_ada).reshape(DEPTH, MOD_ROWS, 1, N_MOD * D_MODEL)
    lat_row = lambda b: b
    ctx_row = lambda b: bsz

    w_pool0 = w_pool[0].astype(bf16)
    layer0 = functools.partial(_pool_mlp_layer, mods=mods[0], g_mix_pre=row(g_mix_pre[0]),
                               g_mix_post=row(g_mix_post[0]), pool_scale=row(pool_scale[0]), w_pool=w_pool0,
                               g_mlp_pre=row(g_mlp_pre[0]), g_mlp_post=row(g_mlp_post[0]),
                               w_in=w_mlp_in[0:1].astype(bf16), w_out=w_mlp_out[0:1].astype(bf16), layer=0)
    x = layer0(x, mod_row=lat_row)
    ctx = layer0(ctx, mod_row=ctx_row)

    qk_cols = (N_Q_HEADS + N_KV_HEADS) * HEAD_DIM
    w_qkv1 = jnp.concatenate([_to_rope_lanes(w_qkv[0, :, :qk_cols]), w_qkv[0, :, qk_cols:]], axis=-1).astype(bf16)
    qkv = functools.partial(_qkv_project, mods=mods[1], g_pre=row(g_mix_pre[1]),
                            g_q=row(_to_rope_lanes(g_q[0])), g_k=row(_to_rope_lanes(g_k[0])))
    qk_lat, v_lat, w_in1, w_out1, w_o1 = qkv(x, mod_row=lat_row, w=w_qkv1, w_col0=0, tile=QKV_TILE, n_q=N_Q_HEADS,
                                             rope=True, casts=((w_mlp_in, 1), (w_mlp_out, 1), (w_o, 0)))
    k_ctx, v_ctx = qkv(ctx, mod_row=ctx_row, w=w_qkv1, w_col0=N_Q_HEADS * HEAD_DIM, tile=ctx_len, n_q=0, rope=False)
    score_bound = 1.02 * Q_SCALE * HEAD_DIM * jnp.max(jnp.abs(g_q[0])) * jnp.max(jnp.abs(g_k[0]))
    attn_out = lax.cond(score_bound <= SCORE_LOG2_LIMIT,
                        functools.partial(_attention, bounded=True), functools.partial(_attention, bounded=False),
                        qk_lat, k_ctx, v_ctx, v_lat)
    return _mlp_sublayer(x, mods[1], lat_row, row(g_mlp_pre[1]), row(g_mlp_post[1]),
                         w_in1, w_out1, 0, proj=(attn_out, w_o1, row(g_mix_post[1])))
```

```python
import functools
import math

import numpy as np
import jax
import jax.numpy as jnp
from jax import lax
from jax.experimental import pallas as pl
from jax.experimental.pallas import tpu as pltpu

D_MODEL = 1024
DEPTH = 2
GRID_W = 64
POOL_WINDOWS = (2, 4, 8, 16)
POOL_GROUP_DIM = D_MODEL // len(POOL_WINDOWS)
POOL_HALO = 8
HEAD_DIM = 128
N_Q_HEADS = D_MODEL // HEAD_DIM
N_KV_HEADS = 2
V_WIDTH = 2 * HEAD_DIM
Q_PER_KV = N_Q_HEADS // N_KV_HEADS
ROPE_THETA = 10000.0
D_FF = 4 * D_MODEL
N_MOD = 6
EPS = 1e-6
MOD_ROWS = 8
Q_SCALE = (HEAD_DIM ** -0.5) * math.log2(math.e)

ROW_TILE = 512
QKV_TILE = 1024
MLP_TILE = 1024
MLP_SUB_ROWS = 512
ATTN_TQ = 128
ATTN_TK = 512
ATTN_TQ_BOUNDED = 512
ATTN_CAST_SLABS = 64
ATTN_TK_BOUNDED = 256
SCORE_LOG2_LIMIT = 100.0
FF_CHUNK = 1024
LAYER0_FF_CHUNK = 512
ADA_TN = 3072
VMEM_LIMIT = 56 * 1024 * 1024

f32 = jnp.float32
bf16 = jnp.bfloat16


def _params(*semantics):
    return pltpu.CompilerParams(dimension_semantics=semantics, vmem_limit_bytes=VMEM_LIMIT)


def _rms_scale(x):
    return x * lax.rsqrt(jnp.mean(x * x, axis=-1, keepdims=True) + EPS)


def _sq_relu(u):
    ub = jnp.maximum(u.astype(bf16), 0.0)
    return ub * ub


def _cast_slabs(src_refs, dst_refs):
    for src, dst in zip(src_refs, dst_refs):
        dst[...] = src[...].astype(bf16)


def _cast_specs(casts, n_blocks, block_of_step):
    args, in_specs, out_specs, out_shape = [], [], [], []
    for weight, index in casts:
        _, rows, cols = weight.shape
        assert rows % (n_blocks * 16) == 0
        slab = (None, rows // n_blocks, cols)
        args.append(weight)
        in_specs.append(pl.BlockSpec(slab, lambda *ids, l=index: (l, block_of_step(*ids), 0)))
        out_specs.append(pl.BlockSpec(slab, lambda *ids: (0, block_of_step(*ids), 0)))
        out_shape.append(jax.ShapeDtypeStruct((1, rows, cols), bf16))
    return args, in_specs, out_specs, out_shape


def _mod_slices(mods_ref, first):
    return [mods_ref[:, (first + j) * D_MODEL:(first + j + 1) * D_MODEL] for j in range(3)]


def _ada_kernel(c_ref, w_ref, b_ref, o_ref):
    c = c_ref[...]
    s = c * jax.nn.sigmoid(c)
    o_ref[...] = jnp.dot(s.astype(bf16), w_ref[...].astype(bf16), preferred_element_type=f32) + b_ref[...]


def _ada_mods(cond, w_ada, b_ada):
    n = N_MOD * D_MODEL
    return pl.pallas_call(
        _ada_kernel,
        grid=(DEPTH, n // ADA_TN),
        in_specs=[
            pl.BlockSpec((MOD_ROWS, D_MODEL), lambda i, j: (0, 0)),
            pl.BlockSpec((None, D_MODEL, ADA_TN), lambda i, j: (i, 0, j)),
            pl.BlockSpec((None, 1, ADA_TN), lambda i, j: (i, 0, j)),
        ],
        out_specs=pl.BlockSpec((None, MOD_ROWS, ADA_TN), lambda i, j: (i, 0, j)),
        out_shape=jax.ShapeDtypeStruct((DEPTH, MOD_ROWS, n), f32),
        compiler_params=_params("arbitrary", "arbitrary"),
        name="ada_mods",
    )(cond, w_ada, b_ada.reshape(DEPTH, 1, n))


def _pool_mlp_kernel(xp_ref, x_ref, xn_ref, mods_ref, prev_mods_ref, gmix_pre_ref, gmix_post_ref, ps_ref, wpool_ref,
                     gpre_ref, gpost_ref, win_ref, wout_ref, o_ref, hbuf, x1_buf, h_buf, *level_bufs, tile, seq_len):
    s = pl.program_id(0)
    n_tiles = seq_len // tile
    i = jnp.minimum(s, pl.num_programs(0) - 2) % n_tiles

    def mlp_chunk(c, acc):
        cols = slice(c * LAYER0_FF_CHUNK, (c + 1) * LAYER0_FF_CHUNK)
        u = jnp.dot(h_buf[...], win_ref[:, cols], preferred_element_type=f32)
        part = jnp.dot(_sq_relu(u), wout_ref[cols, :], preferred_element_type=f32)
        return part if acc is None else acc + part

    sh1, sc1, gt1 = _mod_slices(mods_ref, 0)
    sh2, sc2, _ = _mod_slices(mods_ref, 3)
    mix_in_gain = gmix_pre_ref[...] * (1.0 + sc1)

    def hmod(xv):
        return _rms_scale(xv) * mix_in_gain + sh1

    ext = tile + 2 * POOL_HALO
    n_groups = len(POOL_WINDOWS)

    def zero_level_padding():
        for buf in (hbuf,) + tuple(level_bufs):
            buf[ext:, :] = jnp.zeros((POOL_HALO, buf.shape[1]), f32)

    def build_level(k):
        src = hbuf if k == 1 else level_bufs[k - 2]
        lane0 = 0 if k == 1 else POOL_GROUP_DIM
        rows = ext if k < n_groups else tile
        summed = src[0:rows, lane0:] + src[2 ** (k - 1):2 ** (k - 1) + rows, lane0:]
        if k == n_groups:
            return summed
        level_bufs[k - 1][0:ext, :] = summed

    def pool_group(g, top_level=None):
        w = POOL_WINDOWS[g]
        cols = slice(g * POOL_GROUP_DIM, (g + 1) * POOL_GROUP_DIM)
        acc = (top_level if g == n_groups - 1 else
               level_bufs[g][POOL_HALO - w // 2:POOL_HALO - w // 2 + tile, 0:POOL_GROUP_DIM])
        def clipped_mean(r0):
            t = i * tile + r0 + lax.broadcasted_iota(jnp.int32, (POOL_HALO, 1), 0)
            cnt = jnp.minimum(t + (w - w // 2), seq_len) - jnp.maximum(t - w // 2, 0)
            return acc[r0:r0 + POOL_HALO] / cnt.astype(f32)

        mean = jnp.concatenate([clipped_mean(0), acc[POOL_HALO:tile - POOL_HALO] * (1.0 / w),
                                clipped_mean(tile - POOL_HALO)], axis=0)
        diff = mean - hbuf[POOL_HALO:POOL_HALO + tile, cols]
        return jnp.dot(diff.astype(bf16), wpool_ref[g], preferred_element_type=f32) * ps_ref[:, cols]

    half = tile // 2

    def fill_top():
        hbuf[POOL_HALO:POOL_HALO + half, :] = hmod(x_ref[0:half, :])
        hbuf[0:POOL_HALO, :] = jnp.where(i > 0, hmod(xp_ref[...]), 0.0)

    def fill_bottom():
        hbuf[POOL_HALO + half:POOL_HALO + tile, :] = hmod(x_ref[half:tile, :])
        hbuf[POOL_HALO + tile:ext, :] = jnp.where(i < n_tiles - 1, hmod(xn_ref[...]), 0.0)

    def step(with_mlp, with_pool):
        acc, ys = None, []
        vpu_pieces = [fill_top, fill_bottom, lambda: build_level(1),
                      lambda: (build_level(2), ys.append(pool_group(0))),
                      lambda: (ys.append(pool_group(1)), build_level(3)),
                      lambda: ys.append(pool_group(2)),
                      lambda: ys.append(pool_group(3, build_level(4)))]
        if not with_mlp:
            zero_level_padding()
        for c in range(D_FF // LAYER0_FF_CHUNK):
            if with_mlp:
                acc = mlp_chunk(c, acc)
            if with_pool and c < len(vpu_pieces):
                vpu_pieces[c]()
        if with_pool:
            x1 = x_ref[...] + _rms_scale(jnp.concatenate(ys, axis=-1)) * (gt1 * gmix_post_ref[...])
            h = (_rms_scale(x1) * (gpre_ref[...] * (1.0 + sc2)) + sh2).astype(bf16)
        if with_mlp:
            gt2_prev = prev_mods_ref[:, 5 * D_MODEL:6 * D_MODEL]
            o_ref[...] = x1_buf[...] + _rms_scale(acc) * (gt2_prev * gpost_ref[...])
        if with_pool:
            x1_buf[...] = x1
            h_buf[...] = h

    last = pl.num_programs(0) - 1
    pl.when(s == 0)(functools.partial(step, False, True))
    pl.when(jnp.logical_and(s > 0, s < last))(functools.partial(step, True, True))
    pl.when(s == last)(functools.partial(step, True, False))


def _pool_mlp_layer(x, mods, mod_row, g_mix_pre, g_mix_post, pool_scale, w_pool, g_mlp_pre, g_mlp_post,
                    w_in, w_out):
    bsz, seq_len, _ = x.shape
    tile = min(ROW_TILE, seq_len)
    n_groups = len(POOL_WINDOWS)
    assert seq_len % tile == 0 and tile % POOL_HALO == 0 and D_FF // LAYER0_FF_CHUNK >= 7 and n_groups == 4
    assert POOL_WINDOWS == tuple(2 ** (g + 1) for g in range(n_groups)) and POOL_WINDOWS[-1] == 2 * POOL_HALO
    n_tiles = seq_len // tile
    total = bsz * n_tiles
    hb = tile // POOL_HALO
    last_halo = seq_len // POOL_HALO - 1
    cur = lambda s: jnp.minimum(s, total - 1)
    prev = lambda s: jnp.maximum(s - 1, 0)
    row = pl.BlockSpec((1, D_MODEL), lambda s: (0, 0))
    mods_spec = lambda step: pl.BlockSpec((None, 1, N_MOD * D_MODEL), lambda s: (mod_row(step(s) // n_tiles), 0, 0))
    tok = lambda step: pl.BlockSpec((None, tile, D_MODEL), lambda s: (step(s) // n_tiles, step(s) % n_tiles, 0))
    whole = lambda w: pl.BlockSpec((None,) + w.shape[1:], lambda s: (0, 0, 0), pipeline_mode=pl.Buffered(1))
    return pl.pallas_call(
        functools.partial(_pool_mlp_kernel, tile=tile, seq_len=seq_len),
        grid=(total + 1,),
        in_specs=[
            pl.BlockSpec((None, POOL_HALO, D_MODEL),
                         lambda s: (cur(s) // n_tiles, jnp.maximum(cur(s) % n_tiles * hb - 1, 0), 0)),
            tok(cur),
            pl.BlockSpec((None, POOL_HALO, D_MODEL),
                         lambda s: (cur(s) // n_tiles, jnp.minimum((cur(s) % n_tiles + 1) * hb, last_halo), 0)),
            mods_spec(cur), mods_spec(prev),
            row, row, row,
            pl.BlockSpec(w_pool.shape, lambda s: (0, 0, 0)),
            row, row,
            whole(w_in), whole(w_out),
        ],
        out_specs=tok(prev),
        out_shape=jax.ShapeDtypeStruct(x.shape, f32),
        scratch_shapes=[pltpu.VMEM((tile + 3 * POOL_HALO, D_MODEL), f32),
                        pltpu.VMEM((tile, D_MODEL), f32), pltpu.VMEM((tile, D_MODEL), bf16)]
        + [pltpu.VMEM((tile + 3 * POOL_HALO, D_MODEL - k * POOL_GROUP_DIM), f32) for k in range(n_groups - 1)],
        compiler_params=_params("arbitrary"),
        name="pool_mlp_layer",
    )(x, x, x, mods, mods, g_mix_pre, g_mix_post, pool_scale, w_pool, g_mlp_pre, g_mlp_post, w_in, w_out)


def _proj_mlp_kernel(x_ref, a_ref, wo_ref, gmix_ref, mods_ref, gpre_ref, gpost_ref, win_ref, wout_ref, o_ref):
    sh, sc, gt = _mod_slices(mods_ref, 3)
    pre_gain = gpre_ref[...] * (1.0 + sc)
    post_gain = gt * gpost_ref[...]
    mix_gain = mods_ref[:, 2 * D_MODEL:3 * D_MODEL] * gmix_ref[...]
    n_sub = x_ref.shape[0] // MLP_SUB_ROWS
    rows = [slice(r * MLP_SUB_ROWS, (r + 1) * MLP_SUB_ROWS) for r in range(n_sub)]

    def pre(r, y):
        x = x_ref[rows[r], :] + _rms_scale(y) * mix_gain
        return x, (_rms_scale(x) * pre_gain + sh).astype(bf16)

    def mlp_chunk(h, c, acc):
        cols = slice(c * FF_CHUNK, (c + 1) * FF_CHUNK)
        u = jnp.dot(h, win_ref[:, cols], preferred_element_type=f32)
        part = jnp.dot(_sq_relu(u), wout_ref[cols, :], preferred_element_type=f32)
        return part if acc is None else acc + part

    def post(r, x, acc):
        o_ref[rows[r], :] = x + _rms_scale(acc) * post_gain

    ys = [jnp.dot(a_ref[rows[r], :], wo_ref[...], preferred_element_type=f32) for r in range(n_sub)]
    cur = pre(0, ys[0])
    done = None
    for r in range(n_sub):
        x, h = cur
        acc = mlp_chunk(h, 0, None)
        if r + 1 < n_sub:
            cur = pre(r + 1, ys[r + 1])
        if done is not None:
            post(*done)
        for c in range(1, D_FF // FF_CHUNK):
            acc = mlp_chunk(h, c, acc)
        done = (r, x, acc)
    post(*done)


def _proj_mlp_layer(x, attn_out, w_o, g_mix_post, mods, mod_row, g_pre, g_post, w_in, w_out):
    bsz, seq_len, _ = x.shape
    tile = min(MLP_TILE, seq_len)
    assert seq_len % tile == 0 and tile % MLP_SUB_ROWS == 0
    tok = pl.BlockSpec((None, tile, D_MODEL), lambda b, i: (b, i, 0))
    row = pl.BlockSpec((1, D_MODEL), lambda b, i: (0, 0))
    whole = lambda w: pl.BlockSpec((None,) + w.shape[1:], lambda b, i: (0, 0, 0), pipeline_mode=pl.Buffered(1))
    mods_spec = pl.BlockSpec((None, 1, N_MOD * D_MODEL), lambda b, i: (mod_row(b), 0, 0))
    return pl.pallas_call(
        _proj_mlp_kernel,
        grid=(bsz, seq_len // tile),
        in_specs=[tok, tok, whole(w_o), row, mods_spec, row, row, whole(w_in), whole(w_out)],
        out_specs=tok,
        out_shape=jax.ShapeDtypeStruct(x.shape, f32),
        compiler_params=_params("arbitrary", "arbitrary"),
        name="proj_mlp_layer",
    )(x, attn_out, w_o, g_mix_post, mods, g_pre, g_post, w_in, w_out)


def _rope_tables(seq_len):
    half = HEAD_DIM // 2
    t = np.arange(seq_len)
    inv_freq = np.power(np.float32(ROPE_THETA), -np.arange(0, half, 2, dtype=np.float32) / np.float32(half))
    ang_r = (t // GRID_W).astype(np.float32)[:, None] * inv_freq
    ang_c = (t % GRID_W).astype(np.float32)[:, None] * inv_freq
    cos = np.concatenate([np.cos(ang_r), np.cos(ang_c)] * 2, axis=-1)
    sin = np.concatenate([-np.sin(ang_r), -np.sin(ang_c), np.sin(ang_r), np.sin(ang_c)], axis=-1)
    return jnp.asarray(cos, f32), jnp.asarray(sin, f32)


def _to_rope_lanes(a):
    quarters = a.reshape(a.shape[:-1] + (-1, 4, HEAD_DIM // 4))
    swapped = jnp.concatenate([quarters[..., 0:1, :], quarters[..., 2:3, :], quarters[..., 1:2, :],
                               quarters[..., 3:4, :]], axis=-2)
    return swapped.reshape(a.shape)


def _qkv_kernel(*refs, n_q, rope):
    x_ref, mods_ref, gpre_ref, w_ref, gq_ref, gk_ref = refs[:6]
    refs = refs[6:]
    if rope:
        cos_ref, sin_ref = refs[:2]
        refs = refs[2:]
    qk_ref, v_ref, h_buf, slab_buf = refs
    sh, sc, _ = _mod_slices(mods_ref, 0)
    pre_gain = gpre_ref[...] * (1.0 + sc)
    sub = slab_buf.shape[1]
    n_sub = x_ref.shape[0] // sub
    for r in range(n_sub):
        rows = slice(r * sub, (r + 1) * sub)
        h_buf[rows, :] = (_rms_scale(x_ref[rows, :]) * pre_gain + sh).astype(bf16)
    n_slabs = (n_q + N_KV_HEADS) // 2
    slab_cols = 2 * HEAD_DIM
    gq = gq_ref[...] * Q_SCALE
    same_head = (lax.broadcasted_iota(jnp.int32, (slab_cols, slab_cols), 0) // HEAD_DIM
                 == lax.broadcasted_iota(jnp.int32, (slab_cols, slab_cols), 1) // HEAD_DIM)
    head_ones = same_head.astype(bf16)
    ones_col = (lax.broadcasted_iota(jnp.int32, (sub, HEAD_DIM), 1) == 0).astype(bf16)

    for r in range(n_sub):
        rows = slice(r * sub, (r + 1) * sub)

        def project(i):
            return jnp.dot(h_buf[rows, :], w_ref[:, i * slab_cols:(i + 1) * slab_cols], preferred_element_type=f32)

        slab_buf[2 * r] = project(0)
        for i in range(n_slabs):
            slab_buf[2 * r + (i + 1) % 2] = project(i + 1)
            z = slab_buf[2 * r + i % 2]
            ssq = jnp.dot((z * z).astype(bf16), head_ones, preferred_element_type=f32)
            zn = z * lax.rsqrt(ssq * (1.0 / HEAD_DIM) + EPS)
            for j in range(2):
                head = 2 * i + j
                y = zn[:, j * HEAD_DIM:(j + 1) * HEAD_DIM] * (gq if head < n_q else gk_ref[...])
                if rope:
                    y = y * cos_ref[rows, :] + pltpu.roll(y, HEAD_DIM // 2, 1) * sin_ref[rows, :]
                qk_ref[head, rows, :] = y.astype(bf16)
        for j in range(N_KV_HEADS):
            v_ref[j, rows, 0:HEAD_DIM] = slab_buf[2 * r + n_slabs % 2, :, j * HEAD_DIM:(j + 1) * HEAD_DIM].astype(bf16)
            v_ref[j, rows, HEAD_DIM:] = ones_col


def _qkv_project(x, mods, mod_row, g_pre, w, w_col0, g_q, g_k, tile, n_q, rope):
    bsz, seq_len, _ = x.shape
    w_cols = (n_q + 2 * N_KV_HEADS) * HEAD_DIM
    sub = min(tile, ROW_TILE)
    assert N_KV_HEADS == 2 and n_q % 2 == 0 and w_col0 % w_cols == 0 and tile % sub == 0
    n_tiles = seq_len // tile
    row = lambda d: pl.BlockSpec((1, d), lambda i, b: (0, 0))
    heads = lambda n, d: pl.BlockSpec((None, n, tile, d), lambda i, b: (b, 0, i, 0))
    args = [x, mods, g_pre, w, g_q, g_k]
    specs = [
        pl.BlockSpec((None, tile, D_MODEL), lambda i, b: (b, i, 0)),
        pl.BlockSpec((None, 1, N_MOD * D_MODEL), lambda i, b: (mod_row(b), 0, 0)),
        row(D_MODEL),
        pl.BlockSpec((w.shape[0], w_cols), lambda i, b: (0, w_col0 // w_cols), pipeline_mode=pl.Buffered(1)),
        row(HEAD_DIM), row(HEAD_DIM),
    ]
    if rope:
        args += list(_rope_tables(seq_len))
        specs += [pl.BlockSpec((tile, HEAD_DIM), lambda i, b: (i, 0))] * 2
    n_qk = n_q + N_KV_HEADS
    return pl.pallas_call(
        functools.partial(_qkv_kernel, n_q=n_q, rope=rope),
        grid=(n_tiles, bsz),
        in_specs=specs,
        out_specs=[heads(n_qk, HEAD_DIM), heads(N_KV_HEADS, V_WIDTH)],
        out_shape=[jax.ShapeDtypeStruct((bsz, n_qk, seq_len, HEAD_DIM), bf16),
                   jax.ShapeDtypeStruct((bsz, N_KV_HEADS, seq_len, V_WIDTH), bf16)],
        scratch_shapes=[pltpu.VMEM((tile, D_MODEL), bf16), pltpu.VMEM((2 * (tile // sub), sub, 2 * HEAD_DIM), f32)],
        compiler_params=_params("arbitrary", "arbitrary"),
        name="qkv_project" if n_q else "kv_project",
    )(*args)


def _split_cast_refs(rest, n_casts):
    return rest[n_casts], rest[:n_casts], rest[n_casts + 1:]


def _attn_kernel(q_ref, kc_ref, vc_ref, kl_ref, vl_ref, *rest, tq, tk, n_casts):
    o_ref, cast_in, cast_out = _split_cast_refs(rest, n_casts)
    _cast_slabs(cast_in, cast_out)
    rows = Q_PER_KV * tq
    q = q_ref[...].reshape(rows, HEAD_DIM)

    def step(k, v, carry):
        m, l, acc = carry
        s = lax.dot_general(q, k, (((1,), (1,)), ((), ())), preferred_element_type=f32)
        m_new = jnp.maximum(m, jnp.max(s, axis=-1, keepdims=True))
        alpha = jnp.exp2(m - m_new)
        p = jnp.exp2(s - m_new)
        l = alpha * l + jnp.sum(p, axis=-1, keepdims=True)
        acc = alpha * acc + jnp.dot(p.astype(bf16), v, preferred_element_type=f32)
        return m_new, l, acc

    carry = (jnp.full((rows, 1), -jnp.inf, f32), jnp.zeros((rows, 1), f32), jnp.zeros((rows, HEAD_DIM), f32))
    carry = step(kc_ref[...], vc_ref[:, 0:HEAD_DIM], carry)

    def body(j, carry):
        off = pl.multiple_of(j * tk, tk)
        return step(kl_ref[pl.ds(off, tk), :], vl_ref[pl.ds(off, tk), 0:HEAD_DIM], carry)

    _, l, acc = lax.fori_loop(0, kl_ref.shape[0] // tk, body, carry)
    out = (acc / l).astype(bf16)
    for g in range(Q_PER_KV):
        o_ref[:, g * HEAD_DIM:(g + 1) * HEAD_DIM] = out[g * tq:(g + 1) * tq]


def _attn_bounded_kernel(q_ref, kc_ref, vc_ref, kl_ref, vl_ref, *rest, tq, tk, n_casts):
    o_ref, cast_in, cast_out = _split_cast_refs(rest, n_casts)
    _cast_slabs(cast_in, cast_out)
    chunks = [(k_ref, v_ref, slice(c0, min(c0 + tk, k_ref.shape[0])))
              for k_ref, v_ref in ((kc_ref, vc_ref), (kl_ref, vl_ref)) for c0 in range(0, k_ref.shape[0], tk)]
    accs = [None] * Q_PER_KV
    for k_ref, v_ref, keys in chunks:
        for g in range(Q_PER_KV):
            s = lax.dot_general(q_ref[g], k_ref[keys, :], (((1,), (1,)), ((), ())), preferred_element_type=f32)
            pv = jnp.dot(jnp.exp2(s).astype(bf16), v_ref[keys, :], preferred_element_type=f32)
            accs[g] = pv if accs[g] is None else accs[g] + pv
    for g, acc in enumerate(accs):
        o_ref[:, g * HEAD_DIM:(g + 1) * HEAD_DIM] = (acc[:, 0:HEAD_DIM] / acc[:, HEAD_DIM:HEAD_DIM + 1]).astype(bf16)


def _attention(qk_lat, k_ctx, v_ctx, v_lat, *cast_weights, bounded, cast_index):
    bsz, _, seq_len, _ = qk_lat.shape
    ctx_len = k_ctx.shape[2]
    kv = lambda n, d, h0=0: pl.BlockSpec((None, None, n, d), lambda b, h, i: (b, h0 + h, 0, 0))
    tq = ATTN_TQ_BOUNDED if bounded else ATTN_TQ
    n_tiles = seq_len // tq
    steps = bsz * N_KV_HEADS * n_tiles
    assert steps % ATTN_CAST_SLABS == 0
    slab_of_step = lambda b, h, i: ((b * N_KV_HEADS + h) * n_tiles + i) // (steps // ATTN_CAST_SLABS)
    cast_args, cast_in_specs, cast_out_specs, cast_shapes = _cast_specs(
        tuple(zip(cast_weights, cast_index)), ATTN_CAST_SLABS, slab_of_step)
    body = functools.partial(_attn_bounded_kernel if bounded else _attn_kernel, tq=tq,
                             tk=ATTN_TK_BOUNDED if bounded else ATTN_TK, n_casts=len(cast_args))
    return pl.pallas_call(
        body,
        grid=(bsz, N_KV_HEADS, n_tiles),
        in_specs=[
            pl.BlockSpec((None, Q_PER_KV, tq, HEAD_DIM), lambda b, h, i: (b, h, i, 0)),
            kv(ctx_len, HEAD_DIM), kv(ctx_len, V_WIDTH), kv(seq_len, HEAD_DIM, N_Q_HEADS), kv(seq_len, V_WIDTH),
        ] + cast_in_specs,
        out_specs=[pl.BlockSpec((None, tq, Q_PER_KV * HEAD_DIM), lambda b, h, i: (b, i, h))] + cast_out_specs,
        out_shape=[jax.ShapeDtypeStruct((bsz, seq_len, N_Q_HEADS * HEAD_DIM), bf16)] + cast_shapes,
        compiler_params=_params("arbitrary", "arbitrary", "arbitrary"),
        name="attention_bounded" if bounded else "attention",
    )(qk_lat, k_ctx, v_ctx, qk_lat, v_lat, *cast_args)


def kernel(x, c, ctx, c_ctx, w_ada, b_ada, g_mix_pre, g_mix_post, g_mlp_pre, g_mlp_post, w_pool, pool_scale,
           w_qkv, g_q, g_k, w_o, w_mlp_in, w_mlp_out):
    bsz, seq_len, d = x.shape
    ctx_len = ctx.shape[1]
    assert d == D_MODEL and seq_len % ROW_TILE == 0 and bsz + 1 <= MOD_ROWS
    assert ctx_len % POOL_HALO == 0 and (bsz * ctx_len) % ROW_TILE == 0
    row = lambda v: v.reshape(1, -1)

    cond = jnp.concatenate([c, c_ctx[None], jnp.zeros((MOD_ROWS - bsz - 1, D_MODEL), f32)], axis=0)
    mods = _ada_mods(cond, w_ada, b_ada).reshape(DEPTH, MOD_ROWS, 1, N_MOD * D_MODEL)
    lat_row = lambda b: b
    ctx_row = lambda b: bsz

    w_pool0 = w_pool[0].astype(bf16)
    layer0 = functools.partial(_pool_mlp_layer, mods=mods[0], g_mix_pre=row(g_mix_pre[0]),
                               g_mix_post=row(g_mix_post[0]), pool_scale=row(pool_scale[0]), w_pool=w_pool0,
                               g_mlp_pre=row(g_mlp_pre[0]), g_mlp_post=row(g_mlp_post[0]),
                               w_in=w_mlp_in[0:1].astype(bf16), w_out=w_mlp_out[0:1].astype(bf16))
    x = layer0(x, mod_row=lat_row)
    ctx = layer0(ctx, mod_row=ctx_row)

    qk_cols = (N_Q_HEADS + N_KV_HEADS) * HEAD_DIM
    w_qkv1 = jnp.concatenate([_to_rope_lanes(w_qkv[0, :, :qk_cols]), w_qkv[0, :, qk_cols:]], axis=-1).astype(bf16)
    qkv = functools.partial(_qkv_project, mods=mods[1], g_pre=row(g_mix_pre[1]),
                            g_q=row(_to_rope_lanes(g_q[0])), g_k=row(_to_rope_lanes(g_k[0])))
    qk_lat, v_lat = qkv(x, mod_row=lat_row, w=w_qkv1, w_col0=0, tile=QKV_TILE, n_q=N_Q_HEADS, rope=True)
    k_ctx, v_ctx = qkv(ctx, mod_row=ctx_row, w=w_qkv1, w_col0=N_Q_HEADS * HEAD_DIM, tile=ctx_len, n_q=0, rope=False)
    score_bound = 1.02 * Q_SCALE * HEAD_DIM * jnp.max(jnp.abs(g_q[0])) * jnp.max(jnp.abs(g_k[0]))
    attn = functools.partial(_attention, cast_index=(1, 1, 0))
    attn_out, w_in1, w_out1, w_o1 = lax.cond(
        score_bound <= SCORE_LOG2_LIMIT, functools.partial(attn, bounded=True), functools.partial(attn, bounded=False),
        qk_lat, k_ctx, v_ctx, v_lat, w_mlp_in, w_mlp_out, w_o)
    return _proj_mlp_layer(x, attn_out, w_o1, row(g_mix_post[1]), mods[1], lat_row,
                           row(g_mlp_pre[1]), row(g_mlp_post[1]), w_in1, w_out1)
```

```python
import functools
import math

import numpy as np
import jax
import jax.numpy as jnp
from jax import lax
from jax.experimental import pallas as pl
from jax.experimental.pallas import tpu as pltpu

D_MODEL = 1024
DEPTH = 2
GRID_W = 64
POOL_WINDOWS = (2, 4, 8, 16)
POOL_GROUP_DIM = D_MODEL // len(POOL_WINDOWS)
POOL_HALO = 8
HEAD_DIM = 128
N_Q_HEADS = D_MODEL // HEAD_DIM
N_KV_HEADS = 2
V_WIDTH = 2 * HEAD_DIM
Q_PER_KV = N_Q_HEADS // N_KV_HEADS
ROPE_THETA = 10000.0
D_FF = 4 * D_MODEL
N_MOD = 6
EPS = 1e-6
MOD_ROWS = 8
Q_SCALE = (HEAD_DIM ** -0.5) * math.log2(math.e)

ROW_TILE = 512
QKV_TILE = 1024
MLP_TILE = 1024
MLP_SUB_ROWS = 512
ATTN_TQ = 128
ATTN_TK = 512
ATTN_TQ_BOUNDED = 1024
ATTN_CAST_SLABS = 64
ATTN_TK_BOUNDED = 256
SCORE_LOG2_LIMIT = 100.0
FF_CHUNK = 1024
LAYER0_FF_CHUNK = 512
ADA_TN = 3072
VMEM_LIMIT = 56 * 1024 * 1024

f32 = jnp.float32
bf16 = jnp.bfloat16


def _params(*semantics):
    return pltpu.CompilerParams(dimension_semantics=semantics, vmem_limit_bytes=VMEM_LIMIT)


def _rms_scale(x):
    return x * lax.rsqrt(jnp.mean(x * x, axis=-1, keepdims=True) + EPS)


def _sq_relu(u):
    ub = jnp.maximum(u.astype(bf16), 0.0)
    return ub * ub


def _cast_slabs(src_refs, dst_refs):
    for src, dst in zip(src_refs, dst_refs):
        dst[...] = src[...].astype(bf16)


def _cast_specs(casts, n_blocks, block_of_step):
    args, in_specs, out_specs, out_shape = [], [], [], []
    for weight, index in casts:
        _, rows, cols = weight.shape
        assert rows % (n_blocks * 16) == 0
        slab = (None, rows // n_blocks, cols)
        args.append(weight)
        in_specs.append(pl.BlockSpec(slab, lambda *ids, l=index: (l, block_of_step(*ids), 0)))
        out_specs.append(pl.BlockSpec(slab, lambda *ids: (0, block_of_step(*ids), 0)))
        out_shape.append(jax.ShapeDtypeStruct((1, rows, cols), bf16))
    return args, in_specs, out_specs, out_shape


def _mod_slices(mods_ref, first):
    return [mods_ref[:, (first + j) * D_MODEL:(first + j + 1) * D_MODEL] for j in range(3)]


def _ada_kernel(c_ref, w_ref, b_ref, o_ref):
    c = c_ref[...]
    s = c * jax.nn.sigmoid(c)
    o_ref[...] = jnp.dot(s.astype(bf16), w_ref[...].astype(bf16), preferred_element_type=f32) + b_ref[...]


def _ada_mods(cond, w_ada, b_ada):
    n = N_MOD * D_MODEL
    return pl.pallas_call(
        _ada_kernel,
        grid=(DEPTH, n // ADA_TN),
        in_specs=[
            pl.BlockSpec((MOD_ROWS, D_MODEL), lambda i, j: (0, 0)),
            pl.BlockSpec((None, D_MODEL, ADA_TN), lambda i, j: (i, 0, j)),
            pl.BlockSpec((None, 1, ADA_TN), lambda i, j: (i, 0, j)),
        ],
        out_specs=pl.BlockSpec((None, MOD_ROWS, ADA_TN), lambda i, j: (i, 0, j)),
        out_shape=jax.ShapeDtypeStruct((DEPTH, MOD_ROWS, n), f32),
        compiler_params=_params("arbitrary", "arbitrary"),
        name="ada_mods",
    )(cond, w_ada, b_ada.reshape(DEPTH, 1, n))


def _pool_mlp_kernel(xp_ref, x_ref, xn_ref, mods_ref, prev_mods_ref, gmix_pre_ref, gmix_post_ref, ps_ref, wpool_ref,
                     gpre_ref, gpost_ref, win_ref, wout_ref, o_ref, hbuf, x1_buf, h_buf, *level_bufs, tile, seq_len):
    s = pl.program_id(0)
    n_tiles = seq_len // tile
    i = jnp.minimum(s, pl.num_programs(0) - 2) % n_tiles

    def mlp_chunk(c, acc):
        cols = slice(c * LAYER0_FF_CHUNK, (c + 1) * LAYER0_FF_CHUNK)
        u = jnp.dot(h_buf[...], win_ref[:, cols], preferred_element_type=f32)
        part = jnp.dot(_sq_relu(u), wout_ref[cols, :], preferred_element_type=f32)
        return part if acc is None else acc + part

    sh1, sc1, gt1 = _mod_slices(mods_ref, 0)
    sh2, sc2, _ = _mod_slices(mods_ref, 3)
    mix_in_gain = gmix_pre_ref[...] * (1.0 + sc1)

    def hmod(xv):
        return _rms_scale(xv) * mix_in_gain + sh1

    ext = tile + 2 * POOL_HALO
    n_groups = len(POOL_WINDOWS)

    def zero_level_padding():
        for buf in (hbuf,) + tuple(level_bufs):
            buf[ext:, :] = jnp.zeros((POOL_HALO, buf.shape[1]), f32)

    def build_level(k):
        src = hbuf if k == 1 else level_bufs[k - 2]
        lane0 = 0 if k == 1 else POOL_GROUP_DIM
        rows = ext if k < n_groups else tile
        summed = src[0:rows, lane0:] + src[2 ** (k - 1):2 ** (k - 1) + rows, lane0:]
        if k == n_groups:
            return summed
        level_bufs[k - 1][0:ext, :] = summed

    def pool_group(g, top_level=None):
        w = POOL_WINDOWS[g]
        cols = slice(g * POOL_GROUP_DIM, (g + 1) * POOL_GROUP_DIM)
        acc = (top_level if g == n_groups - 1 else
               level_bufs[g][POOL_HALO - w // 2:POOL_HALO - w // 2 + tile, 0:POOL_GROUP_DIM])
        def clipped_mean(r0):
            t = i * tile + r0 + lax.broadcasted_iota(jnp.int32, (POOL_HALO, 1), 0)
            cnt = jnp.minimum(t + (w - w // 2), seq_len) - jnp.maximum(t - w // 2, 0)
            return acc[r0:r0 + POOL_HALO] / cnt.astype(f32)

        mean = jnp.concatenate([clipped_mean(0), acc[POOL_HALO:tile - POOL_HALO] * (1.0 / w),
                                clipped_mean(tile - POOL_HALO)], axis=0)
        diff = mean - hbuf[POOL_HALO:POOL_HALO + tile, cols]
        return jnp.dot(diff.astype(bf16), wpool_ref[g], preferred_element_type=f32) * ps_ref[:, cols]

    half = tile // 2

    def fill_top():
        hbuf[POOL_HALO:POOL_HALO + half, :] = hmod(x_ref[0:half, :])
        hbuf[0:POOL_HALO, :] = jnp.where(i > 0, hmod(xp_ref[...]), 0.0)

    def fill_bottom():
        hbuf[POOL_HALO + half:POOL_HALO + tile, :] = hmod(x_ref[half:tile, :])
        hbuf[POOL_HALO + tile:ext, :] = jnp.where(i < n_tiles - 1, hmod(xn_ref[...]), 0.0)

    def step(with_mlp, with_pool):
        acc, ys = None, []
        vpu_pieces = [fill_top, fill_bottom, lambda: build_level(1),
                      lambda: (build_level(2), ys.append(pool_group(0))),
                      lambda: (ys.append(pool_group(1)), build_level(3)),
                      lambda: ys.append(pool_group(2)),
                      lambda: ys.append(pool_group(3, build_level(4)))]
        if not with_mlp:
            zero_level_padding()
        for c in range(D_FF // LAYER0_FF_CHUNK):
            if with_mlp:
                acc = mlp_chunk(c, acc)
            if with_pool and c < len(vpu_pieces):
                vpu_pieces[c]()
        if with_pool:
            x1 = x_ref[...] + _rms_scale(jnp.concatenate(ys, axis=-1)) * (gt1 * gmix_post_ref[...])
            h = (_rms_scale(x1) * (gpre_ref[...] * (1.0 + sc2)) + sh2).astype(bf16)
        if with_mlp:
            gt2_prev = prev_mods_ref[:, 5 * D_MODEL:6 * D_MODEL]
            o_ref[...] = x1_buf[...] + _rms_scale(acc) * (gt2_prev * gpost_ref[...])
        if with_pool:
            x1_buf[...] = x1
            h_buf[...] = h

    last = pl.num_programs(0) - 1
    pl.when(s == 0)(functools.partial(step, False, True))
    pl.when(jnp.logical_and(s > 0, s < last))(functools.partial(step, True, True))
    pl.when(s == last)(functools.partial(step, True, False))


def _pool_mlp_layer(x, mods, mod_row, g_mix_pre, g_mix_post, pool_scale, w_pool, g_mlp_pre, g_mlp_post,
                    w_in, w_out):
    bsz, seq_len, _ = x.shape
    tile = min(ROW_TILE, seq_len)
    n_groups = len(POOL_WINDOWS)
    assert seq_len % tile == 0 and tile % POOL_HALO == 0 and D_FF // LAYER0_FF_CHUNK >= 7 and n_groups == 4
    assert POOL_WINDOWS == tuple(2 ** (g + 1) for g in range(n_groups)) and POOL_WINDOWS[-1] == 2 * POOL_HALO
    n_tiles = seq_len // tile
    total = bsz * n_tiles
    hb = tile // POOL_HALO
    last_halo = seq_len // POOL_HALO - 1
    cur = lambda s: jnp.minimum(s, total - 1)
    prev = lambda s: jnp.maximum(s - 1, 0)
    row = pl.BlockSpec((1, D_MODEL), lambda s: (0, 0))
    mods_spec = lambda step: pl.BlockSpec((None, 1, N_MOD * D_MODEL), lambda s: (mod_row(step(s) // n_tiles), 0, 0))
    tok = lambda step: pl.BlockSpec((None, tile, D_MODEL), lambda s: (step(s) // n_tiles, step(s) % n_tiles, 0))
    whole = lambda w: pl.BlockSpec((None,) + w.shape[1:], lambda s: (0, 0, 0), pipeline_mode=pl.Buffered(1))
    return pl.pallas_call(
        functools.partial(_pool_mlp_kernel, tile=tile, seq_len=seq_len),
        grid=(total + 1,),
        in_specs=[
            pl.BlockSpec((None, POOL_HALO, D_MODEL),
                         lambda s: (cur(s) // n_tiles, jnp.maximum(cur(s) % n_tiles * hb - 1, 0), 0)),
            tok(cur),
            pl.BlockSpec((None, POOL_HALO, D_MODEL),
                         lambda s: (cur(s) // n_tiles, jnp.minimum((cur(s) % n_tiles + 1) * hb, last_halo), 0)),
            mods_spec(cur), mods_spec(prev),
            row, row, row,
            pl.BlockSpec(w_pool.shape, lambda s: (0, 0, 0)),
            row, row,
            whole(w_in), whole(w_out),
        ],
        out_specs=tok(prev),
        out_shape=jax.ShapeDtypeStruct(x.shape, f32),
        scratch_shapes=[pltpu.VMEM((tile + 3 * POOL_HALO, D_MODEL), f32),
                        pltpu.VMEM((tile, D_MODEL), f32), pltpu.VMEM((tile, D_MODEL), bf16)]
        + [pltpu.VMEM((tile + 3 * POOL_HALO, D_MODEL - k * POOL_GROUP_DIM), f32) for k in range(n_groups - 1)],
        compiler_params=_params("arbitrary"),
        name="pool_mlp_layer",
    )(x, x, x, mods, mods, g_mix_pre, g_mix_post, pool_scale, w_pool, g_mlp_pre, g_mlp_post, w_in, w_out)


def _proj_mlp_kernel(x_ref, a_ref, wo_ref, gmix_ref, mods_ref, gpre_ref, gpost_ref, win_ref, wout_ref, o_ref):
    sh, sc, gt = _mod_slices(mods_ref, 3)
    pre_gain = gpre_ref[...] * (1.0 + sc)
    post_gain = gt * gpost_ref[...]
    mix_gain = mods_ref[:, 2 * D_MODEL:3 * D_MODEL] * gmix_ref[...]
    n_sub = x_ref.shape[0] // MLP_SUB_ROWS
    rows = [slice(r * MLP_SUB_ROWS, (r + 1) * MLP_SUB_ROWS) for r in range(n_sub)]

    def pre(r, y):
        x = x_ref[rows[r], :] + _rms_scale(y) * mix_gain
        return x, (_rms_scale(x) * pre_gain + sh).astype(bf16)

    def mlp_chunk(h, c, acc):
        cols = slice(c * FF_CHUNK, (c + 1) * FF_CHUNK)
        u = jnp.dot(h, win_ref[:, cols], preferred_element_type=f32)
        part = jnp.dot(_sq_relu(u), wout_ref[cols, :], preferred_element_type=f32)
        return part if acc is None else acc + part

    def post(r, x, acc):
        o_ref[rows[r], :] = x + _rms_scale(acc) * post_gain

    ys = [jnp.dot(a_ref[rows[r], :], wo_ref[...], preferred_element_type=f32) for r in range(n_sub)]
    cur = pre(0, ys[0])
    done = None
    for r in range(n_sub):
        x, h = cur
        acc = mlp_chunk(h, 0, None)
        if r + 1 < n_sub:
            cur = pre(r + 1, ys[r + 1])
        if done is not None:
            post(*done)
        for c in range(1, D_FF // FF_CHUNK):
            acc = mlp_chunk(h, c, acc)
        done = (r, x, acc)
    post(*done)


def _proj_mlp_layer(x, attn_out, w_o, g_mix_post, mods, mod_row, g_pre, g_post, w_in, w_out):
    bsz, seq_len, _ = x.shape
    tile = min(MLP_TILE, seq_len)
    assert seq_len % tile == 0 and tile % MLP_SUB_ROWS == 0
    tok = pl.BlockSpec((None, tile, D_MODEL), lambda b, i: (b, i, 0))
    row = pl.BlockSpec((1, D_MODEL), lambda b, i: (0, 0))
    whole = lambda w: pl.BlockSpec((None,) + w.shape[1:], lambda b, i: (0, 0, 0), pipeline_mode=pl.Buffered(1))
    mods_spec = pl.BlockSpec((None, 1, N_MOD * D_MODEL), lambda b, i: (mod_row(b), 0, 0))
    return pl.pallas_call(
        _proj_mlp_kernel,
        grid=(bsz, seq_len // tile),
        in_specs=[tok, tok, whole(w_o), row, mods_spec, row, row, whole(w_in), whole(w_out)],
        out_specs=tok,
        out_shape=jax.ShapeDtypeStruct(x.shape, f32),
        compiler_params=_params("arbitrary", "arbitrary"),
        name="proj_mlp_layer",
    )(x, attn_out, w_o, g_mix_post, mods, g_pre, g_post, w_in, w_out)


def _rope_tables(seq_len):
    half = HEAD_DIM // 2
    t = np.arange(seq_len)
    inv_freq = np.power(np.float32(ROPE_THETA), -np.arange(0, half, 2, dtype=np.float32) / np.float32(half))
    ang_r = (t // GRID_W).astype(np.float32)[:, None] * inv_freq
    ang_c = (t % GRID_W).astype(np.float32)[:, None] * inv_freq
    cos = np.concatenate([np.cos(ang_r), np.cos(ang_c)] * 2, axis=-1)
    sin = np.concatenate([-np.sin(ang_r), -np.sin(ang_c), np.sin(ang_r), np.sin(ang_c)], axis=-1)
    return jnp.asarray(cos, f32), jnp.asarray(sin, f32)


def _to_rope_lanes(a):
    quarters = a.reshape(a.shape[:-1] + (-1, 4, HEAD_DIM // 4))
    swapped = jnp.concatenate([quarters[..., 0:1, :], quarters[..., 2:3, :], quarters[..., 1:2, :],
                               quarters[..., 3:4, :]], axis=-2)
    return swapped.reshape(a.shape)


def _qkv_kernel(*refs, n_q, rope):
    x_ref, mods_ref, gpre_ref, w_ref, gq_ref, gk_ref = refs[:6]
    refs = refs[6:]
    if rope:
        cos_ref, sin_ref = refs[:2]
        refs = refs[2:]
    qk_ref, v_ref, h_buf, slab_buf = refs
    sh, sc, _ = _mod_slices(mods_ref, 0)
    pre_gain = gpre_ref[...] * (1.0 + sc)
    sub = slab_buf.shape[1]
    n_sub = x_ref.shape[0] // sub
    for r in range(n_sub):
        rows = slice(r * sub, (r + 1) * sub)
        h_buf[rows, :] = (_rms_scale(x_ref[rows, :]) * pre_gain + sh).astype(bf16)
    n_slabs = (n_q + N_KV_HEADS) // 2
    slab_cols = 2 * HEAD_DIM
    gq = gq_ref[...] * Q_SCALE
    same_head = (lax.broadcasted_iota(jnp.int32, (slab_cols, slab_cols), 0) // HEAD_DIM
                 == lax.broadcasted_iota(jnp.int32, (slab_cols, slab_cols), 1) // HEAD_DIM)
    head_ones = same_head.astype(bf16)
    ones_col = (lax.broadcasted_iota(jnp.int32, (sub, HEAD_DIM), 1) == 0).astype(bf16)

    for r in range(n_sub):
        rows = slice(r * sub, (r + 1) * sub)

        def project(i):
            return jnp.dot(h_buf[rows, :], w_ref[:, i * slab_cols:(i + 1) * slab_cols], preferred_element_type=f32)

        slab_buf[2 * r] = project(0)
        for i in range(n_slabs):
            slab_buf[2 * r + (i + 1) % 2] = project(i + 1)
            z = slab_buf[2 * r + i % 2]
            ssq = jnp.dot((z * z).astype(bf16), head_ones, preferred_element_type=f32)
            zn = z * lax.rsqrt(ssq * (1.0 / HEAD_DIM) + EPS)
            for j in range(2):
                head = 2 * i + j
                y = zn[:, j * HEAD_DIM:(j + 1) * HEAD_DIM] * (gq if head < n_q else gk_ref[...])
                if rope:
                    y = y * cos_ref[rows, :] + pltpu.roll(y, HEAD_DIM // 2, 1) * sin_ref[rows, :]
                qk_ref[head, rows, :] = y.astype(bf16)
        for j in range(N_KV_HEADS):
            v_ref[j, rows, 0:HEAD_DIM] = slab_buf[2 * r + n_slabs % 2, :, j * HEAD_DIM:(j + 1) * HEAD_DIM].astype(bf16)
            v_ref[j, rows, HEAD_DIM:] = ones_col


def _qkv_project(x, mods, mod_row, g_pre, w, w_col0, g_q, g_k, tile, n_q, rope):
    bsz, seq_len, _ = x.shape
    w_cols = (n_q + 2 * N_KV_HEADS) * HEAD_DIM
    sub = min(tile, ROW_TILE)
    assert N_KV_HEADS == 2 and n_q % 2 == 0 and w_col0 % w_cols == 0 and tile % sub == 0
    n_tiles = seq_len // tile
    row = lambda d: pl.BlockSpec((1, d), lambda i, b: (0, 0))
    heads = lambda n, d: pl.BlockSpec((None, n, tile, d), lambda i, b: (b, 0, i, 0))
    args = [x, mods, g_pre, w, g_q, g_k]
    specs = [
        pl.BlockSpec((None, tile, D_MODEL), lambda i, b: (b, i, 0)),
        pl.BlockSpec((None, 1, N_MOD * D_MODEL), lambda i, b: (mod_row(b), 0, 0)),
        row(D_MODEL),
        pl.BlockSpec((w.shape[0], w_cols), lambda i, b: (0, w_col0 // w_cols), pipeline_mode=pl.Buffered(1)),
        row(HEAD_DIM), row(HEAD_DIM),
    ]
    if rope:
        args += list(_rope_tables(seq_len))
        specs += [pl.BlockSpec((tile, HEAD_DIM), lambda i, b: (i, 0))] * 2
    n_qk = n_q + N_KV_HEADS
    return pl.pallas_call(
        functools.partial(_qkv_kernel, n_q=n_q, rope=rope),
        grid=(n_tiles, bsz),
        in_specs=specs,
        out_specs=[heads(n_qk, HEAD_DIM), heads(N_KV_HEADS, V_WIDTH)],
        out_shape=[jax.ShapeDtypeStruct((bsz, n_qk, seq_len, HEAD_DIM), bf16),
                   jax.ShapeDtypeStruct((bsz, N_KV_HEADS, seq_len, V_WIDTH), bf16)],
        scratch_shapes=[pltpu.VMEM((tile, D_MODEL), bf16), pltpu.VMEM((2 * (tile // sub), sub, 2 * HEAD_DIM), f32)],
        compiler_params=_params("arbitrary", "arbitrary"),
        name="qkv_project" if n_q else "kv_project",
    )(*args)


def _split_cast_refs(rest, n_casts):
    return rest[n_casts], rest[:n_casts], rest[n_casts + 1:]


def _attn_kernel(q_ref, kc_ref, vc_ref, kl_ref, vl_ref, *rest, tq, tk, n_casts):
    o_ref, cast_in, cast_out = _split_cast_refs(rest, n_casts)
    _cast_slabs(cast_in, cast_out)
    rows = Q_PER_KV * tq
    q = q_ref[...].reshape(rows, HEAD_DIM)

    def step(k, v, carry):
        m, l, acc = carry
        s = lax.dot_general(q, k, (((1,), (1,)), ((), ())), preferred_element_type=f32)
        m_new = jnp.maximum(m, jnp.max(s, axis=-1, keepdims=True))
        alpha = jnp.exp2(m - m_new)
        p = jnp.exp2(s - m_new)
        l = alpha * l + jnp.sum(p, axis=-1, keepdims=True)
        acc = alpha * acc + jnp.dot(p.astype(bf16), v, preferred_element_type=f32)
        return m_new, l, acc

    carry = (jnp.full((rows, 1), -jnp.inf, f32), jnp.zeros((rows, 1), f32), jnp.zeros((rows, HEAD_DIM), f32))
    carry = step(kc_ref[...], vc_ref[:, 0:HEAD_DIM], carry)

    def body(j, carry):
        off = pl.multiple_of(j * tk, tk)
        return step(kl_ref[pl.ds(off, tk), :], vl_ref[pl.ds(off, tk), 0:HEAD_DIM], carry)

    _, l, acc = lax.fori_loop(0, kl_ref.shape[0] // tk, body, carry)
    out = (acc / l).astype(bf16)
    for g in range(Q_PER_KV):
        o_ref[:, g * HEAD_DIM:(g + 1) * HEAD_DIM] = out[g * tq:(g + 1) * tq]


def _attn_bounded_kernel(q_ref, kc_ref, vc_ref, kl_ref, vl_ref, *rest, tq, tk, n_casts):
    o_ref, cast_in, cast_out = _split_cast_refs(rest, n_casts)
    _cast_slabs(cast_in, cast_out)
    chunks = [(k_ref, v_ref, slice(c0, min(c0 + tk, k_ref.shape[0])))
              for k_ref, v_ref in ((kc_ref, vc_ref), (kl_ref, vl_ref)) for c0 in range(0, k_ref.shape[0], tk)]
    accs = [None] * Q_PER_KV
    for k_ref, v_ref, keys in chunks:
        for g in range(Q_PER_KV):
            s = lax.dot_general(q_ref[g], k_ref[keys, :], (((1,), (1,)), ((), ())), preferred_element_type=f32)
            pv = jnp.dot(jnp.exp2(s).astype(bf16), v_ref[keys, :], preferred_element_type=f32)
            accs[g] = pv if accs[g] is None else accs[g] + pv
    for g, acc in enumerate(accs):
        o_ref[:, g * HEAD_DIM:(g + 1) * HEAD_DIM] = (acc[:, 0:HEAD_DIM] / acc[:, HEAD_DIM:HEAD_DIM + 1]).astype(bf16)


def _attention(qk_lat, k_ctx, v_ctx, v_lat, *cast_weights, bounded, cast_index):
    bsz, _, seq_len, _ = qk_lat.shape
    ctx_len = k_ctx.shape[2]
    kv = lambda n, d, h0=0: pl.BlockSpec((None, None, n, d), lambda b, h, i: (b, h0 + h, 0, 0))
    tq = ATTN_TQ_BOUNDED if bounded else ATTN_TQ
    n_tiles = seq_len // tq
    steps = bsz * N_KV_HEADS * n_tiles
    n_slabs = min(steps, ATTN_CAST_SLABS)
    assert steps % n_slabs == 0
    slab_of_step = lambda b, h, i: ((b * N_KV_HEADS + h) * n_tiles + i) // (steps // n_slabs)
    cast_args, cast_in_specs, cast_out_specs, cast_shapes = _cast_specs(
        tuple(zip(cast_weights, cast_index)), n_slabs, slab_of_step)
    body = functools.partial(_attn_bounded_kernel if bounded else _attn_kernel, tq=tq,
                             tk=ATTN_TK_BOUNDED if bounded else ATTN_TK, n_casts=len(cast_args))
    return pl.pallas_call(
        body,
        grid=(bsz, N_KV_HEADS, n_tiles),
        in_specs=[
            pl.BlockSpec((None, Q_PER_KV, tq, HEAD_DIM), lambda b, h, i: (b, h, i, 0)),
            kv(ctx_len, HEAD_DIM), kv(ctx_len, V_WIDTH), kv(seq_len, HEAD_DIM, N_Q_HEADS), kv(seq_len, V_WIDTH),
        ] + cast_in_specs,
        out_specs=[pl.BlockSpec((None, tq, Q_PER_KV * HEAD_DIM), lambda b, h, i: (b, i, h))] + cast_out_specs,
        out_shape=[jax.ShapeDtypeStruct((bsz, seq_len, N_Q_HEADS * HEAD_DIM), bf16)] + cast_shapes,
        compiler_params=_params("arbitrary", "arbitrary", "arbitrary"),
        name="attention_bounded" if bounded else "attention",
    )(qk_lat, k_ctx, v_ctx, qk_lat, v_lat, *cast_args)


def kernel(x, c, ctx, c_ctx, w_ada, b_ada, g_mix_pre, g_mix_post, g_mlp_pre, g_mlp_post, w_pool, pool_scale,
           w_qkv, g_q, g_k, w_o, w_mlp_in, w_mlp_out):
    bsz, seq_len, d = x.shape
    ctx_len = ctx.shape[1]
    assert d == D_MODEL and seq_len % ROW_TILE == 0 and bsz + 1 <= MOD_ROWS
    assert ctx_len % POOL_HALO == 0 and (bsz * ctx_len) % ROW_TILE == 0
    row = lambda v: v.reshape(1, -1)

    cond = jnp.concatenate([c, c_ctx[None], jnp.zeros((MOD_ROWS - bsz - 1, D_MODEL), f32)], axis=0)
    mods = _ada_mods(cond, w_ada, b_ada).reshape(DEPTH, MOD_ROWS, 1, N_MOD * D_MODEL)
    lat_row = lambda b: b
    ctx_row = lambda b: bsz

    w_pool0 = w_pool[0].astype(bf16)
    layer0 = functools.partial(_pool_mlp_layer, mods=mods[0], g_mix_pre=row(g_mix_pre[0]),
                               g_mix_post=row(g_mix_post[0]), pool_scale=row(pool_scale[0]), w_pool=w_pool0,
                               g_mlp_pre=row(g_mlp_pre[0]), g_mlp_post=row(g_mlp_post[0]),
                               w_in=w_mlp_in[0:1].astype(bf16), w_out=w_mlp_out[0:1].astype(bf16))
    x = layer0(x, mod_row=lat_row)
    ctx = layer0(ctx, mod_row=ctx_row)

    qk_cols = (N_Q_HEADS + N_KV_HEADS) * HEAD_DIM
    w_qkv1 = jnp.concatenate([_to_rope_lanes(w_qkv[0, :, :qk_cols]), w_qkv[0, :, qk_cols:]], axis=-1).astype(bf16)
    qkv = functools.partial(_qkv_project, mods=mods[1], g_pre=row(g_mix_pre[1]),
                            g_q=row(_to_rope_lanes(g_q[0])), g_k=row(_to_rope_lanes(g_k[0])))
    qk_lat, v_lat = qkv(x, mod_row=lat_row, w=w_qkv1, w_col0=0, tile=QKV_TILE, n_q=N_Q_HEADS, rope=True)
    k_ctx, v_ctx = qkv(ctx, mod_row=ctx_row, w=w_qkv1, w_col0=N_Q_HEADS * HEAD_DIM, tile=ctx_len, n_q=0, rope=False)
    score_bound = 1.02 * Q_SCALE * HEAD_DIM * jnp.max(jnp.abs(g_q[0])) * jnp.max(jnp.abs(g_k[0]))
    attn = functools.partial(_attention, cast_index=(1, 1, 0))
    attn_out, w_in1, w_out1, w_o1 = lax.cond(
        score_bound <= SCORE_LOG2_LIMIT, functools.partial(attn, bounded=True), functools.partial(attn, bounded=False),
        qk_lat, k_ctx, v_ctx, v_lat, w_mlp_in, w_mlp_out, w_o)
    return _proj_mlp_layer(x, attn_out, w_o1, row(g_mix_post[1]), mods[1], lat_row,
                           row(g_mlp_pre[1]), row(g_mlp_post[1]), w_in1, w_out1)
```

```python
import functools
import math

import numpy as np
import jax
import jax.numpy as jnp
from jax import lax
from jax.experimental import pallas as pl
from jax.experimental.pallas import tpu as pltpu

D_MODEL = 1024
DEPTH = 2
GRID_W = 64
POOL_WINDOWS = (2, 4, 8, 16)
POOL_GROUP_DIM = D_MODEL // len(POOL_WINDOWS)
POOL_HALO = 8
HEAD_DIM = 128
N_Q_HEADS = D_MODEL // HEAD_DIM
N_KV_HEADS = 2
V_WIDTH = 2 * HEAD_DIM
Q_PER_KV = N_Q_HEADS // N_KV_HEADS
ROPE_THETA = 10000.0
D_FF = 4 * D_MODEL
N_MOD = 6
EPS = 1e-6
MOD_ROWS = 8
Q_SCALE = (HEAD_DIM ** -0.5) * math.log2(math.e)

ROW_TILE = 512
QKV_TILE = 1024
MLP_TILE = 1024
MLP_SUB_ROWS = 512
ATTN_TQ = 128
ATTN_TK = 512
ATTN_TQ_BOUNDED = 512
ATTN_CAST_SLABS = 64
ATTN_TK_BOUNDED = 256
SCORE_LOG2_LIMIT = 64.0
FF_CHUNK = 1024
LAYER0_FF_CHUNK = 512
ADA_TN = 3072
VMEM_LIMIT = 56 * 1024 * 1024

f32 = jnp.float32
bf16 = jnp.bfloat16


def _params(*semantics):
    return pltpu.CompilerParams(dimension_semantics=semantics, vmem_limit_bytes=VMEM_LIMIT)


def _rms_scale(x):
    return x * lax.rsqrt(jnp.mean(x * x, axis=-1, keepdims=True) + EPS)


def _sq_relu(u):
    ub = jnp.maximum(u.astype(bf16), 0.0)
    return ub * ub


def _cast_slabs(src_refs, dst_refs):
    for src, dst in zip(src_refs, dst_refs):
        dst[...] = src[...].astype(bf16)


def _cast_specs(casts, n_blocks, block_of_step):
    args, in_specs, out_specs, out_shape = [], [], [], []
    for weight, index in casts:
        _, rows, cols = weight.shape
        assert rows % (n_blocks * 16) == 0
        slab = (None, rows // n_blocks, cols)
        args.append(weight)
        in_specs.append(pl.BlockSpec(slab, lambda *ids, l=index: (l, block_of_step(*ids), 0)))
        out_specs.append(pl.BlockSpec(slab, lambda *ids: (0, block_of_step(*ids), 0)))
        out_shape.append(jax.ShapeDtypeStruct((1, rows, cols), bf16))
    return args, in_specs, out_specs, out_shape


def _mod_slices(mods_ref, first):
    return [mods_ref[:, (first + j) * D_MODEL:(first + j + 1) * D_MODEL] for j in range(3)]


def _ada_kernel(c_ref, w_ref, b_ref, o_ref):
    c = c_ref[...]
    s = c * jax.nn.sigmoid(c)
    o_ref[...] = jnp.dot(s.astype(bf16), w_ref[...].astype(bf16), preferred_element_type=f32) + b_ref[...]


def _ada_mods(cond, w_ada, b_ada):
    n = N_MOD * D_MODEL
    return pl.pallas_call(
        _ada_kernel,
        grid=(DEPTH, n // ADA_TN),
        in_specs=[
            pl.BlockSpec((MOD_ROWS, D_MODEL), lambda i, j: (0, 0)),
            pl.BlockSpec((None, D_MODEL, ADA_TN), lambda i, j: (i, 0, j)),
            pl.BlockSpec((None, 1, ADA_TN), lambda i, j: (i, 0, j)),
        ],
        out_specs=pl.BlockSpec((None, MOD_ROWS, ADA_TN), lambda i, j: (i, 0, j)),
        out_shape=jax.ShapeDtypeStruct((DEPTH, MOD_ROWS, n), f32),
        compiler_params=_params("arbitrary", "arbitrary"),
        name="ada_mods",
    )(cond, w_ada, b_ada.reshape(DEPTH, 1, n))


def _pool_mlp_kernel(xp_ref, x_ref, xn_ref, mods_ref, prev_mods_ref, gmix_pre_ref, gmix_post_ref, ps_ref, wpool_ref,
                     gpre_ref, gpost_ref, win_ref, wout_ref, o_ref, hbuf, x1_buf, h_buf, *level_bufs, tile, seq_len):
    s = pl.program_id(0)
    n_tiles = seq_len // tile
    i = jnp.minimum(s, pl.num_programs(0) - 2) % n_tiles

    def mlp_chunk(c, acc):
        cols = slice(c * LAYER0_FF_CHUNK, (c + 1) * LAYER0_FF_CHUNK)
        u = jnp.dot(h_buf[...], win_ref[:, cols], preferred_element_type=f32)
        part = jnp.dot(_sq_relu(u), wout_ref[cols, :], preferred_element_type=f32)
        return part if acc is None else acc + part

    sh1, sc1, gt1 = _mod_slices(mods_ref, 0)
    sh2, sc2, _ = _mod_slices(mods_ref, 3)
    mix_in_gain = gmix_pre_ref[...] * (1.0 + sc1)

    def hmod(xv):
        return _rms_scale(xv) * mix_in_gain + sh1

    ext = tile + 2 * POOL_HALO
    n_groups = len(POOL_WINDOWS)

    def zero_level_padding():
        for buf in (hbuf,) + tuple(level_bufs):
            buf[ext:, :] = jnp.zeros((POOL_HALO, buf.shape[1]), f32)

    def build_level(k):
        src = hbuf if k == 1 else level_bufs[k - 2]
        lane0 = 0 if k == 1 else POOL_GROUP_DIM
        rows = ext if k < n_groups else tile
        summed = src[0:rows, lane0:] + src[2 ** (k - 1):2 ** (k - 1) + rows, lane0:]
        if k == n_groups:
            return summed
        level_bufs[k - 1][0:ext, :] = summed

    def pool_group(g, top_level=None):
        w = POOL_WINDOWS[g]
        cols = slice(g * POOL_GROUP_DIM, (g + 1) * POOL_GROUP_DIM)
        acc = (top_level if g == n_groups - 1 else
               level_bufs[g][POOL_HALO - w // 2:POOL_HALO - w // 2 + tile, 0:POOL_GROUP_DIM])
        def clipped_mean(r0):
            t = i * tile + r0 + lax.broadcasted_iota(jnp.int32, (POOL_HALO, 1), 0)
            cnt = jnp.minimum(t + (w - w // 2), seq_len) - jnp.maximum(t - w // 2, 0)
            return acc[r0:r0 + POOL_HALO] / cnt.astype(f32)

        mean = jnp.concatenate([clipped_mean(0), acc[POOL_HALO:tile - POOL_HALO] * (1.0 / w),
                                clipped_mean(tile - POOL_HALO)], axis=0)
        diff = mean - hbuf[POOL_HALO:POOL_HALO + tile, cols]
        return jnp.dot(diff.astype(bf16), wpool_ref[g], preferred_element_type=f32) * ps_ref[:, cols]

    half = tile // 2

    def fill_top():
        hbuf[POOL_HALO:POOL_HALO + half, :] = hmod(x_ref[0:half, :])
        hbuf[0:POOL_HALO, :] = jnp.where(i > 0, hmod(xp_ref[...]), 0.0)

    def fill_bottom():
        hbuf[POOL_HALO + half:POOL_HALO + tile, :] = hmod(x_ref[half:tile, :])
        hbuf[POOL_HALO + tile:ext, :] = jnp.where(i < n_tiles - 1, hmod(xn_ref[...]), 0.0)

    def step(with_mlp, with_pool):
        acc, ys = None, []
        vpu_pieces = [fill_top, fill_bottom, lambda: build_level(1),
                      lambda: (build_level(2), ys.append(pool_group(0))),
                      lambda: (ys.append(pool_group(1)), build_level(3)),
                      lambda: ys.append(pool_group(2)),
                      lambda: ys.append(pool_group(3, build_level(4)))]
        if not with_mlp:
            zero_level_padding()
        for c in range(D_FF // LAYER0_FF_CHUNK):
            if with_mlp:
                acc = mlp_chunk(c, acc)
            if with_pool and c < len(vpu_pieces):
                vpu_pieces[c]()
        if with_pool:
            x1 = x_ref[...] + _rms_scale(jnp.concatenate(ys, axis=-1)) * (gt1 * gmix_post_ref[...])
            h = (_rms_scale(x1) * (gpre_ref[...] * (1.0 + sc2)) + sh2).astype(bf16)
        if with_mlp:
            gt2_prev = prev_mods_ref[:, 5 * D_MODEL:6 * D_MODEL]
            o_ref[...] = x1_buf[...] + _rms_scale(acc) * (gt2_prev * gpost_ref[...])
        if with_pool:
            x1_buf[...] = x1
            h_buf[...] = h

    last = pl.num_programs(0) - 1
    pl.when(s == 0)(functools.partial(step, False, True))
    pl.when(jnp.logical_and(s > 0, s < last))(functools.partial(step, True, True))
    pl.when(s == last)(functools.partial(step, True, False))


def _pool_mlp_layer(x, mods, mod_row, g_mix_pre, g_mix_post, pool_scale, w_pool, g_mlp_pre, g_mlp_post,
                    w_in, w_out):
    bsz, seq_len, _ = x.shape
    tile = min(ROW_TILE, seq_len)
    n_groups = len(POOL_WINDOWS)
    assert seq_len % tile == 0 and tile % POOL_HALO == 0 and D_FF // LAYER0_FF_CHUNK >= 7 and n_groups == 4
    assert POOL_WINDOWS == tuple(2 ** (g + 1) for g in range(n_groups)) and POOL_WINDOWS[-1] == 2 * POOL_HALO
    n_tiles = seq_len // tile
    total = bsz * n_tiles
    hb = tile // POOL_HALO
    last_halo = seq_len // POOL_HALO - 1
    cur = lambda s: jnp.minimum(s, total - 1)
    prev = lambda s: jnp.maximum(s - 1, 0)
    row = pl.BlockSpec((1, D_MODEL), lambda s: (0, 0))
    mods_spec = lambda step: pl.BlockSpec((None, 1, N_MOD * D_MODEL), lambda s: (mod_row(step(s) // n_tiles), 0, 0))
    tok = lambda step: pl.BlockSpec((None, tile, D_MODEL), lambda s: (step(s) // n_tiles, step(s) % n_tiles, 0))
    whole = lambda w: pl.BlockSpec((None,) + w.shape[1:], lambda s: (0, 0, 0), pipeline_mode=pl.Buffered(1))
    return pl.pallas_call(
        functools.partial(_pool_mlp_kernel, tile=tile, seq_len=seq_len),
        grid=(total + 1,),
        in_specs=[
            pl.BlockSpec((None, POOL_HALO, D_MODEL),
                         lambda s: (cur(s) // n_tiles, jnp.maximum(cur(s) % n_tiles * hb - 1, 0), 0)),
            tok(cur),
            pl.BlockSpec((None, POOL_HALO, D_MODEL),
                         lambda s: (cur(s) // n_tiles, jnp.minimum((cur(s) % n_tiles + 1) * hb, last_halo), 0)),
            mods_spec(cur), mods_spec(prev),
            row, row, row,
            pl.BlockSpec(w_pool.shape, lambda s: (0, 0, 0)),
            row, row,
            whole(w_in), whole(w_out),
        ],
        out_specs=tok(prev),
        out_shape=jax.ShapeDtypeStruct(x.shape, f32),
        scratch_shapes=[pltpu.VMEM((tile + 3 * POOL_HALO, D_MODEL), f32),
                        pltpu.VMEM((tile, D_MODEL), f32), pltpu.VMEM((tile, D_MODEL), bf16)]
        + [pltpu.VMEM((tile + 3 * POOL_HALO, D_MODEL - k * POOL_GROUP_DIM), f32) for k in range(n_groups - 1)],
        compiler_params=_params("arbitrary"),
        name="pool_mlp_layer",
    )(x, x, x, mods, mods, g_mix_pre, g_mix_post, pool_scale, w_pool, g_mlp_pre, g_mlp_post, w_in, w_out)


def _proj_mlp_kernel(x_ref, a_ref, wo_ref, gmix_ref, mods_ref, gpre_ref, gpost_ref, win_ref, wout_ref, o_ref):
    sh, sc, gt = _mod_slices(mods_ref, 3)
    pre_gain = gpre_ref[...] * (1.0 + sc)
    post_gain = gt * gpost_ref[...]
    mix_gain = mods_ref[:, 2 * D_MODEL:3 * D_MODEL] * gmix_ref[...]
    n_sub = x_ref.shape[0] // MLP_SUB_ROWS
    rows = [slice(r * MLP_SUB_ROWS, (r + 1) * MLP_SUB_ROWS) for r in range(n_sub)]

    def pre(r, y):
        x = x_ref[rows[r], :] + _rms_scale(y) * mix_gain
        return x, (_rms_scale(x) * pre_gain + sh).astype(bf16)

    def mlp_chunk(h, c, acc):
        cols = slice(c * FF_CHUNK, (c + 1) * FF_CHUNK)
        u = jnp.dot(h, win_ref[:, cols], preferred_element_type=f32)
        part = jnp.dot(_sq_relu(u), wout_ref[cols, :], preferred_element_type=f32)
        return part if acc is None else acc + part

    def post(r, x, acc):
        o_ref[rows[r], :] = x + _rms_scale(acc) * post_gain

    ys = [jnp.dot(a_ref[rows[r], :], wo_ref[...], preferred_element_type=f32) for r in range(n_sub)]
    cur = pre(0, ys[0])
    done = None
    for r in range(n_sub):
        x, h = cur
        acc = mlp_chunk(h, 0, None)
        if r + 1 < n_sub:
            cur = pre(r + 1, ys[r + 1])
        if done is not None:
            post(*done)
        for c in range(1, D_FF // FF_CHUNK):
            acc = mlp_chunk(h, c, acc)
        done = (r, x, acc)
    post(*done)


def _proj_mlp_layer(x, attn_out, w_o, g_mix_post, mods, mod_row, g_pre, g_post, w_in, w_out):
    bsz, seq_len, _ = x.shape
    tile = min(MLP_TILE, seq_len)
    assert seq_len % tile == 0 and tile % MLP_SUB_ROWS == 0
    tok = pl.BlockSpec((None, tile, D_MODEL), lambda b, i: (b, i, 0))
    row = pl.BlockSpec((1, D_MODEL), lambda b, i: (0, 0))
    whole = lambda w: pl.BlockSpec((None,) + w.shape[1:], lambda b, i: (0, 0, 0), pipeline_mode=pl.Buffered(1))
    mods_spec = pl.BlockSpec((None, 1, N_MOD * D_MODEL), lambda b, i: (mod_row(b), 0, 0))
    return pl.pallas_call(
        _proj_mlp_kernel,
        grid=(bsz, seq_len // tile),
        in_specs=[tok, tok, whole(w_o), row, mods_spec, row, row, whole(w_in), whole(w_out)],
        out_specs=tok,
        out_shape=jax.ShapeDtypeStruct(x.shape, f32),
        compiler_params=_params("arbitrary", "arbitrary"),
        name="proj_mlp_layer",
    )(x, attn_out, w_o, g_mix_post, mods, g_pre, g_post, w_in, w_out)


def _rope_tables(seq_len):
    half = HEAD_DIM // 2
    t = np.arange(seq_len)
    inv_freq = np.power(np.float32(ROPE_THETA), -np.arange(0, half, 2, dtype=np.float32) / np.float32(half))
    ang_r = (t // GRID_W).astype(np.float32)[:, None] * inv_freq
    ang_c = (t % GRID_W).astype(np.float32)[:, None] * inv_freq
    cos = np.concatenate([np.cos(ang_r), np.cos(ang_c)] * 2, axis=-1)
    sin = np.concatenate([-np.sin(ang_r), -np.sin(ang_c), np.sin(ang_r), np.sin(ang_c)], axis=-1)
    return jnp.asarray(cos, f32), jnp.asarray(sin, f32)


def _to_rope_lanes(a):
    quarters = a.reshape(a.shape[:-1] + (-1, 4, HEAD_DIM // 4))
    swapped = jnp.concatenate([quarters[..., 0:1, :], quarters[..., 2:3, :], quarters[..., 1:2, :],
                               quarters[..., 3:4, :]], axis=-2)
    return swapped.reshape(a.shape)


def _qkv_kernel(*refs, n_q, rope):
    x_ref, mods_ref, gpre_ref, w_ref, gq_ref, gk_ref = refs[:6]
    refs = refs[6:]
    if rope:
        cos_ref, sin_ref = refs[:2]
        refs = refs[2:]
    qk_ref, v_ref, h_buf, slab_buf = refs
    sh, sc, _ = _mod_slices(mods_ref, 0)
    pre_gain = gpre_ref[...] * (1.0 + sc)
    sub = slab_buf.shape[1]
    n_sub = x_ref.shape[0] // sub
    for r in range(n_sub):
        rows = slice(r * sub, (r + 1) * sub)
        h_buf[rows, :] = (_rms_scale(x_ref[rows, :]) * pre_gain + sh).astype(bf16)
    n_slabs = (n_q + N_KV_HEADS) // 2
    slab_cols = 2 * HEAD_DIM
    gq = gq_ref[...] * Q_SCALE
    same_head = (lax.broadcasted_iota(jnp.int32, (slab_cols, slab_cols), 0) // HEAD_DIM
                 == lax.broadcasted_iota(jnp.int32, (slab_cols, slab_cols), 1) // HEAD_DIM)
    head_ones = same_head.astype(bf16)
    ones_col = (lax.broadcasted_iota(jnp.int32, (sub, HEAD_DIM), 1) == 0).astype(bf16)

    for r in range(n_sub):
        rows = slice(r * sub, (r + 1) * sub)

        def project(i):
            return jnp.dot(h_buf[rows, :], w_ref[:, i * slab_cols:(i + 1) * slab_cols], preferred_element_type=f32)

        slab_buf[2 * r] = project(0)
        for i in range(n_slabs):
            slab_buf[2 * r + (i + 1) % 2] = project(i + 1)
            z = slab_buf[2 * r + i % 2]
            ssq = jnp.dot((z * z).astype(bf16), head_ones, preferred_element_type=f32)
            zn = z * lax.rsqrt(ssq * (1.0 / HEAD_DIM) + EPS)
            for j in range(2):
                head = 2 * i + j
                y = zn[:, j * HEAD_DIM:(j + 1) * HEAD_DIM] * (gq if head < n_q else gk_ref[...])
                if rope:
                    y = y * cos_ref[rows, :] + pltpu.roll(y, HEAD_DIM // 2, 1) * sin_ref[rows, :]
                qk_ref[head, rows, :] = y.astype(bf16)
        for j in range(N_KV_HEADS):
            v_ref[j, rows, 0:HEAD_DIM] = slab_buf[2 * r + n_slabs % 2, :, j * HEAD_DIM:(j + 1) * HEAD_DIM].astype(bf16)
            v_ref[j, rows, HEAD_DIM:] = ones_col


def _qkv_project(x, mods, mod_row, g_pre, w, w_col0, g_q, g_k, tile, n_q, rope):
    bsz, seq_len, _ = x.shape
    w_cols = (n_q + 2 * N_KV_HEADS) * HEAD_DIM
    sub = min(tile, ROW_TILE)
    assert N_KV_HEADS == 2 and n_q % 2 == 0 and w_col0 % w_cols == 0 and tile % sub == 0
    n_tiles = seq_len // tile
    row = lambda d: pl.BlockSpec((1, d), lambda i, b: (0, 0))
    heads = lambda n, d: pl.BlockSpec((None, n, tile, d), lambda i, b: (b, 0, i, 0))
    args = [x, mods, g_pre, w, g_q, g_k]
    specs = [
        pl.BlockSpec((None, tile, D_MODEL), lambda i, b: (b, i, 0)),
        pl.BlockSpec((None, 1, N_MOD * D_MODEL), lambda i, b: (mod_row(b), 0, 0)),
        row(D_MODEL),
        pl.BlockSpec((w.shape[0], w_cols), lambda i, b: (0, w_col0 // w_cols), pipeline_mode=pl.Buffered(1)),
        row(HEAD_DIM), row(HEAD_DIM),
    ]
    if rope:
        args += list(_rope_tables(seq_len))
        specs += [pl.BlockSpec((tile, HEAD_DIM), lambda i, b: (i, 0))] * 2
    n_qk = n_q + N_KV_HEADS
    return pl.pallas_call(
        functools.partial(_qkv_kernel, n_q=n_q, rope=rope),
        grid=(n_tiles, bsz),
        in_specs=specs,
        out_specs=[heads(n_qk, HEAD_DIM), heads(N_KV_HEADS, V_WIDTH)],
        out_shape=[jax.ShapeDtypeStruct((bsz, n_qk, seq_len, HEAD_DIM), bf16),
                   jax.ShapeDtypeStruct((bsz, N_KV_HEADS, seq_len, V_WIDTH), bf16)],
        scratch_shapes=[pltpu.VMEM((tile, D_MODEL), bf16), pltpu.VMEM((2 * (tile // sub), sub, 2 * HEAD_DIM), f32)],
        compiler_params=_params("arbitrary", "arbitrary"),
        name="qkv_project" if n_q else "kv_project",
    )(*args)


def _split_cast_refs(rest, n_casts):
    return rest[n_casts], rest[:n_casts], rest[n_casts + 1:]


def _attn_kernel(q_ref, kc_ref, vc_ref, kl_ref, vl_ref, *rest, tq, tk, n_casts):
    o_ref, cast_in, cast_out = _split_cast_refs(rest, n_casts)
    _cast_slabs(cast_in, cast_out)
    rows = Q_PER_KV * tq
    q = q_ref[...].reshape(rows, HEAD_DIM)

    def step(k, v, carry):
        m, l, acc = carry
        s = lax.dot_general(q, k, (((1,), (1,)), ((), ())), preferred_element_type=f32)
        m_new = jnp.maximum(m, jnp.max(s, axis=-1, keepdims=True))
        alpha = jnp.exp2(m - m_new)
        p = jnp.exp2(s - m_new)
        l = alpha * l + jnp.sum(p, axis=-1, keepdims=True)
        acc = alpha * acc + jnp.dot(p.astype(bf16), v, preferred_element_type=f32)
        return m_new, l, acc

    carry = (jnp.full((rows, 1), -jnp.inf, f32), jnp.zeros((rows, 1), f32), jnp.zeros((rows, HEAD_DIM), f32))
    carry = step(kc_ref[...], vc_ref[:, 0:HEAD_DIM], carry)

    def body(j, carry):
        off = pl.multiple_of(j * tk, tk)
        return step(kl_ref[pl.ds(off, tk), :], vl_ref[pl.ds(off, tk), 0:HEAD_DIM], carry)

    _, l, acc = lax.fori_loop(0, kl_ref.shape[0] // tk, body, carry)
    out = (acc / l).astype(bf16)
    for g in range(Q_PER_KV):
        o_ref[:, g * HEAD_DIM:(g + 1) * HEAD_DIM] = out[g * tq:(g + 1) * tq]


def _attn_bounded_kernel(q_ref, kc_ref, vc_ref, kl_ref, vl_ref, *rest, tq, tk, n_casts):
    o_ref, cast_in, cast_out = _split_cast_refs(rest, n_casts)
    _cast_slabs(cast_in, cast_out)
    chunks = [(k_ref, v_ref, slice(c0, min(c0 + tk, k_ref.shape[0])))
              for k_ref, v_ref in ((kc_ref, vc_ref), (kl_ref, vl_ref)) for c0 in range(0, k_ref.shape[0], tk)]
    accs = [None] * Q_PER_KV
    for k_ref, v_ref, keys in chunks:
        for g in range(Q_PER_KV):
            s = lax.dot_general(q_ref[g], k_ref[keys, :], (((1,), (1,)), ((), ())), preferred_element_type=f32)
            pv = jnp.dot(jnp.exp2(s).astype(bf16), v_ref[keys, :], preferred_element_type=f32)
            accs[g] = pv if accs[g] is None else accs[g] + pv
    for g, acc in enumerate(accs):
        o_ref[:, g * HEAD_DIM:(g + 1) * HEAD_DIM] = (acc[:, 0:HEAD_DIM] / acc[:, HEAD_DIM:HEAD_DIM + 1]).astype(bf16)


def _attention(qk_lat, k_ctx, v_ctx, v_lat, *cast_weights, bounded, cast_index):
    bsz, _, seq_len, _ = qk_lat.shape
    ctx_len = k_ctx.shape[2]
    kv = lambda n, d, h0=0: pl.BlockSpec((None, None, n, d), lambda b, h, i: (b, h0 + h, 0, 0))
    tq = ATTN_TQ_BOUNDED if bounded else ATTN_TQ
    n_tiles = seq_len // tq
    steps = bsz * N_KV_HEADS * n_tiles
    assert steps % ATTN_CAST_SLABS == 0
    slab_of_step = lambda b, h, i: ((b * N_KV_HEADS + h) * n_tiles + i) // (steps // ATTN_CAST_SLABS)
    cast_args, cast_in_specs, cast_out_specs, cast_shapes = _cast_specs(
        tuple(zip(cast_weights, cast_index)), ATTN_CAST_SLABS, slab_of_step)
    body = functools.partial(_attn_bounded_kernel if bounded else _attn_kernel, tq=tq,
                             tk=ATTN_TK_BOUNDED if bounded else ATTN_TK, n_casts=len(cast_args))
    return pl.pallas_call(
        body,
        grid=(bsz, N_KV_HEADS, n_tiles),
        in_specs=[
            pl.BlockSpec((None, Q_PER_KV, tq, HEAD_DIM), lambda b, h, i: (b, h, i, 0)),
            kv(ctx_len, HEAD_DIM), kv(ctx_len, V_WIDTH), kv(seq_len, HEAD_DIM, N_Q_HEADS), kv(seq_len, V_WIDTH),
        ] + cast_in_specs,
        out_specs=[pl.BlockSpec((None, tq, Q_PER_KV * HEAD_DIM), lambda b, h, i: (b, i, h))] + cast_out_specs,
        out_shape=[jax.ShapeDtypeStruct((bsz, seq_len, N_Q_HEADS * HEAD_DIM), bf16)] + cast_shapes,
        compiler_params=_params("arbitrary", "arbitrary", "arbitrary"),
        name="attention_bounded" if bounded else "attention",
    )(qk_lat, k_ctx, v_ctx, qk_lat, v_lat, *cast_args)


def kernel(x, c, ctx, c_ctx, w_ada, b_ada, g_mix_pre, g_mix_post, g_mlp_pre, g_mlp_post, w_pool, pool_scale,
           w_qkv, g_q, g_k, w_o, w_mlp_in, w_mlp_out):
    bsz, seq_len, d = x.shape
    ctx_len = ctx.shape[1]
    assert d == D_MODEL and seq_len % ROW_TILE == 0 and bsz + 1 <= MOD_ROWS
    assert ctx_len % POOL_HALO == 0 and (bsz * ctx_len) % ROW_TILE == 0
    row = lambda v: v.reshape(1, -1)

    cond = jnp.concatenate([c, c_ctx[None], jnp.zeros((MOD_ROWS - bsz - 1, D_MODEL), f32)], axis=0)
    mods = _ada_mods(cond, w_ada, b_ada).reshape(DEPTH, MOD_ROWS, 1, N_MOD * D_MODEL)
    lat_row = lambda b: b
    ctx_row = lambda b: bsz

    w_pool0 = w_pool[0].astype(bf16)
    layer0 = functools.partial(_pool_mlp_layer, mods=mods[0], g_mix_pre=row(g_mix_pre[0]),
                               g_mix_post=row(g_mix_post[0]), pool_scale=row(pool_scale[0]), w_pool=w_pool0,
                               g_mlp_pre=row(g_mlp_pre[0]), g_mlp_post=row(g_mlp_post[0]),
                               w_in=w_mlp_in[0:1].astype(bf16), w_out=w_mlp_out[0:1].astype(bf16))
    x = layer0(x, mod_row=lat_row)
    ctx = layer0(ctx, mod_row=ctx_row)

    qk_cols = (N_Q_HEADS + N_KV_HEADS) * HEAD_DIM
    w_qkv1 = jnp.concatenate([_to_rope_lanes(w_qkv[0, :, :qk_cols]), w_qkv[0, :, qk_cols:]], axis=-1).astype(bf16)
    qkv = functools.partial(_qkv_project, mods=mods[1], g_pre=row(g_mix_pre[1]),
                            g_q=row(_to_rope_lanes(g_q[0])), g_k=row(_to_rope_lanes(g_k[0])))
    qk_lat, v_lat = qkv(x, mod_row=lat_row, w=w_qkv1, w_col0=0, tile=QKV_TILE, n_q=N_Q_HEADS, rope=True)
    k_ctx, v_ctx = qkv(ctx, mod_row=ctx_row, w=w_qkv1, w_col0=N_Q_HEADS * HEAD_DIM, tile=ctx_len, n_q=0, rope=False)
    score_bound = 1.02 * Q_SCALE * HEAD_DIM * jnp.max(jnp.abs(g_q[0])) * jnp.max(jnp.abs(g_k[0]))
    attn = functools.partial(_attention, cast_index=(1, 1, 0))
    attn_out, w_in1, w_out1, w_o1 = lax.cond(
        score_bound <= SCORE_LOG2_LIMIT, functools.partial(attn, bounded=True), functools.partial(attn, bounded=False),
        qk_lat, k_ctx, v_ctx, v_lat, w_mlp_in, w_mlp_out, w_o)
    return _proj_mlp_layer(x, attn_out, w_o1, row(g_mix_post[1]), mods[1], lat_row,
                           row(g_mlp_pre[1]), row(g_mlp_post[1]), w_in1, w_out1)
```

```python
import functools
import math

import numpy as np
import jax
import jax.numpy as jnp
from jax import lax
from jax.experimental import pallas as pl
from jax.experimental.pallas import tpu as pltpu

D_MODEL = 1024
DEPTH = 2
GRID_W = 64
POOL_WINDOWS = (2, 4, 8, 16)
POOL_GROUP_DIM = D_MODEL // len(POOL_WINDOWS)
POOL_HALO = 8
HEAD_DIM = 128
N_Q_HEADS = D_MODEL // HEAD_DIM
N_KV_HEADS = 2
V_WIDTH = 2 * HEAD_DIM
Q_PER_KV = N_Q_HEADS // N_KV_HEADS
ROPE_THETA = 10000.0
D_FF = 4 * D_MODEL
N_MOD = 6
EPS = 1e-6
MOD_ROWS = 8
Q_SCALE = (HEAD_DIM ** -0.5) * math.log2(math.e)

ROW_TILE = 512
QKV_TILE = 1024
MLP_TILE = 1024
MLP_SUB_ROWS = 512
ATTN_TQ = 128
ATTN_TK = 512
ATTN_TQ_BOUNDED = 512
ATTN_CAST_SLABS = 64
ATTN_TK_BOUNDED = 256
SCORE_LOG2_LIMIT = 64.0
FF_CHUNK = 1024
LAYER0_FF_CHUNK = 512
CAST_CHUNKS = 8
ADA_TN = 1536
VMEM_LIMIT = 56 * 1024 * 1024

f32 = jnp.float32
bf16 = jnp.bfloat16


def _params(*semantics):
    return pltpu.CompilerParams(dimension_semantics=semantics, vmem_limit_bytes=VMEM_LIMIT)


def _rms_scale(x):
    return x * lax.rsqrt(jnp.mean(x * x, axis=-1, keepdims=True) + EPS)


def _sq_relu(u):
    ub = jnp.maximum(u.astype(bf16), 0.0)
    return ub * ub


def _cast_slabs(src_refs, dst_refs):
    for src, dst in zip(src_refs, dst_refs):
        dst[...] = src[...].astype(bf16)


def _cast_specs(casts, n_blocks, block_of_step):
    args, in_specs, out_specs, out_shape = [], [], [], []
    for weight, index in casts:
        _, rows, cols = weight.shape
        assert rows % (n_blocks * 16) == 0
        slab = (None, rows // n_blocks, cols)
        args.append(weight)
        in_specs.append(pl.BlockSpec(slab, lambda *ids, l=index: (l, block_of_step(*ids), 0)))
        out_specs.append(pl.BlockSpec(slab, lambda *ids: (0, block_of_step(*ids), 0)))
        out_shape.append(jax.ShapeDtypeStruct((1, rows, cols), bf16))
    return args, in_specs, out_specs, out_shape


def _mod_slices(mods_ref, first):
    return [mods_ref[:, (first + j) * D_MODEL:(first + j + 1) * D_MODEL] for j in range(3)]


def _ada_kernel(c_ref, w_ref, b_ref, o_ref):
    c = c_ref[...]
    s = c * jax.nn.sigmoid(c)
    o_ref[...] = jnp.dot(s.astype(bf16), w_ref[...].astype(bf16), preferred_element_type=f32) + b_ref[...]


def _ada_mods(cond, w_ada, b_ada):
    n = N_MOD * D_MODEL
    return pl.pallas_call(
        _ada_kernel,
        grid=(DEPTH, n // ADA_TN),
        in_specs=[
            pl.BlockSpec((MOD_ROWS, D_MODEL), lambda i, j: (0, 0)),
            pl.BlockSpec((None, D_MODEL, ADA_TN), lambda i, j: (i, 0, j)),
            pl.BlockSpec((None, 1, ADA_TN), lambda i, j: (i, 0, j)),
        ],
        out_specs=pl.BlockSpec((None, MOD_ROWS, ADA_TN), lambda i, j: (i, 0, j)),
        out_shape=jax.ShapeDtypeStruct((DEPTH, MOD_ROWS, n), f32),
        compiler_params=_params("arbitrary", "arbitrary"),
        name="ada_mods",
    )(cond, w_ada, b_ada.reshape(DEPTH, 1, n))


def _pool_mlp_kernel(xp_ref, x_ref, xn_ref, mods_ref, prev_mods_ref, gmix_pre_ref, gmix_post_ref, ps_ref, wpool_ref,
                     gpre_ref, gpost_ref, win_ref, wout_ref, *rest, tile, seq_len, convert):
    if convert:
        (o_ref, win_hbm_out, wout_hbm_out, hbuf, x1_buf, h_buf, lv0, lv1, lv2,
         win_vm, wout_vm, win_stage, wout_stage, in_sems, out_sems) = rest
        level_bufs = (lv0, lv1, lv2)
        win_hbm, wout_hbm, win_ref, wout_ref = win_ref, wout_ref, win_vm, wout_vm
    else:
        o_ref, hbuf, x1_buf, h_buf, *level_bufs = rest
    s = pl.program_id(0)

    def hbm_copies(src_hbm, stage, k):
        rows = stage.shape[1]
        return [pltpu.make_async_copy(src_hbm.at[0, pl.ds(c * rows, rows), :], stage.at[c % 2], in_sems.at[k, c % 2])
                for c in range(src_hbm.shape[1] // rows)]

    def out_copies():
        return [pltpu.make_async_copy(win_vm, win_hbm_out.at[0], out_sems.at[0]),
                pltpu.make_async_copy(wout_vm, wout_hbm_out.at[0], out_sems.at[1])]

    def convert_weights():
        for k, (src, stage, dst) in enumerate(((win_hbm, win_stage, win_vm), (wout_hbm, wout_stage, wout_vm))):
            copies = hbm_copies(src, stage, k)
            rows = stage.shape[1]
            copies[0].start()
            for c, cp in enumerate(copies):
                if c + 1 < len(copies):
                    copies[c + 1].start()
                cp.wait()
                dst[c * rows:(c + 1) * rows, :] = stage[c % 2].astype(bf16)
        for cp in out_copies():
            cp.start()
    n_tiles = seq_len // tile
    i = jnp.minimum(s, pl.num_programs(0) - 2) % n_tiles

    def mlp_chunk(c, acc):
        cols = slice(c * LAYER0_FF_CHUNK, (c + 1) * LAYER0_FF_CHUNK)
        u = jnp.dot(h_buf[...], win_ref[:, cols], preferred_element_type=f32)
        part = jnp.dot(_sq_relu(u), wout_ref[cols, :], preferred_element_type=f32)
        return part if acc is None else acc + part

    sh1, sc1, gt1 = _mod_slices(mods_ref, 0)
    sh2, sc2, _ = _mod_slices(mods_ref, 3)
    mix_in_gain = gmix_pre_ref[...] * (1.0 + sc1)

    def hmod(xv):
        return _rms_scale(xv) * mix_in_gain + sh1

    ext = tile + 2 * POOL_HALO
    n_groups = len(POOL_WINDOWS)

    def zero_level_padding():
        for buf in (hbuf,) + tuple(level_bufs):
            buf[ext:, :] = jnp.zeros((POOL_HALO, buf.shape[1]), f32)

    def build_level(k):
        src = hbuf if k == 1 else level_bufs[k - 2]
        lane0 = 0 if k == 1 else POOL_GROUP_DIM
        rows = ext if k < n_groups else tile
        summed = src[0:rows, lane0:] + src[2 ** (k - 1):2 ** (k - 1) + rows, lane0:]
        if k == n_groups:
            return summed
        level_bufs[k - 1][0:ext, :] = summed

    def pool_group(g, top_level=None):
        w = POOL_WINDOWS[g]
        cols = slice(g * POOL_GROUP_DIM, (g + 1) * POOL_GROUP_DIM)
        acc = (top_level if g == n_groups - 1 else
               level_bufs[g][POOL_HALO - w // 2:POOL_HALO - w // 2 + tile, 0:POOL_GROUP_DIM])
        def clipped_mean(r0):
            t = i * tile + r0 + lax.broadcasted_iota(jnp.int32, (POOL_HALO, 1), 0)
            cnt = jnp.minimum(t + (w - w // 2), seq_len) - jnp.maximum(t - w // 2, 0)
            return acc[r0:r0 + POOL_HALO] / cnt.astype(f32)

        mean = jnp.concatenate([clipped_mean(0), acc[POOL_HALO:tile - POOL_HALO] * (1.0 / w),
                                clipped_mean(tile - POOL_HALO)], axis=0)
        diff = mean - hbuf[POOL_HALO:POOL_HALO + tile, cols]
        return jnp.dot(diff.astype(bf16), wpool_ref[g], preferred_element_type=f32) * ps_ref[:, cols]

    half = tile // 2

    def fill_top():
        hbuf[POOL_HALO:POOL_HALO + half, :] = hmod(x_ref[0:half, :])
        hbuf[0:POOL_HALO, :] = jnp.where(i > 0, hmod(xp_ref[...]), 0.0)

    def fill_bottom():
        hbuf[POOL_HALO + half:POOL_HALO + tile, :] = hmod(x_ref[half:tile, :])
        hbuf[POOL_HALO + tile:ext, :] = jnp.where(i < n_tiles - 1, hmod(xn_ref[...]), 0.0)

    def step(with_mlp, with_pool):
        acc, ys = None, []
        vpu_pieces = [fill_top, fill_bottom, lambda: build_level(1),
                      lambda: (build_level(2), ys.append(pool_group(0))),
                      lambda: (ys.append(pool_group(1)), build_level(3)),
                      lambda: ys.append(pool_group(2)),
                      lambda: ys.append(pool_group(3, build_level(4)))]
        if not with_mlp:
            zero_level_padding()
            if convert:
                convert_weights()
        for c in range(D_FF // LAYER0_FF_CHUNK):
            if with_mlp:
                acc = mlp_chunk(c, acc)
            if with_pool and c < len(vpu_pieces):
                vpu_pieces[c]()
        if with_pool:
            x1 = x_ref[...] + _rms_scale(jnp.concatenate(ys, axis=-1)) * (gt1 * gmix_post_ref[...])
            h = (_rms_scale(x1) * (gpre_ref[...] * (1.0 + sc2)) + sh2).astype(bf16)
        if with_mlp:
            gt2_prev = prev_mods_ref[:, 5 * D_MODEL:6 * D_MODEL]
            o_ref[...] = x1_buf[...] + _rms_scale(acc) * (gt2_prev * gpost_ref[...])
        if with_pool:
            x1_buf[...] = x1
            h_buf[...] = h
        elif convert:
            for cp in out_copies():
                cp.wait()

    last = pl.num_programs(0) - 1
    pl.when(s == 0)(functools.partial(step, False, True))
    pl.when(jnp.logical_and(s > 0, s < last))(functools.partial(step, True, True))
    pl.when(s == last)(functools.partial(step, True, False))


def _pool_mlp_layer(x, mods, mod_row, g_mix_pre, g_mix_post, pool_scale, w_pool, g_mlp_pre, g_mlp_post,
                    w_in, w_out, convert=False):
    bsz, seq_len, _ = x.shape
    tile = min(ROW_TILE, seq_len)
    n_groups = len(POOL_WINDOWS)
    assert seq_len % tile == 0 and tile % POOL_HALO == 0 and D_FF // LAYER0_FF_CHUNK >= 7 and n_groups == 4
    assert POOL_WINDOWS == tuple(2 ** (g + 1) for g in range(n_groups)) and POOL_WINDOWS[-1] == 2 * POOL_HALO
    n_tiles = seq_len // tile
    total = bsz * n_tiles
    hb = tile // POOL_HALO
    last_halo = seq_len // POOL_HALO - 1
    cur = lambda s: jnp.minimum(s, total - 1)
    prev = lambda s: jnp.maximum(s - 1, 0)
    row = pl.BlockSpec((1, D_MODEL), lambda s: (0, 0))
    mods_spec = lambda step: pl.BlockSpec((None, 1, N_MOD * D_MODEL), lambda s: (mod_row(step(s) // n_tiles), 0, 0))
    tok = lambda step: pl.BlockSpec((None, tile, D_MODEL), lambda s: (step(s) // n_tiles, step(s) % n_tiles, 0))
    whole = lambda w: pl.BlockSpec((None,) + w.shape[1:], lambda s: (0, 0, 0), pipeline_mode=pl.Buffered(1))
    in_hbm = pl.BlockSpec(memory_space=pl.ANY)
    out_specs, out_shape = tok(prev), jax.ShapeDtypeStruct(x.shape, f32)
    scratch = ([pltpu.VMEM((tile + 3 * POOL_HALO, D_MODEL), f32),
                pltpu.VMEM((tile, D_MODEL), f32), pltpu.VMEM((tile, D_MODEL), bf16)]
               + [pltpu.VMEM((tile + 3 * POOL_HALO, D_MODEL - k * POOL_GROUP_DIM), f32) for k in range(n_groups - 1)])
    if convert:
        assert w_in.dtype == f32 and total >= 2 and D_MODEL % CAST_CHUNKS == 0 and D_FF % CAST_CHUNKS == 0
        out_specs = [out_specs, in_hbm, in_hbm]
        out_shape = [out_shape, jax.ShapeDtypeStruct((1,) + w_in.shape[1:], bf16),
                     jax.ShapeDtypeStruct((1,) + w_out.shape[1:], bf16)]
        scratch += [pltpu.VMEM(w_in.shape[1:], bf16), pltpu.VMEM(w_out.shape[1:], bf16),
                    pltpu.VMEM((2, D_MODEL // CAST_CHUNKS, D_FF), f32), pltpu.VMEM((2, D_FF // CAST_CHUNKS, D_MODEL), f32),
                    pltpu.SemaphoreType.DMA((2, 2)), pltpu.SemaphoreType.DMA((2,))]
    return pl.pallas_call(
        functools.partial(_pool_mlp_kernel, tile=tile, seq_len=seq_len, convert=convert),
        grid=(total + 1,),
        in_specs=[
            pl.BlockSpec((None, POOL_HALO, D_MODEL),
                         lambda s: (cur(s) // n_tiles, jnp.maximum(cur(s) % n_tiles * hb - 1, 0), 0)),
            tok(cur),
            pl.BlockSpec((None, POOL_HALO, D_MODEL),
                         lambda s: (cur(s) // n_tiles, jnp.minimum((cur(s) % n_tiles + 1) * hb, last_halo), 0)),
            mods_spec(cur), mods_spec(prev),
            row, row, row,
            pl.BlockSpec(w_pool.shape, lambda s: (0, 0, 0)),
            row, row,
        ] + ([in_hbm, in_hbm] if convert else [whole(w_in), whole(w_out)]),
        out_specs=out_specs,
        out_shape=out_shape,
        scratch_shapes=scratch,
        compiler_params=_params("arbitrary"),
        name="pool_mlp_layer_convert" if convert else "pool_mlp_layer",
    )(x, x, x, mods, mods, g_mix_pre, g_mix_post, pool_scale, w_pool, g_mlp_pre, g_mlp_post, w_in, w_out)


def _proj_mlp_kernel(x_ref, a_ref, wo_ref, gmix_ref, mods_ref, gpre_ref, gpost_ref, win_ref, wout_ref, o_ref):
    sh, sc, gt = _mod_slices(mods_ref, 3)
    pre_gain = gpre_ref[...] * (1.0 + sc)
    post_gain = gt * gpost_ref[...]
    mix_gain = mods_ref[:, 2 * D_MODEL:3 * D_MODEL] * gmix_ref[...]
    n_sub = x_ref.shape[0] // MLP_SUB_ROWS
    rows = [slice(r * MLP_SUB_ROWS, (r + 1) * MLP_SUB_ROWS) for r in range(n_sub)]

    def pre(r, y):
        x = x_ref[rows[r], :] + _rms_scale(y) * mix_gain
        return x, (_rms_scale(x) * pre_gain + sh).astype(bf16)

    def mlp_chunk(h, c, acc):
        cols = slice(c * FF_CHUNK, (c + 1) * FF_CHUNK)
        u = jnp.dot(h, win_ref[:, cols], preferred_element_type=f32)
        part = jnp.dot(_sq_relu(u), wout_ref[cols, :], preferred_element_type=f32)
        return part if acc is None else acc + part

    def post(r, x, acc):
        o_ref[rows[r], :] = x + _rms_scale(acc) * post_gain

    ys = [jnp.dot(a_ref[rows[r], :], wo_ref[...], preferred_element_type=f32) for r in range(n_sub)]
    cur = pre(0, ys[0])
    done = None
    for r in range(n_sub):
        x, h = cur
        acc = mlp_chunk(h, 0, None)
        if r + 1 < n_sub:
            cur = pre(r + 1, ys[r + 1])
        if done is not None:
            post(*done)
        for c in range(1, D_FF // FF_CHUNK):
            acc = mlp_chunk(h, c, acc)
        done = (r, x, acc)
    post(*done)


def _proj_mlp_layer(x, attn_out, w_o, g_mix_post, mods, mod_row, g_pre, g_post, w_in, w_out):
    bsz, seq_len, _ = x.shape
    tile = min(MLP_TILE, seq_len)
    assert seq_len % tile == 0 and tile % MLP_SUB_ROWS == 0
    tok = pl.BlockSpec((None, tile, D_MODEL), lambda b, i: (b, i, 0))
    row = pl.BlockSpec((1, D_MODEL), lambda b, i: (0, 0))
    whole = lambda w: pl.BlockSpec((None,) + w.shape[1:], lambda b, i: (0, 0, 0), pipeline_mode=pl.Buffered(1))
    mods_spec = pl.BlockSpec((None, 1, N_MOD * D_MODEL), lambda b, i: (mod_row(b), 0, 0))
    return pl.pallas_call(
        _proj_mlp_kernel,
        grid=(bsz, seq_len // tile),
        in_specs=[tok, tok, whole(w_o), row, mods_spec, row, row, whole(w_in), whole(w_out)],
        out_specs=tok,
        out_shape=jax.ShapeDtypeStruct(x.shape, f32),
        compiler_params=_params("arbitrary", "arbitrary"),
        name="proj_mlp_layer",
    )(x, attn_out, w_o, g_mix_post, mods, g_pre, g_post, w_in, w_out)


def _rope_tables(seq_len):
    half = HEAD_DIM // 2
    t = np.arange(seq_len)
    inv_freq = np.power(np.float32(ROPE_THETA), -np.arange(0, half, 2, dtype=np.float32) / np.float32(half))
    ang_r = (t // GRID_W).astype(np.float32)[:, None] * inv_freq
    ang_c = (t % GRID_W).astype(np.float32)[:, None] * inv_freq
    cos = np.concatenate([np.cos(ang_r), np.cos(ang_c)] * 2, axis=-1)
    sin = np.concatenate([-np.sin(ang_r), -np.sin(ang_c), np.sin(ang_r), np.sin(ang_c)], axis=-1)
    return jnp.asarray(cos, f32), jnp.asarray(sin, f32)


def _to_rope_lanes(a):
    quarters = a.reshape(a.shape[:-1] + (-1, 4, HEAD_DIM // 4))
    swapped = jnp.concatenate([quarters[..., 0:1, :], quarters[..., 2:3, :], quarters[..., 1:2, :],
                               quarters[..., 3:4, :]], axis=-2)
    return swapped.reshape(a.shape)


def _qkv_kernel(*refs, n_q, rope):
    x_ref, mods_ref, gpre_ref, w_ref, gq_ref, gk_ref = refs[:6]
    refs = refs[6:]
    if rope:
        cos_ref, sin_ref = refs[:2]
        refs = refs[2:]
    qk_ref, v_ref, h_buf, slab_buf = refs
    sh, sc, _ = _mod_slices(mods_ref, 0)
    pre_gain = gpre_ref[...] * (1.0 + sc)
    sub = slab_buf.shape[1]
    n_sub = x_ref.shape[0] // sub
    for r in range(n_sub):
        rows = slice(r * sub, (r + 1) * sub)
        h_buf[rows, :] = (_rms_scale(x_ref[rows, :]) * pre_gain + sh).astype(bf16)
    n_slabs = (n_q + N_KV_HEADS) // 2
    slab_cols = 2 * HEAD_DIM
    gq = gq_ref[...] * Q_SCALE
    same_head = (lax.broadcasted_iota(jnp.int32, (slab_cols, slab_cols), 0) // HEAD_DIM
                 == lax.broadcasted_iota(jnp.int32, (slab_cols, slab_cols), 1) // HEAD_DIM)
    head_ones = same_head.astype(bf16)
    ones_col = (lax.broadcasted_iota(jnp.int32, (sub, HEAD_DIM), 1) == 0).astype(bf16)

    for r in range(n_sub):
        rows = slice(r * sub, (r + 1) * sub)

        def project(i):
            return jnp.dot(h_buf[rows, :], w_ref[:, i * slab_cols:(i + 1) * slab_cols], preferred_element_type=f32)

        slab_buf[2 * r] = project(0)
        for i in range(n_slabs):
            slab_buf[2 * r + (i + 1) % 2] = project(i + 1)
            z = slab_buf[2 * r + i % 2]
            ssq = jnp.dot((z * z).astype(bf16), head_ones, preferred_element_type=f32)
            zn = z * lax.rsqrt(ssq * (1.0 / HEAD_DIM) + EPS)
            for j in range(2):
                head = 2 * i + j
                y = zn[:, j * HEAD_DIM:(j + 1) * HEAD_DIM] * (gq if head < n_q else gk_ref[...])
                if rope:
                    y = y * cos_ref[rows, :] + pltpu.roll(y, HEAD_DIM // 2, 1) * sin_ref[rows, :]
                qk_ref[head, rows, :] = y.astype(bf16)
        for j in range(N_KV_HEADS):
            v_ref[j, rows, 0:HEAD_DIM] = slab_buf[2 * r + n_slabs % 2, :, j * HEAD_DIM:(j + 1) * HEAD_DIM].astype(bf16)
            v_ref[j, rows, HEAD_DIM:] = ones_col


def _qkv_project(x, mods, mod_row, g_pre, w, w_col0, g_q, g_k, tile, n_q, rope):
    bsz, seq_len, _ = x.shape
    w_cols = (n_q + 2 * N_KV_HEADS) * HEAD_DIM
    sub = min(tile, ROW_TILE)
    assert N_KV_HEADS == 2 and n_q % 2 == 0 and w_col0 % w_cols == 0 and tile % sub == 0
    n_tiles = seq_len // tile
    row = lambda d: pl.BlockSpec((1, d), lambda i, b: (0, 0))
    heads = lambda n, d: pl.BlockSpec((None, n, tile, d), lambda i, b: (b, 0, i, 0))
    args = [x, mods, g_pre, w, g_q, g_k]
    specs = [
        pl.BlockSpec((None, tile, D_MODEL), lambda i, b: (b, i, 0)),
        pl.BlockSpec((None, 1, N_MOD * D_MODEL), lambda i, b: (mod_row(b), 0, 0)),
        row(D_MODEL),
        pl.BlockSpec((w.shape[0], w_cols), lambda i, b: (0, w_col0 // w_cols), pipeline_mode=pl.Buffered(1)),
        row(HEAD_DIM), row(HEAD_DIM),
    ]
    if rope:
        args += list(_rope_tables(seq_len))
        specs += [pl.BlockSpec((tile, HEAD_DIM), lambda i, b: (i, 0))] * 2
    n_qk = n_q + N_KV_HEADS
    return pl.pallas_call(
        functools.partial(_qkv_kernel, n_q=n_q, rope=rope),
        grid=(n_tiles, bsz),
        in_specs=specs,
        out_specs=[heads(n_qk, HEAD_DIM), heads(N_KV_HEADS, V_WIDTH)],
        out_shape=[jax.ShapeDtypeStruct((bsz, n_qk, seq_len, HEAD_DIM), bf16),
                   jax.ShapeDtypeStruct((bsz, N_KV_HEADS, seq_len, V_WIDTH), bf16)],
        scratch_shapes=[pltpu.VMEM((tile, D_MODEL), bf16), pltpu.VMEM((2 * (tile // sub), sub, 2 * HEAD_DIM), f32)],
        compiler_params=_params("arbitrary", "arbitrary"),
        name="qkv_project" if n_q else "kv_project",
    )(*args)


def _split_cast_refs(rest, n_casts):
    return rest[n_casts], rest[:n_casts], rest[n_casts + 1:]


def _attn_kernel(q_ref, kc_ref, vc_ref, kl_ref, vl_ref, *rest, tq, tk, n_casts):
    o_ref, cast_in, cast_out = _split_cast_refs(rest, n_casts)
    _cast_slabs(cast_in, cast_out)
    rows = Q_PER_KV * tq
    q = q_ref[...].reshape(rows, HEAD_DIM)

    def step(k, v, carry):
        m, l, acc = carry
        s = lax.dot_general(q, k, (((1,), (1,)), ((), ())), preferred_element_type=f32)
        m_new = jnp.maximum(m, jnp.max(s, axis=-1, keepdims=True))
        alpha = jnp.exp2(m - m_new)
        p = jnp.exp2(s - m_new)
        l = alpha * l + jnp.sum(p, axis=-1, keepdims=True)
        acc = alpha * acc + jnp.dot(p.astype(bf16), v, preferred_element_type=f32)
        return m_new, l, acc

    carry = (jnp.full((rows, 1), -jnp.inf, f32), jnp.zeros((rows, 1), f32), jnp.zeros((rows, HEAD_DIM), f32))
    carry = step(kc_ref[...], vc_ref[:, 0:HEAD_DIM], carry)

    def body(j, carry):
        off = pl.multiple_of(j * tk, tk)
        return step(kl_ref[pl.ds(off, tk), :], vl_ref[pl.ds(off, tk), 0:HEAD_DIM], carry)

    _, l, acc = lax.fori_loop(0, kl_ref.shape[0] // tk, body, carry)
    out = (acc / l).astype(bf16)
    for g in range(Q_PER_KV):
        o_ref[:, g * HEAD_DIM:(g + 1) * HEAD_DIM] = out[g * tq:(g + 1) * tq]


def _attn_bounded_kernel(q_ref, kc_ref, vc_ref, kl_ref, vl_ref, *rest, tq, tk, n_casts):
    o_ref, cast_in, cast_out = _split_cast_refs(rest, n_casts)
    _cast_slabs(cast_in, cast_out)
    chunks = [(k_ref, v_ref, slice(c0, min(c0 + tk, k_ref.shape[0])))
              for k_ref, v_ref in ((kc_ref, vc_ref), (kl_ref, vl_ref)) for c0 in range(0, k_ref.shape[0], tk)]
    accs = [None] * Q_PER_KV
    for k_ref, v_ref, keys in chunks:
        for g in range(Q_PER_KV):
            s = lax.dot_general(q_ref[g], k_ref[keys, :], (((1,), (1,)), ((), ())), preferred_element_type=f32)
            pv = jnp.dot(jnp.exp2(s).astype(bf16), v_ref[keys, :], preferred_element_type=f32)
            accs[g] = pv if accs[g] is None else accs[g] + pv
    for g, acc in enumerate(accs):
        o_ref[:, g * HEAD_DIM:(g + 1) * HEAD_DIM] = (acc[:, 0:HEAD_DIM] / acc[:, HEAD_DIM:HEAD_DIM + 1]).astype(bf16)


def _attention(qk_lat, k_ctx, v_ctx, v_lat, *cast_weights, bounded, cast_index):
    bsz, _, seq_len, _ = qk_lat.shape
    ctx_len = k_ctx.shape[2]
    kv = lambda n, d, h0=0: pl.BlockSpec((None, None, n, d), lambda b, h, i: (b, h0 + h, 0, 0))
    tq = ATTN_TQ_BOUNDED if bounded else ATTN_TQ
    n_tiles = seq_len // tq
    steps = bsz * N_KV_HEADS * n_tiles
    assert steps % ATTN_CAST_SLABS == 0
    slab_of_step = lambda b, h, i: ((b * N_KV_HEADS + h) * n_tiles + i) // (steps // ATTN_CAST_SLABS)
    cast_args, cast_in_specs, cast_out_specs, cast_shapes = _cast_specs(
        tuple(zip(cast_weights, cast_index)), ATTN_CAST_SLABS, slab_of_step)
    body = functools.partial(_attn_bounded_kernel if bounded else _attn_kernel, tq=tq,
                             tk=ATTN_TK_BOUNDED if bounded else ATTN_TK, n_casts=len(cast_args))
    return pl.pallas_call(
        body,
        grid=(bsz, N_KV_HEADS, n_tiles),
        in_specs=[
            pl.BlockSpec((None, Q_PER_KV, tq, HEAD_DIM), lambda b, h, i: (b, h, i, 0)),
            kv(ctx_len, HEAD_DIM), kv(ctx_len, V_WIDTH), kv(seq_len, HEAD_DIM, N_Q_HEADS), kv(seq_len, V_WIDTH),
        ] + cast_in_specs,
        out_specs=[pl.BlockSpec((None, tq, Q_PER_KV * HEAD_DIM), lambda b, h, i: (b, i, h))] + cast_out_specs,
        out_shape=[jax.ShapeDtypeStruct((bsz, seq_len, N_Q_HEADS * HEAD_DIM), bf16)] + cast_shapes,
        compiler_params=_params("arbitrary", "arbitrary", "arbitrary"),
        name="attention_bounded" if bounded else "attention",
    )(qk_lat, k_ctx, v_ctx, qk_lat, v_lat, *cast_args)


def kernel(x, c, ctx, c_ctx, w_ada, b_ada, g_mix_pre, g_mix_post, g_mlp_pre, g_mlp_post, w_pool, pool_scale,
           w_qkv, g_q, g_k, w_o, w_mlp_in, w_mlp_out):
    bsz, seq_len, d = x.shape
    ctx_len = ctx.shape[1]
    assert d == D_MODEL and seq_len % ROW_TILE == 0 and bsz + 1 <= MOD_ROWS
    assert ctx_len % POOL_HALO == 0 and (bsz * ctx_len) % ROW_TILE == 0
    row = lambda v: v.reshape(1, -1)

    cond = jnp.concatenate([c, c_ctx[None], jnp.zeros((MOD_ROWS - bsz - 1, D_MODEL), f32)], axis=0)
    mods = _ada_mods(cond, w_ada, b_ada).reshape(DEPTH, MOD_ROWS, 1, N_MOD * D_MODEL)
    lat_row = lambda b: b
    ctx_row = lambda b: bsz

    w_pool0 = w_pool[0].astype(bf16)
    layer0 = functools.partial(_pool_mlp_layer, mods=mods[0], g_mix_pre=row(g_mix_pre[0]),
                               g_mix_post=row(g_mix_post[0]), pool_scale=row(pool_scale[0]), w_pool=w_pool0,
                               g_mlp_pre=row(g_mlp_pre[0]), g_mlp_post=row(g_mlp_post[0]))
    ctx, w_in0, w_out0 = layer0(ctx, mod_row=ctx_row, w_in=w_mlp_in, w_out=w_mlp_out, convert=True)
    x = layer0(x, mod_row=lat_row, w_in=w_in0, w_out=w_out0)

    qk_cols = (N_Q_HEADS + N_KV_HEADS) * HEAD_DIM
    w_qkv1 = jnp.concatenate([_to_rope_lanes(w_qkv[0, :, :qk_cols]), w_qkv[0, :, qk_cols:]], axis=-1).astype(bf16)
    qkv = functools.partial(_qkv_project, mods=mods[1], g_pre=row(g_mix_pre[1]),
                            g_q=row(_to_rope_lanes(g_q[0])), g_k=row(_to_rope_lanes(g_k[0])))
    qk_lat, v_lat = qkv(x, mod_row=lat_row, w=w_qkv1, w_col0=0, tile=QKV_TILE, n_q=N_Q_HEADS, rope=True)
    k_ctx, v_ctx = qkv(ctx, mod_row=ctx_row, w=w_qkv1, w_col0=N_Q_HEADS * HEAD_DIM, tile=ctx_len, n_q=0, rope=False)
    score_bound = 1.02 * Q_SCALE * HEAD_DIM * jnp.max(jnp.abs(g_q[0])) * jnp.max(jnp.abs(g_k[0]))
    attn = functools.partial(_attention, cast_index=(1, 1, 0))
    attn_out, w_in1, w_out1, w_o1 = lax.cond(
        score_bound <= SCORE_LOG2_LIMIT, functools.partial(attn, bounded=True), functools.partial(attn, bounded=False),
        qk_lat, k_ctx, v_ctx, v_lat, w_mlp_in, w_mlp_out, w_o)
    return _proj_mlp_layer(x, attn_out, w_o1, row(g_mix_post[1]), mods[1], lat_row,
                           row(g_mlp_pre[1]), row(g_mlp_post[1]), w_in1, w_out1)
```

```python
import functools
import math

import numpy as np
import jax
import jax.numpy as jnp
from jax import lax
from jax.experimental import pallas as pl
from jax.experimental.pallas import tpu as pltpu

D_MODEL = 1024
DEPTH = 2
GRID_W = 64
POOL_WINDOWS = (2, 4, 8, 16)
POOL_GROUP_DIM = D_MODEL // len(POOL_WINDOWS)
POOL_HALO = 8
HEAD_DIM = 128
N_Q_HEADS = D_MODEL // HEAD_DIM
N_KV_HEADS = 2
V_WIDTH = 2 * HEAD_DIM
Q_PER_KV = N_Q_HEADS // N_KV_HEADS
ROPE_THETA = 10000.0
D_FF = 4 * D_MODEL
N_MOD = 6
EPS = 1e-6
MOD_ROWS = 8
Q_SCALE = (HEAD_DIM ** -0.5) * math.log2(math.e)

ROW_TILE = 512
QKV_TILE = 1024
MLP_TILE = 1024
MLP_SUB_ROWS = 512
ATTN_TQ = 128
ATTN_TK = 512
ATTN_TQ_BOUNDED = 512
ATTN_CAST_SLABS = 64
ATTN_TK_BOUNDED = 256
SCORE_LOG2_LIMIT = 64.0
FF_CHUNK = 1024
LAYER0_FF_CHUNK = 512
CAST_CHUNKS = 8
ADA_TN = 1536
VMEM_LIMIT = 56 * 1024 * 1024

f32 = jnp.float32
bf16 = jnp.bfloat16


def _params(*semantics):
    return pltpu.CompilerParams(dimension_semantics=semantics, vmem_limit_bytes=VMEM_LIMIT)


def _rms_scale(x):
    return x * lax.rsqrt(jnp.mean(x * x, axis=-1, keepdims=True) + EPS)


def _sq_relu(u):
    ub = jnp.maximum(u.astype(bf16), 0.0)
    return ub * ub


def _cast_slabs(src_refs, dst_refs):
    for src, dst in zip(src_refs, dst_refs):
        dst[...] = src[...].astype(bf16)


def _cast_specs(casts, n_blocks, block_of_step):
    args, in_specs, out_specs, out_shape = [], [], [], []
    for weight, index in casts:
        _, rows, cols = weight.shape
        assert rows % (n_blocks * 16) == 0
        slab = (None, rows // n_blocks, cols)
        args.append(weight)
        in_specs.append(pl.BlockSpec(slab, lambda *ids, l=index: (l, block_of_step(*ids), 0)))
        out_specs.append(pl.BlockSpec(slab, lambda *ids: (0, block_of_step(*ids), 0)))
        out_shape.append(jax.ShapeDtypeStruct((1, rows, cols), bf16))
    return args, in_specs, out_specs, out_shape


def _mod_slices(mods_ref, first):
    return [mods_ref[:, (first + j) * D_MODEL:(first + j + 1) * D_MODEL] for j in range(3)]


def _ada_kernel(c_ref, w_ref, b_ref, o_ref):
    c = c_ref[...]
    s = c * jax.nn.sigmoid(c)
    o_ref[...] = jnp.dot(s.astype(bf16), w_ref[...].astype(bf16), preferred_element_type=f32) + b_ref[...]


def _ada_mods(cond, w_ada, b_ada):
    n = N_MOD * D_MODEL
    return pl.pallas_call(
        _ada_kernel,
        grid=(DEPTH, n // ADA_TN),
        in_specs=[
            pl.BlockSpec((MOD_ROWS, D_MODEL), lambda i, j: (0, 0)),
            pl.BlockSpec((None, D_MODEL, ADA_TN), lambda i, j: (i, 0, j)),
            pl.BlockSpec((None, 1, ADA_TN), lambda i, j: (i, 0, j)),
        ],
        out_specs=pl.BlockSpec((None, MOD_ROWS, ADA_TN), lambda i, j: (i, 0, j)),
        out_shape=jax.ShapeDtypeStruct((DEPTH, MOD_ROWS, n), f32),
        compiler_params=_params("arbitrary", "arbitrary"),
        name="ada_mods",
    )(cond, w_ada, b_ada.reshape(DEPTH, 1, n))


def _pool_mlp_kernel(xp_ref, x_ref, xn_ref, mods_ref, prev_mods_ref, gmix_pre_ref, gmix_post_ref, ps_ref, wpool_ref,
                     gpre_ref, gpost_ref, win_ref, wout_ref, *rest, tile, seq_len, convert):
    if convert:
        (o_ref, win_hbm_out, wout_hbm_out, hbuf, x1_buf, h_buf, lv0, lv1, lv2,
         win_vm, wout_vm, win_stage, wout_stage, in_sems, out_sems) = rest
        level_bufs = (lv0, lv1, lv2)
        win_hbm, wout_hbm, win_ref, wout_ref = win_ref, wout_ref, win_vm, wout_vm
    else:
        o_ref, hbuf, x1_buf, h_buf, *level_bufs = rest
    s = pl.program_id(0)

    def hbm_copies(src_hbm, stage, k):
        rows = stage.shape[1]
        return [pltpu.make_async_copy(src_hbm.at[0, pl.ds(c * rows, rows), :], stage.at[c % 2], in_sems.at[k, c % 2])
                for c in range(src_hbm.shape[1] // rows)]

    def out_copies():
        return [pltpu.make_async_copy(win_vm, win_hbm_out.at[0], out_sems.at[0]),
                pltpu.make_async_copy(wout_vm, wout_hbm_out.at[0], out_sems.at[1])]

    def convert_w_in():
        copies = hbm_copies(win_hbm, win_stage, 0)
        rows = win_stage.shape[1]
        copies[0].start()
        for c, cp in enumerate(copies):
            if c + 1 < len(copies):
                copies[c + 1].start()
            cp.wait()
            win_vm[c * rows:(c + 1) * rows, :] = win_stage[c % 2].astype(bf16)
        out_copies()[0].start()
        hbm_copies(wout_hbm, wout_stage, 1)[0].start()

    def convert_w_out_chunk(c):
        copies = hbm_copies(wout_hbm, wout_stage, 1)
        rows = wout_stage.shape[1]
        if c + 1 < len(copies):
            copies[c + 1].start()
        copies[c].wait()
        wout_vm[c * rows:(c + 1) * rows, :] = wout_stage[c % 2].astype(bf16)
    n_tiles = seq_len // tile
    i = jnp.minimum(s, pl.num_programs(0) - 2) % n_tiles

    def mlp_chunk(c, acc, before_down=None):
        cols = slice(c * LAYER0_FF_CHUNK, (c + 1) * LAYER0_FF_CHUNK)
        u = jnp.dot(h_buf[...], win_ref[:, cols], preferred_element_type=f32)
        if before_down is not None:
            before_down(c)
        part = jnp.dot(_sq_relu(u), wout_ref[cols, :], preferred_element_type=f32)
        return part if acc is None else acc + part

    sh1, sc1, gt1 = _mod_slices(mods_ref, 0)
    sh2, sc2, _ = _mod_slices(mods_ref, 3)
    mix_in_gain = gmix_pre_ref[...] * (1.0 + sc1)

    def hmod(xv):
        return _rms_scale(xv) * mix_in_gain + sh1

    ext = tile + 2 * POOL_HALO
    n_groups = len(POOL_WINDOWS)

    def zero_level_padding():
        for buf in (hbuf,) + tuple(level_bufs):
            buf[ext:, :] = jnp.zeros((POOL_HALO, buf.shape[1]), f32)

    def build_level(k):
        src = hbuf if k == 1 else level_bufs[k - 2]
        lane0 = 0 if k == 1 else POOL_GROUP_DIM
        rows = ext if k < n_groups else tile
        summed = src[0:rows, lane0:] + src[2 ** (k - 1):2 ** (k - 1) + rows, lane0:]
        if k == n_groups:
            return summed
        level_bufs[k - 1][0:ext, :] = summed

    def pool_group(g, top_level=None):
        w = POOL_WINDOWS[g]
        cols = slice(g * POOL_GROUP_DIM, (g + 1) * POOL_GROUP_DIM)
        acc = (top_level if g == n_groups - 1 else
               level_bufs[g][POOL_HALO - w // 2:POOL_HALO - w // 2 + tile, 0:POOL_GROUP_DIM])
        def clipped_mean(r0):
            t = i * tile + r0 + lax.broadcasted_iota(jnp.int32, (POOL_HALO, 1), 0)
            cnt = jnp.minimum(t + (w - w // 2), seq_len) - jnp.maximum(t - w // 2, 0)
            return acc[r0:r0 + POOL_HALO] / cnt.astype(f32)

        mean = jnp.concatenate([clipped_mean(0), acc[POOL_HALO:tile - POOL_HALO] * (1.0 / w),
                                clipped_mean(tile - POOL_HALO)], axis=0)
        diff = mean - hbuf[POOL_HALO:POOL_HALO + tile, cols]
        return jnp.dot(diff.astype(bf16), wpool_ref[g], preferred_element_type=f32) * ps_ref[:, cols]

    half = tile // 2

    def fill_top():
        hbuf[POOL_HALO:POOL_HALO + half, :] = hmod(x_ref[0:half, :])
        hbuf[0:POOL_HALO, :] = jnp.where(i > 0, hmod(xp_ref[...]), 0.0)

    def fill_bottom():
        hbuf[POOL_HALO + half:POOL_HALO + tile, :] = hmod(x_ref[half:tile, :])
        hbuf[POOL_HALO + tile:ext, :] = jnp.where(i < n_tiles - 1, hmod(xn_ref[...]), 0.0)

    def step(with_mlp, with_pool):
        acc, ys = None, []
        vpu_pieces = [fill_top, fill_bottom, lambda: build_level(1),
                      lambda: (build_level(2), ys.append(pool_group(0))),
                      lambda: (ys.append(pool_group(1)), build_level(3)),
                      lambda: ys.append(pool_group(2)),
                      lambda: ys.append(pool_group(3, build_level(4)))]
        if not with_mlp:
            zero_level_padding()
            if convert:
                convert_w_in()
        convert_here = convert and with_mlp and with_pool
        for c in range(D_FF // LAYER0_FF_CHUNK):
            if with_mlp:
                acc = mlp_chunk(c, acc, convert_w_out_chunk if convert_here else None)
            if with_pool and c < len(vpu_pieces):
                vpu_pieces[c]()
        if convert_here:
            out_copies()[1].start()
        if with_pool:
            x1 = x_ref[...] + _rms_scale(jnp.concatenate(ys, axis=-1)) * (gt1 * gmix_post_ref[...])
            h = (_rms_scale(x1) * (gpre_ref[...] * (1.0 + sc2)) + sh2).astype(bf16)
        if with_mlp:
            gt2_prev = prev_mods_ref[:, 5 * D_MODEL:6 * D_MODEL]
            o_ref[...] = x1_buf[...] + _rms_scale(acc) * (gt2_prev * gpost_ref[...])
        if with_pool:
            x1_buf[...] = x1
            h_buf[...] = h
        elif convert:
            for cp in out_copies():
                cp.wait()

    last = pl.num_programs(0) - 1
    pl.when(s == 0)(functools.partial(step, False, True))
    pl.when(jnp.logical_and(s > 0, s < last))(functools.partial(step, True, True))
    pl.when(s == last)(functools.partial(step, True, False))


def _pool_mlp_layer(x, mods, mod_row, g_mix_pre, g_mix_post, pool_scale, w_pool, g_mlp_pre, g_mlp_post,
                    w_in, w_out, convert=False):
    bsz, seq_len, _ = x.shape
    tile = min(ROW_TILE, seq_len)
    n_groups = len(POOL_WINDOWS)
    assert seq_len % tile == 0 and tile % POOL_HALO == 0 and D_FF // LAYER0_FF_CHUNK >= 7 and n_groups == 4
    assert POOL_WINDOWS == tuple(2 ** (g + 1) for g in range(n_groups)) and POOL_WINDOWS[-1] == 2 * POOL_HALO
    n_tiles = seq_len // tile
    total = bsz * n_tiles
    hb = tile // POOL_HALO
    last_halo = seq_len // POOL_HALO - 1
    cur = lambda s: jnp.minimum(s, total - 1)
    prev = lambda s: jnp.maximum(s - 1, 0)
    row = pl.BlockSpec((1, D_MODEL), lambda s: (0, 0))
    mods_spec = lambda step: pl.BlockSpec((None, 1, N_MOD * D_MODEL), lambda s: (mod_row(step(s) // n_tiles), 0, 0))
    tok = lambda step: pl.BlockSpec((None, tile, D_MODEL), lambda s: (step(s) // n_tiles, step(s) % n_tiles, 0))
    whole = lambda w: pl.BlockSpec((None,) + w.shape[1:], lambda s: (0, 0, 0), pipeline_mode=pl.Buffered(1))
    in_hbm = pl.BlockSpec(memory_space=pl.ANY)
    out_specs, out_shape = tok(prev), jax.ShapeDtypeStruct(x.shape, f32)
    scratch = ([pltpu.VMEM((tile + 3 * POOL_HALO, D_MODEL), f32),
                pltpu.VMEM((tile, D_MODEL), f32), pltpu.VMEM((tile, D_MODEL), bf16)]
               + [pltpu.VMEM((tile + 3 * POOL_HALO, D_MODEL - k * POOL_GROUP_DIM), f32) for k in range(n_groups - 1)])
    if convert:
        assert w_in.dtype == f32 and total == 2 and D_MODEL % CAST_CHUNKS == 0 and D_FF // CAST_CHUNKS == LAYER0_FF_CHUNK
        out_specs = [out_specs, in_hbm, in_hbm]
        out_shape = [out_shape, jax.ShapeDtypeStruct((1,) + w_in.shape[1:], bf16),
                     jax.ShapeDtypeStruct((1,) + w_out.shape[1:], bf16)]
        scratch += [pltpu.VMEM(w_in.shape[1:], bf16), pltpu.VMEM(w_out.shape[1:], bf16),
                    pltpu.VMEM((2, D_MODEL // CAST_CHUNKS, D_FF), f32), pltpu.VMEM((2, D_FF // CAST_CHUNKS, D_MODEL), f32),
                    pltpu.SemaphoreType.DMA((2, 2)), pltpu.SemaphoreType.DMA((2,))]
    return pl.pallas_call(
        functools.partial(_pool_mlp_kernel, tile=tile, seq_len=seq_len, convert=convert),
        grid=(total + 1,),
        in_specs=[
            pl.BlockSpec((None, POOL_HALO, D_MODEL),
                         lambda s: (cur(s) // n_tiles, jnp.maximum(cur(s) % n_tiles * hb - 1, 0), 0)),
            tok(cur),
            pl.BlockSpec((None, POOL_HALO, D_MODEL),
                         lambda s: (cur(s) // n_tiles, jnp.minimum((cur(s) % n_tiles + 1) * hb, last_halo), 0)),
            mods_spec(cur), mods_spec(prev),
            row, row, row,
            pl.BlockSpec(w_pool.shape, lambda s: (0, 0, 0)),
            row, row,
        ] + ([in_hbm, in_hbm] if convert else [whole(w_in), whole(w_out)]),
        out_specs=out_specs,
        out_shape=out_shape,
        scratch_shapes=scratch,
        compiler_params=_params("arbitrary"),
        name="pool_mlp_layer_convert" if convert else "pool_mlp_layer",
    )(x, x, x, mods, mods, g_mix_pre, g_mix_post, pool_scale, w_pool, g_mlp_pre, g_mlp_post, w_in, w_out)


def _proj_mlp_kernel(x_ref, a_ref, wo_ref, gmix_ref, mods_ref, gpre_ref, gpost_ref, win_ref, wout_ref, o_ref):
    sh, sc, gt = _mod_slices(mods_ref, 3)
    pre_gain = gpre_ref[...] * (1.0 + sc)
    post_gain = gt * gpost_ref[...]
    mix_gain = mods_ref[:, 2 * D_MODEL:3 * D_MODEL] * gmix_ref[...]
    n_sub = x_ref.shape[0] // MLP_SUB_ROWS
    rows = [slice(r * MLP_SUB_ROWS, (r + 1) * MLP_SUB_ROWS) for r in range(n_sub)]

    def pre(r, y):
        x = x_ref[rows[r], :] + _rms_scale(y) * mix_gain
        return x, (_rms_scale(x) * pre_gain + sh).astype(bf16)

    def mlp_chunk(h, c, acc):
        cols = slice(c * FF_CHUNK, (c + 1) * FF_CHUNK)
        u = jnp.dot(h, win_ref[:, cols], preferred_element_type=f32)
        part = jnp.dot(_sq_relu(u), wout_ref[cols, :], preferred_element_type=f32)
        return part if acc is None else acc + part

    def post(r, x, acc):
        o_ref[rows[r], :] = x + _rms_scale(acc) * post_gain

    ys = [jnp.dot(a_ref[rows[r], :], wo_ref[...], preferred_element_type=f32) for r in range(n_sub)]
    cur = pre(0, ys[0])
    done = None
    for r in range(n_sub):
        x, h = cur
        acc = mlp_chunk(h, 0, None)
        if r + 1 < n_sub:
            cur = pre(r + 1, ys[r + 1])
        if done is not None:
            post(*done)
        for c in range(1, D_FF // FF_CHUNK):
            acc = mlp_chunk(h, c, acc)
        done = (r, x, acc)
    post(*done)


def _proj_mlp_layer(x, attn_out, w_o, g_mix_post, mods, mod_row, g_pre, g_post, w_in, w_out):
    bsz, seq_len, _ = x.shape
    tile = min(MLP_TILE, seq_len)
    assert seq_len % tile == 0 and tile % MLP_SUB_ROWS == 0
    tok = pl.BlockSpec((None, tile, D_MODEL), lambda b, i: (b, i, 0))
    row = pl.BlockSpec((1, D_MODEL), lambda b, i: (0, 0))
    whole = lambda w: pl.BlockSpec((None,) + w.shape[1:], lambda b, i: (0, 0, 0), pipeline_mode=pl.Buffered(1))
    mods_spec = pl.BlockSpec((None, 1, N_MOD * D_MODEL), lambda b, i: (mod_row(b), 0, 0))
    return pl.pallas_call(
        _proj_mlp_kernel,
        grid=(bsz, seq_len // tile),
        in_specs=[tok, tok, whole(w_o), row, mods_spec, row, row, whole(w_in), whole(w_out)],
        out_specs=tok,
        out_shape=jax.ShapeDtypeStruct(x.shape, f32),
        compiler_params=_params("arbitrary", "arbitrary"),
        name="proj_mlp_layer",
    )(x, attn_out, w_o, g_mix_post, mods, g_pre, g_post, w_in, w_out)


def _rope_tables(seq_len):
    half = HEAD_DIM // 2
    t = np.arange(seq_len)
    inv_freq = np.power(np.float32(ROPE_THETA), -np.arange(0, half, 2, dtype=np.float32) / np.float32(half))
    ang_r = (t // GRID_W).astype(np.float32)[:, None] * inv_freq
    ang_c = (t % GRID_W).astype(np.float32)[:, None] * inv_freq
    cos = np.concatenate([np.cos(ang_r), np.cos(ang_c)] * 2, axis=-1)
    sin = np.concatenate([-np.sin(ang_r), -np.sin(ang_c), np.sin(ang_r), np.sin(ang_c)], axis=-1)
    return jnp.asarray(cos, f32), jnp.asarray(sin, f32)


def _to_rope_lanes(a):
    quarters = a.reshape(a.shape[:-1] + (-1, 4, HEAD_DIM // 4))
    swapped = jnp.concatenate([quarters[..., 0:1, :], quarters[..., 2:3, :], quarters[..., 1:2, :],
                               quarters[..., 3:4, :]], axis=-2)
    return swapped.reshape(a.shape)


def _qkv_kernel(*refs, n_q, rope):
    x_ref, mods_ref, gpre_ref, w_ref, gq_ref, gk_ref = refs[:6]
    refs = refs[6:]
    if rope:
        cos_ref, sin_ref = refs[:2]
        refs = refs[2:]
    qk_ref, v_ref, h_buf, slab_buf = refs
    sh, sc, _ = _mod_slices(mods_ref, 0)
    pre_gain = gpre_ref[...] * (1.0 + sc)
    sub = slab_buf.shape[1]
    n_sub = x_ref.shape[0] // sub
    for r in range(n_sub):
        rows = slice(r * sub, (r + 1) * sub)
        h_buf[rows, :] = (_rms_scale(x_ref[rows, :]) * pre_gain + sh).astype(bf16)
    n_slabs = (n_q + N_KV_HEADS) // 2
    slab_cols = 2 * HEAD_DIM
    gq = gq_ref[...] * Q_SCALE
    same_head = (lax.broadcasted_iota(jnp.int32, (slab_cols, slab_cols), 0) // HEAD_DIM
                 == lax.broadcasted_iota(jnp.int32, (slab_cols, slab_cols), 1) // HEAD_DIM)
    head_ones = same_head.astype(bf16)
    ones_col = (lax.broadcasted_iota(jnp.int32, (sub, HEAD_DIM), 1) == 0).astype(bf16)

    for r in range(n_sub):
        rows = slice(r * sub, (r + 1) * sub)

        def project(i):
            return jnp.dot(h_buf[rows, :], w_ref[:, i * slab_cols:(i + 1) * slab_cols], preferred_element_type=f32)

        slab_buf[2 * r] = project(0)
        for i in range(n_slabs):
            slab_buf[2 * r + (i + 1) % 2] = project(i + 1)
            z = slab_buf[2 * r + i % 2]
            ssq = jnp.dot((z * z).astype(bf16), head_ones, preferred_element_type=f32)
            zn = z * lax.rsqrt(ssq * (1.0 / HEAD_DIM) + EPS)
            for j in range(2):
                head = 2 * i + j
                y = zn[:, j * HEAD_DIM:(j + 1) * HEAD_DIM] * (gq if head < n_q else gk_ref[...])
                if rope:
                    y = y * cos_ref[rows, :] + pltpu.roll(y, HEAD_DIM // 2, 1) * sin_ref[rows, :]
                qk_ref[head, rows, :] = y.astype(bf16)
        for j in range(N_KV_HEADS):
            v_ref[j, rows, 0:HEAD_DIM] = slab_buf[2 * r + n_slabs % 2, :, j * HEAD_DIM:(j + 1) * HEAD_DIM].astype(bf16)
            v_ref[j, rows, HEAD_DIM:] = ones_col


def _qkv_project(x, mods, mod_row, g_pre, w, w_col0, g_q, g_k, tile, n_q, rope):
    bsz, seq_len, _ = x.shape
    w_cols = (n_q + 2 * N_KV_HEADS) * HEAD_DIM
    sub = min(tile, ROW_TILE)
    assert N_KV_HEADS == 2 and n_q % 2 == 0 and w_col0 % w_cols == 0 and tile % sub == 0
    n_tiles = seq_len // tile
    row = lambda d: pl.BlockSpec((1, d), lambda i, b: (0, 0))
    heads = lambda n, d: pl.BlockSpec((None, n, tile, d), lambda i, b: (b, 0, i, 0))
    args = [x, mods, g_pre, w, g_q, g_k]
    specs = [
        pl.BlockSpec((None, tile, D_MODEL), lambda i, b: (b, i, 0)),
        pl.BlockSpec((None, 1, N_MOD * D_MODEL), lambda i, b: (mod_row(b), 0, 0)),
        row(D_MODEL),
        pl.BlockSpec((w.shape[0], w_cols), lambda i, b: (0, w_col0 // w_cols), pipeline_mode=pl.Buffered(1)),
        row(HEAD_DIM), row(HEAD_DIM),
    ]
    if rope:
        args += list(_rope_tables(seq_len))
        specs += [pl.BlockSpec((tile, HEAD_DIM), lambda i, b: (i, 0))] * 2
    n_qk = n_q + N_KV_HEADS
    return pl.pallas_call(
        functools.partial(_qkv_kernel, n_q=n_q, rope=rope),
        grid=(n_tiles, bsz),
        in_specs=specs,
        out_specs=[heads(n_qk, HEAD_DIM), heads(N_KV_HEADS, V_WIDTH)],
        out_shape=[jax.ShapeDtypeStruct((bsz, n_qk, seq_len, HEAD_DIM), bf16),
                   jax.ShapeDtypeStruct((bsz, N_KV_HEADS, seq_len, V_WIDTH), bf16)],
        scratch_shapes=[pltpu.VMEM((tile, D_MODEL), bf16), pltpu.VMEM((2 * (tile // sub), sub, 2 * HEAD_DIM), f32)],
        compiler_params=_params("arbitrary", "arbitrary"),
        name="qkv_project" if n_q else "kv_project",
    )(*args)


def _split_cast_refs(rest, n_casts):
    return rest[n_casts], rest[:n_casts], rest[n_casts + 1:]


def _attn_kernel(q_ref, kc_ref, vc_ref, kl_ref, vl_ref, *rest, tq, tk, n_casts):
    o_ref, cast_in, cast_out = _split_cast_refs(rest, n_casts)
    _cast_slabs(cast_in, cast_out)
    rows = Q_PER_KV * tq
    q = q_ref[...].reshape(rows, HEAD_DIM)

    def step(k, v, carry):
        m, l, acc = carry
        s = lax.dot_general(q, k, (((1,), (1,)), ((), ())), preferred_element_type=f32)
        m_new = jnp.maximum(m, jnp.max(s, axis=-1, keepdims=True))
        alpha = jnp.exp2(m - m_new)
        p = jnp.exp2(s - m_new)
        l = alpha * l + jnp.sum(p, axis=-1, keepdims=True)
        acc = alpha * acc + jnp.dot(p.astype(bf16), v, preferred_element_type=f32)
        return m_new, l, acc

    carry = (jnp.full((rows, 1), -jnp.inf, f32), jnp.zeros((rows, 1), f32), jnp.zeros((rows, HEAD_DIM), f32))
    carry = step(kc_ref[...], vc_ref[:, 0:HEAD_DIM], carry)

    def body(j, carry):
        off = pl.multiple_of(j * tk, tk)
        return step(kl_ref[pl.ds(off, tk), :], vl_ref[pl.ds(off, tk), 0:HEAD_DIM], carry)

    _, l, acc = lax.fori_loop(0, kl_ref.shape[0] // tk, body, carry)
    out = (acc / l).astype(bf16)
    for g in range(Q_PER_KV):
        o_ref[:, g * HEAD_DIM:(g + 1) * HEAD_DIM] = out[g * tq:(g + 1) * tq]


def _attn_bounded_kernel(q_ref, kc_ref, vc_ref, kl_ref, vl_ref, *rest, tq, tk, n_casts):
    o_ref, cast_in, cast_out = _split_cast_refs(rest, n_casts)
    _cast_slabs(cast_in, cast_out)
    chunks = [(k_ref, v_ref, slice(c0, min(c0 + tk, k_ref.shape[0])))
              for k_ref, v_ref in ((kc_ref, vc_ref), (kl_ref, vl_ref)) for c0 in range(0, k_ref.shape[0], tk)]
    accs = [None] * Q_PER_KV
    for k_ref, v_ref, keys in chunks:
        for g in range(Q_PER_KV):
            s = lax.dot_general(q_ref[g], k_ref[keys, :], (((1,), (1,)), ((), ())), preferred_element_type=f32)
            pv = jnp.dot(jnp.exp2(s).astype(bf16), v_ref[keys, :], preferred_element_type=f32)
            accs[g] = pv if accs[g] is None else accs[g] + pv
    for g, acc in enumerate(accs):
        o_ref[:, g * HEAD_DIM:(g + 1) * HEAD_DIM] = (acc[:, 0:HEAD_DIM] / acc[:, HEAD_DIM:HEAD_DIM + 1]).astype(bf16)


def _attention(qk_lat, k_ctx, v_ctx, v_lat, *cast_weights, bounded, cast_index):
    bsz, _, seq_len, _ = qk_lat.shape
    ctx_len = k_ctx.shape[2]
    kv = lambda n, d, h0=0: pl.BlockSpec((None, None, n, d), lambda b, h, i: (b, h0 + h, 0, 0))
    tq = ATTN_TQ_BOUNDED if bounded else ATTN_TQ
    n_tiles = seq_len // tq
    steps = bsz * N_KV_HEADS * n_tiles
    assert steps % ATTN_CAST_SLABS == 0
    slab_of_step = lambda b, h, i: ((b * N_KV_HEADS + h) * n_tiles + i) // (steps // ATTN_CAST_SLABS)
    cast_args, cast_in_specs, cast_out_specs, cast_shapes = _cast_specs(
        tuple(zip(cast_weights, cast_index)), ATTN_CAST_SLABS, slab_of_step)
    body = functools.partial(_attn_bounded_kernel if bounded else _attn_kernel, tq=tq,
                             tk=ATTN_TK_BOUNDED if bounded else ATTN_TK, n_casts=len(cast_args))
    return pl.pallas_call(
        body,
        grid=(bsz, N_KV_HEADS, n_tiles),
        in_specs=[
            pl.BlockSpec((None, Q_PER_KV, tq, HEAD_DIM), lambda b, h, i: (b, h, i, 0)),
            kv(ctx_len, HEAD_DIM), kv(ctx_len, V_WIDTH), kv(seq_len, HEAD_DIM, N_Q_HEADS), kv(seq_len, V_WIDTH),
        ] + cast_in_specs,
        out_specs=[pl.BlockSpec((None, tq, Q_PER_KV * HEAD_DIM), lambda b, h, i: (b, i, h))] + cast_out_specs,
        out_shape=[jax.ShapeDtypeStruct((bsz, seq_len, N_Q_HEADS * HEAD_DIM), bf16)] + cast_shapes,
        compiler_params=_params("arbitrary", "arbitrary", "arbitrary"),
        name="attention_bounded" if bounded else "attention",
    )(qk_lat, k_ctx, v_ctx, qk_lat, v_lat, *cast_args)


def kernel(x, c, ctx, c_ctx, w_ada, b_ada, g_mix_pre, g_mix_post, g_mlp_pre, g_mlp_post, w_pool, pool_scale,
           w_qkv, g_q, g_k, w_o, w_mlp_in, w_mlp_out):
    bsz, seq_len, d = x.shape
    ctx_len = ctx.shape[1]
    assert d == D_MODEL and seq_len % ROW_TILE == 0 and bsz + 1 <= MOD_ROWS
    assert ctx_len % POOL_HALO == 0 and (bsz * ctx_len) % ROW_TILE == 0
    row = lambda v: v.reshape(1, -1)

    cond = jnp.concatenate([c, c_ctx[None], jnp.zeros((MOD_ROWS - bsz - 1, D_MODEL), f32)], axis=0)
    mods = _ada_mods(cond, w_ada, b_ada).reshape(DEPTH, MOD_ROWS, 1, N_MOD * D_MODEL)
    lat_row = lambda b: b
    ctx_row = lambda b: bsz

    w_pool0 = w_pool[0].astype(bf16)
    layer0 = functools.partial(_pool_mlp_layer, mods=mods[0], g_mix_pre=row(g_mix_pre[0]),
                               g_mix_post=row(g_mix_post[0]), pool_scale=row(pool_scale[0]), w_pool=w_pool0,
                               g_mlp_pre=row(g_mlp_pre[0]), g_mlp_post=row(g_mlp_post[0]))
    ctx, w_in0, w_out0 = layer0(ctx, mod_row=ctx_row, w_in=w_mlp_in, w_out=w_mlp_out, convert=True)
    x = layer0(x, mod_row=lat_row, w_in=w_in0, w_out=w_out0)

    qk_cols = (N_Q_HEADS + N_KV_HEADS) * HEAD_DIM
    w_qkv1 = jnp.concatenate([_to_rope_lanes(w_qkv[0, :, :qk_cols]), w_qkv[0, :, qk_cols:]], axis=-1).astype(bf16)
    qkv = functools.partial(_qkv_project, mods=mods[1], g_pre=row(g_mix_pre[1]),
                            g_q=row(_to_rope_lanes(g_q[0])), g_k=row(_to_rope_lanes(g_k[0])))
    qk_lat, v_lat = qkv(x, mod_row=lat_row, w=w_qkv1, w_col0=0, tile=QKV_TILE, n_q=N_Q_HEADS, rope=True)
    k_ctx, v_ctx = qkv(ctx, mod_row=ctx_row, w=w_qkv1, w_col0=N_Q_HEADS * HEAD_DIM, tile=ctx_len, n_q=0, rope=False)
    score_bound = 1.02 * Q_SCALE * HEAD_DIM * jnp.max(jnp.abs(g_q[0])) * jnp.max(jnp.abs(g_k[0]))
    attn = functools.partial(_attention, cast_index=(1, 1, 0))
    attn_out, w_in1, w_out1, w_o1 = lax.cond(
        score_bound <= SCORE_LOG2_LIMIT, functools.partial(attn, bounded=True), functools.partial(attn, bounded=False),
        qk_lat, k_ctx, v_ctx, v_lat, w_mlp_in, w_mlp_out, w_o)
    return _proj_mlp_layer(x, attn_out, w_o1, row(g_mix_post[1]), mods[1], lat_row,
                           row(g_mlp_pre[1]), row(g_mlp_post[1]), w_in1, w_out1)
```

```python
import functools
import math

import numpy as np
import jax
import jax.numpy as jnp
from jax import lax
from jax.experimental import pallas as pl
from jax.experimental.pallas import tpu as pltpu

D_MODEL = 1024
DEPTH = 2
GRID_W = 64
POOL_WINDOWS = (2, 4, 8, 16)
POOL_GROUP_DIM = D_MODEL // len(POOL_WINDOWS)
POOL_HALO = 8
HEAD_DIM = 128
N_Q_HEADS = D_MODEL // HEAD_DIM
N_KV_HEADS = 2
V_WIDTH = 2 * HEAD_DIM
Q_PER_KV = N_Q_HEADS // N_KV_HEADS
ROPE_THETA = 10000.0
D_FF = 4 * D_MODEL
N_MOD = 6
EPS = 1e-6
MOD_ROWS = 8
Q_SCALE = (HEAD_DIM ** -0.5) * math.log2(math.e)

ROW_TILE = 512
QKV_TILE = 1024
MLP_TILE = 1024
MLP_SUB_ROWS = 512
ATTN_TQ = 128
ATTN_TK = 512
ATTN_TQ_BOUNDED = 512
ATTN_CAST_SLABS = 64
ATTN_TK_BOUNDED = 256
SCORE_LOG2_LIMIT = 64.0
FF_CHUNK = 1024
LAYER0_FF_CHUNK = 512
CAST_CHUNKS = 8
ADA_TN = 1536
VMEM_LIMIT = 56 * 1024 * 1024

f32 = jnp.float32
bf16 = jnp.bfloat16


def _params(*semantics, fuse_inputs=None):
    return pltpu.CompilerParams(dimension_semantics=semantics, vmem_limit_bytes=VMEM_LIMIT,
                                allow_input_fusion=fuse_inputs)


def _rms_scale(x):
    return x * lax.rsqrt(jnp.mean(x * x, axis=-1, keepdims=True) + EPS)


def _sq_relu(u):
    ub = jnp.maximum(u.astype(bf16), 0.0)
    return ub * ub


def _cast_slabs(src_refs, dst_refs):
    for src, dst in zip(src_refs, dst_refs):
        dst[...] = src[...].astype(bf16)


def _cast_specs(casts, n_blocks, block_of_step):
    args, in_specs, out_specs, out_shape = [], [], [], []
    for weight, index in casts:
        _, rows, cols = weight.shape
        assert rows % (n_blocks * 16) == 0
        slab = (None, rows // n_blocks, cols)
        args.append(weight)
        in_specs.append(pl.BlockSpec(slab, lambda *ids, l=index: (l, block_of_step(*ids), 0)))
        out_specs.append(pl.BlockSpec(slab, lambda *ids: (0, block_of_step(*ids), 0)))
        out_shape.append(jax.ShapeDtypeStruct((1, rows, cols), bf16))
    return args, in_specs, out_specs, out_shape


def _mod_slices(mods_ref, first):
    return [mods_ref[:, (first + j) * D_MODEL:(first + j + 1) * D_MODEL] for j in range(3)]


def _ada_kernel(c_ref, w_ref, b_ref, o_ref):
    c = c_ref[...]
    s = c * jax.nn.sigmoid(c)
    o_ref[...] = jnp.dot(s.astype(bf16), w_ref[...].astype(bf16), preferred_element_type=f32) + b_ref[...]


def _ada_mods(cond, w_ada, b_ada):
    n = N_MOD * D_MODEL
    return pl.pallas_call(
        _ada_kernel,
        grid=(DEPTH, n // ADA_TN),
        in_specs=[
            pl.BlockSpec((MOD_ROWS, D_MODEL), lambda i, j: (0, 0)),
            pl.BlockSpec((None, D_MODEL, ADA_TN), lambda i, j: (i, 0, j)),
            pl.BlockSpec((None, 1, ADA_TN), lambda i, j: (i, 0, j)),
        ],
        out_specs=pl.BlockSpec((None, MOD_ROWS, ADA_TN), lambda i, j: (i, 0, j)),
        out_shape=jax.ShapeDtypeStruct((DEPTH, MOD_ROWS, n), f32),
        compiler_params=_params("arbitrary", "arbitrary"),
        name="ada_mods",
    )(cond, w_ada, b_ada.reshape(DEPTH, 1, n))


def _pool_mlp_kernel(xp_ref, x_ref, xn_ref, mods_ref, prev_mods_ref, gmix_pre_ref, gmix_post_ref, ps_ref, wpool_ref,
                     gpre_ref, gpost_ref, win_ref, wout_ref, *rest, tile, seq_len, convert):
    if convert:
        (o_ref, win_hbm_out, wout_hbm_out, hbuf, x1_buf, h_buf, lv0, lv1, lv2,
         win_vm, wout_vm, win_stage, wout_stage, in_sems, out_sems) = rest
        level_bufs = (lv0, lv1, lv2)
        win_hbm, wout_hbm, win_ref, wout_ref = win_ref, wout_ref, win_vm, wout_vm
    else:
        o_ref, hbuf, x1_buf, h_buf, *level_bufs = rest
    s = pl.program_id(0)

    def hbm_copies(src_hbm, stage, k):
        rows = stage.shape[1]
        return [pltpu.make_async_copy(src_hbm.at[0, pl.ds(c * rows, rows), :], stage.at[c % 2], in_sems.at[k, c % 2])
                for c in range(src_hbm.shape[1] // rows)]

    def out_copies():
        return [pltpu.make_async_copy(win_vm, win_hbm_out.at[0], out_sems.at[0]),
                pltpu.make_async_copy(wout_vm, wout_hbm_out.at[0], out_sems.at[1])]

    def convert_w_in():
        copies = hbm_copies(win_hbm, win_stage, 0)
        rows = win_stage.shape[1]
        copies[0].start()
        for c, cp in enumerate(copies):
            if c + 1 < len(copies):
                copies[c + 1].start()
            cp.wait()
            win_vm[c * rows:(c + 1) * rows, :] = win_stage[c % 2].astype(bf16)
        out_copies()[0].start()
        hbm_copies(wout_hbm, wout_stage, 1)[0].start()

    def convert_w_out_chunk(c):
        copies = hbm_copies(wout_hbm, wout_stage, 1)
        rows = wout_stage.shape[1]
        if c + 1 < len(copies):
            copies[c + 1].start()
        copies[c].wait()
        wout_vm[c * rows:(c + 1) * rows, :] = wout_stage[c % 2].astype(bf16)
    n_tiles = seq_len // tile
    i = jnp.minimum(s, pl.num_programs(0) - 2) % n_tiles

    def mlp_chunk(c, acc, before_down=None):
        cols = slice(c * LAYER0_FF_CHUNK, (c + 1) * LAYER0_FF_CHUNK)
        u = jnp.dot(h_buf[...], win_ref[:, cols], preferred_element_type=f32)
        if before_down is not None:
            before_down(c)
        part = jnp.dot(_sq_relu(u), wout_ref[cols, :], preferred_element_type=f32)
        return part if acc is None else acc + part

    sh1, sc1, gt1 = _mod_slices(mods_ref, 0)
    sh2, sc2, _ = _mod_slices(mods_ref, 3)
    mix_in_gain = gmix_pre_ref[...] * (1.0 + sc1)

    def hmod(xv):
        return _rms_scale(xv) * mix_in_gain + sh1

    ext = tile + 2 * POOL_HALO
    n_groups = len(POOL_WINDOWS)

    def zero_level_padding():
        for buf in (hbuf,) + tuple(level_bufs):
            buf[ext:, :] = jnp.zeros((POOL_HALO, buf.shape[1]), f32)

    def build_level(k):
        src = hbuf if k == 1 else level_bufs[k - 2]
        lane0 = 0 if k == 1 else POOL_GROUP_DIM
        rows = ext if k < n_groups else tile
        summed = src[0:rows, lane0:] + src[2 ** (k - 1):2 ** (k - 1) + rows, lane0:]
        if k == n_groups:
            return summed
        level_bufs[k - 1][0:ext, :] = summed

    def pool_group(g, top_level=None):
        w = POOL_WINDOWS[g]
        cols = slice(g * POOL_GROUP_DIM, (g + 1) * POOL_GROUP_DIM)
        acc = (top_level if g == n_groups - 1 else
               level_bufs[g][POOL_HALO - w // 2:POOL_HALO - w // 2 + tile, 0:POOL_GROUP_DIM])
        def clipped_mean(r0):
            t = i * tile + r0 + lax.broadcasted_iota(jnp.int32, (POOL_HALO, 1), 0)
            cnt = jnp.minimum(t + (w - w // 2), seq_len) - jnp.maximum(t - w // 2, 0)
            return acc[r0:r0 + POOL_HALO] / cnt.astype(f32)

        mean = jnp.concatenate([clipped_mean(0), acc[POOL_HALO:tile - POOL_HALO] * (1.0 / w),
                                clipped_mean(tile - POOL_HALO)], axis=0)
        diff = mean - hbuf[POOL_HALO:POOL_HALO + tile, cols]
        return jnp.dot(diff.astype(bf16), wpool_ref[g], preferred_element_type=f32) * ps_ref[:, cols]

    half = tile // 2

    def fill_top():
        hbuf[POOL_HALO:POOL_HALO + half, :] = hmod(x_ref[0:half, :])
        hbuf[0:POOL_HALO, :] = jnp.where(i > 0, hmod(xp_ref[...]), 0.0)

    def fill_bottom():
        hbuf[POOL_HALO + half:POOL_HALO + tile, :] = hmod(x_ref[half:tile, :])
        hbuf[POOL_HALO + tile:ext, :] = jnp.where(i < n_tiles - 1, hmod(xn_ref[...]), 0.0)

    def step(with_mlp, with_pool):
        acc, ys = None, []
        vpu_pieces = [fill_top, fill_bottom, lambda: build_level(1),
                      lambda: (build_level(2), ys.append(pool_group(0))),
                      lambda: (ys.append(pool_group(1)), build_level(3)),
                      lambda: ys.append(pool_group(2)),
                      lambda: ys.append(pool_group(3, build_level(4)))]
        if not with_mlp:
            zero_level_padding()
            if convert:
                convert_w_in()
        convert_here = convert and with_mlp and with_pool
        for c in range(D_FF // LAYER0_FF_CHUNK):
            if with_mlp:
                acc = mlp_chunk(c, acc, convert_w_out_chunk if convert_here else None)
            if with_pool and c < len(vpu_pieces):
                vpu_pieces[c]()
        if convert_here:
            out_copies()[1].start()
        if with_pool:
            x1 = x_ref[...] + _rms_scale(jnp.concatenate(ys, axis=-1)) * (gt1 * gmix_post_ref[...])
            h = (_rms_scale(x1) * (gpre_ref[...] * (1.0 + sc2)) + sh2).astype(bf16)
        if with_mlp:
            gt2_prev = prev_mods_ref[:, 5 * D_MODEL:6 * D_MODEL]
            o_ref[...] = x1_buf[...] + _rms_scale(acc) * (gt2_prev * gpost_ref[...])
        if with_pool:
            x1_buf[...] = x1
            h_buf[...] = h
        elif convert:
            for cp in out_copies():
                cp.wait()

    last = pl.num_programs(0) - 1
    pl.when(s == 0)(functools.partial(step, False, True))
    pl.when(jnp.logical_and(s > 0, s < last))(functools.partial(step, True, True))
    pl.when(s == last)(functools.partial(step, True, False))


def _pool_mlp_layer(x, mods, mod_row, g_mix_pre, g_mix_post, pool_scale, w_pool, g_mlp_pre, g_mlp_post,
                    w_in, w_out, convert=False):
    bsz, seq_len, _ = x.shape
    tile = min(ROW_TILE, seq_len)
    n_groups = len(POOL_WINDOWS)
    assert seq_len % tile == 0 and tile % POOL_HALO == 0 and D_FF // LAYER0_FF_CHUNK >= 7 and n_groups == 4
    assert POOL_WINDOWS == tuple(2 ** (g + 1) for g in range(n_groups)) and POOL_WINDOWS[-1] == 2 * POOL_HALO
    n_tiles = seq_len // tile
    total = bsz * n_tiles
    hb = tile // POOL_HALO
    last_halo = seq_len // POOL_HALO - 1
    cur = lambda s: jnp.minimum(s, total - 1)
    prev = lambda s: jnp.maximum(s - 1, 0)
    row = pl.BlockSpec((1, D_MODEL), lambda s: (0, 0))
    mods_spec = lambda step: pl.BlockSpec((None, 1, N_MOD * D_MODEL), lambda s: (mod_row(step(s) // n_tiles), 0, 0))
    tok = lambda step: pl.BlockSpec((None, tile, D_MODEL), lambda s: (step(s) // n_tiles, step(s) % n_tiles, 0))
    whole = lambda w: pl.BlockSpec((None,) + w.shape[1:], lambda s: (0, 0, 0), pipeline_mode=pl.Buffered(1))
    in_hbm = pl.BlockSpec(memory_space=pl.ANY)
    out_specs, out_shape = tok(prev), jax.ShapeDtypeStruct(x.shape, f32)
    scratch = ([pltpu.VMEM((tile + 3 * POOL_HALO, D_MODEL), f32),
                pltpu.VMEM((tile, D_MODEL), f32), pltpu.VMEM((tile, D_MODEL), bf16)]
               + [pltpu.VMEM((tile + 3 * POOL_HALO, D_MODEL - k * POOL_GROUP_DIM), f32) for k in range(n_groups - 1)])
    if convert:
        assert w_in.dtype == f32 and total == 2 and D_MODEL % CAST_CHUNKS == 0 and D_FF // CAST_CHUNKS == LAYER0_FF_CHUNK
        out_specs = [out_specs, in_hbm, in_hbm]
        out_shape = [out_shape, jax.ShapeDtypeStruct((1,) + w_in.shape[1:], bf16),
                     jax.ShapeDtypeStruct((1,) + w_out.shape[1:], bf16)]
        scratch += [pltpu.VMEM(w_in.shape[1:], bf16), pltpu.VMEM(w_out.shape[1:], bf16),
                    pltpu.VMEM((2, D_MODEL // CAST_CHUNKS, D_FF), f32), pltpu.VMEM((2, D_FF // CAST_CHUNKS, D_MODEL), f32),
                    pltpu.SemaphoreType.DMA((2, 2)), pltpu.SemaphoreType.DMA((2,))]
    return pl.pallas_call(
        functools.partial(_pool_mlp_kernel, tile=tile, seq_len=seq_len, convert=convert),
        grid=(total + 1,),
        in_specs=[
            pl.BlockSpec((None, POOL_HALO, D_MODEL),
                         lambda s: (cur(s) // n_tiles, jnp.maximum(cur(s) % n_tiles * hb - 1, 0), 0)),
            tok(cur),
            pl.BlockSpec((None, POOL_HALO, D_MODEL),
                         lambda s: (cur(s) // n_tiles, jnp.minimum((cur(s) % n_tiles + 1) * hb, last_halo), 0)),
            mods_spec(cur), mods_spec(prev),
            row, row, row,
            pl.BlockSpec(w_pool.shape, lambda s: (0, 0, 0)),
            row, row,
        ] + ([in_hbm, in_hbm] if convert else [whole(w_in), whole(w_out)]),
        out_specs=out_specs,
        out_shape=out_shape,
        scratch_shapes=scratch,
        compiler_params=_params("arbitrary"),
        name="pool_mlp_layer_convert" if convert else "pool_mlp_layer",
    )(x, x, x, mods, mods, g_mix_pre, g_mix_post, pool_scale, w_pool, g_mlp_pre, g_mlp_post, w_in, w_out)


def _proj_mlp_kernel(x_ref, a_ref, wo_ref, gmix_ref, mods_ref, gpre_ref, gpost_ref, win_ref, wout_ref, o_ref):
    sh, sc, gt = _mod_slices(mods_ref, 3)
    pre_gain = gpre_ref[...] * (1.0 + sc)
    post_gain = gt * gpost_ref[...]
    mix_gain = mods_ref[:, 2 * D_MODEL:3 * D_MODEL] * gmix_ref[...]
    n_sub = x_ref.shape[0] // MLP_SUB_ROWS
    rows = [slice(r * MLP_SUB_ROWS, (r + 1) * MLP_SUB_ROWS) for r in range(n_sub)]

    def pre(r, y):
        x = x_ref[rows[r], :] + _rms_scale(y) * mix_gain
        return x, (_rms_scale(x) * pre_gain + sh).astype(bf16)

    def mlp_chunk(h, c, acc):
        cols = slice(c * FF_CHUNK, (c + 1) * FF_CHUNK)
        u = jnp.dot(h, win_ref[:, cols], preferred_element_type=f32)
        part = jnp.dot(_sq_relu(u), wout_ref[cols, :], preferred_element_type=f32)
        return part if acc is None else acc + part

    def post(r, x, acc):
        o_ref[rows[r], :] = x + _rms_scale(acc) * post_gain

    ys = [jnp.dot(a_ref[rows[r], :], wo_ref[...], preferred_element_type=f32) for r in range(n_sub)]
    cur = pre(0, ys[0])
    done = None
    for r in range(n_sub):
        x, h = cur
        acc = mlp_chunk(h, 0, None)
        if r + 1 < n_sub:
            cur = pre(r + 1, ys[r + 1])
        if done is not None:
            post(*done)
        for c in range(1, D_FF // FF_CHUNK):
            acc = mlp_chunk(h, c, acc)
        done = (r, x, acc)
    post(*done)


def _proj_mlp_layer(x, attn_out, w_o, g_mix_post, mods, mod_row, g_pre, g_post, w_in, w_out):
    bsz, seq_len, _ = x.shape
    tile = min(MLP_TILE, seq_len)
    assert seq_len % tile == 0 and tile % MLP_SUB_ROWS == 0
    tok = pl.BlockSpec((None, tile, D_MODEL), lambda b, i: (b, i, 0))
    row = pl.BlockSpec((1, D_MODEL), lambda b, i: (0, 0))
    whole = lambda w: pl.BlockSpec((None,) + w.shape[1:], lambda b, i: (0, 0, 0), pipeline_mode=pl.Buffered(1))
    mods_spec = pl.BlockSpec((None, 1, N_MOD * D_MODEL), lambda b, i: (mod_row(b), 0, 0))
    return pl.pallas_call(
        _proj_mlp_kernel,
        grid=(bsz, seq_len // tile),
        in_specs=[tok, tok, whole(w_o), row, mods_spec, row, row, whole(w_in), whole(w_out)],
        out_specs=tok,
        out_shape=jax.ShapeDtypeStruct(x.shape, f32),
        compiler_params=_params("arbitrary", "arbitrary"),
        name="proj_mlp_layer",
    )(x, attn_out, w_o, g_mix_post, mods, g_pre, g_post, w_in, w_out)


def _rope_tables(seq_len):
    half = HEAD_DIM // 2
    t = np.arange(seq_len)
    inv_freq = np.power(np.float32(ROPE_THETA), -np.arange(0, half, 2, dtype=np.float32) / np.float32(half))
    ang_r = (t // GRID_W).astype(np.float32)[:, None] * inv_freq
    ang_c = (t % GRID_W).astype(np.float32)[:, None] * inv_freq
    cos = np.concatenate([np.cos(ang_r), np.cos(ang_c)] * 2, axis=-1)
    sin = np.concatenate([-np.sin(ang_r), -np.sin(ang_c), np.sin(ang_r), np.sin(ang_c)], axis=-1)
    return jnp.asarray(cos, f32), jnp.asarray(sin, f32)


def _to_rope_lanes(a):
    quarters = a.reshape(a.shape[:-1] + (-1, 4, HEAD_DIM // 4))
    swapped = jnp.concatenate([quarters[..., 0:1, :], quarters[..., 2:3, :], quarters[..., 1:2, :],
                               quarters[..., 3:4, :]], axis=-2)
    return swapped.reshape(a.shape)


def _qkv_kernel(*refs, n_q, rope):
    x_ref, mods_ref, gpre_ref, w_ref, gq_ref, gk_ref = refs[:6]
    refs = refs[6:]
    if rope:
        cos_ref, sin_ref = refs[:2]
        refs = refs[2:]
    qk_ref, v_ref, h_buf, slab_buf = refs
    sh, sc, _ = _mod_slices(mods_ref, 0)
    pre_gain = gpre_ref[...] * (1.0 + sc)
    sub = slab_buf.shape[1]
    n_sub = x_ref.shape[0] // sub
    for r in range(n_sub):
        rows = slice(r * sub, (r + 1) * sub)
        h_buf[rows, :] = (_rms_scale(x_ref[rows, :]) * pre_gain + sh).astype(bf16)
    n_slabs = (n_q + N_KV_HEADS) // 2
    slab_cols = 2 * HEAD_DIM
    gq = gq_ref[...] * Q_SCALE
    same_head = (lax.broadcasted_iota(jnp.int32, (slab_cols, slab_cols), 0) // HEAD_DIM
                 == lax.broadcasted_iota(jnp.int32, (slab_cols, slab_cols), 1) // HEAD_DIM)
    head_ones = same_head.astype(bf16)
    ones_col = (lax.broadcasted_iota(jnp.int32, (sub, HEAD_DIM), 1) == 0).astype(bf16)

    for r in range(n_sub):
        rows = slice(r * sub, (r + 1) * sub)

        def project(i):
            return jnp.dot(h_buf[rows, :], w_ref[:, i * slab_cols:(i + 1) * slab_cols], preferred_element_type=f32)

        slab_buf[2 * r] = project(0)
        for i in range(n_slabs):
            slab_buf[2 * r + (i + 1) % 2] = project(i + 1)
            z = slab_buf[2 * r + i % 2]
            ssq = jnp.dot((z * z).astype(bf16), head_ones, preferred_element_type=f32)
            zn = z * lax.rsqrt(ssq * (1.0 / HEAD_DIM) + EPS)
            for j in range(2):
                head = 2 * i + j
                y = zn[:, j * HEAD_DIM:(j + 1) * HEAD_DIM] * (gq if head < n_q else gk_ref[...])
                if rope:
                    y = y * cos_ref[rows, :] + pltpu.roll(y, HEAD_DIM // 2, 1) * sin_ref[rows, :]
                qk_ref[head, rows, :] = y.astype(bf16)
        for j in range(N_KV_HEADS):
            v_ref[j, rows, 0:HEAD_DIM] = slab_buf[2 * r + n_slabs % 2, :, j * HEAD_DIM:(j + 1) * HEAD_DIM].astype(bf16)
            v_ref[j, rows, HEAD_DIM:] = ones_col


def _qkv_project(x, mods, mod_row, g_pre, w, w_col0, g_q, g_k, tile, n_q, rope):
    bsz, seq_len, _ = x.shape
    w_cols = (n_q + 2 * N_KV_HEADS) * HEAD_DIM
    sub = min(tile, ROW_TILE)
    assert N_KV_HEADS == 2 and n_q % 2 == 0 and w_col0 % w_cols == 0 and tile % sub == 0
    n_tiles = seq_len // tile
    row = lambda d: pl.BlockSpec((1, d), lambda i, b: (0, 0))
    heads = lambda n, d: pl.BlockSpec((None, n, tile, d), lambda i, b: (b, 0, i, 0))
    args = [x, mods, g_pre, w, g_q, g_k]
    specs = [
        pl.BlockSpec((None, tile, D_MODEL), lambda i, b: (b, i, 0)),
        pl.BlockSpec((None, 1, N_MOD * D_MODEL), lambda i, b: (mod_row(b), 0, 0)),
        row(D_MODEL),
        pl.BlockSpec((w.shape[0], w_cols), lambda i, b: (0, w_col0 // w_cols), pipeline_mode=pl.Buffered(1)),
        row(HEAD_DIM), row(HEAD_DIM),
    ]
    if rope:
        args += list(_rope_tables(seq_len))
        specs += [pl.BlockSpec((tile, HEAD_DIM), lambda i, b: (i, 0))] * 2
    n_qk = n_q + N_KV_HEADS
    return pl.pallas_call(
        functools.partial(_qkv_kernel, n_q=n_q, rope=rope),
        grid=(n_tiles, bsz),
        in_specs=specs,
        out_specs=[heads(n_qk, HEAD_DIM), heads(N_KV_HEADS, V_WIDTH)],
        out_shape=[jax.ShapeDtypeStruct((bsz, n_qk, seq_len, HEAD_DIM), bf16),
                   jax.ShapeDtypeStruct((bsz, N_KV_HEADS, seq_len, V_WIDTH), bf16)],
        scratch_shapes=[pltpu.VMEM((tile, D_MODEL), bf16), pltpu.VMEM((2 * (tile // sub), sub, 2 * HEAD_DIM), f32)],
        compiler_params=_params("arbitrary", "arbitrary", fuse_inputs=[j == 3 for j in range(len(args))]),
        name="qkv_project" if n_q else "kv_project",
    )(*args)


def _split_cast_refs(rest, n_casts):
    return rest[n_casts], rest[:n_casts], rest[n_casts + 1:]


def _attn_kernel(q_ref, kc_ref, vc_ref, kl_ref, vl_ref, *rest, tq, tk, n_casts):
    o_ref, cast_in, cast_out = _split_cast_refs(rest, n_casts)
    _cast_slabs(cast_in, cast_out)
    rows = Q_PER_KV * tq
    q = q_ref[...].reshape(rows, HEAD_DIM)

    def step(k, v, carry):
        m, l, acc = carry
        s = lax.dot_general(q, k, (((1,), (1,)), ((), ())), preferred_element_type=f32)
        m_new = jnp.maximum(m, jnp.max(s, axis=-1, keepdims=True))
        alpha = jnp.exp2(m - m_new)
        p = jnp.exp2(s - m_new)
        l = alpha * l + jnp.sum(p, axis=-1, keepdims=True)
        acc = alpha * acc + jnp.dot(p.astype(bf16), v, preferred_element_type=f32)
        return m_new, l, acc

    carry = (jnp.full((rows, 1), -jnp.inf, f32), jnp.zeros((rows, 1), f32), jnp.zeros((rows, HEAD_DIM), f32))
    carry = step(kc_ref[...], vc_ref[:, 0:HEAD_DIM], carry)

    def body(j, carry):
        off = pl.multiple_of(j * tk, tk)
        return step(kl_ref[pl.ds(off, tk), :], vl_ref[pl.ds(off, tk), 0:HEAD_DIM], carry)

    _, l, acc = lax.fori_loop(0, kl_ref.shape[0] // tk, body, carry)
    out = (acc / l).astype(bf16)
    for g in range(Q_PER_KV):
        o_ref[:, g * HEAD_DIM:(g + 1) * HEAD_DIM] = out[g * tq:(g + 1) * tq]


def _attn_bounded_kernel(q_ref, kc_ref, vc_ref, kl_ref, vl_ref, *rest, tq, tk, n_casts):
    o_ref, cast_in, cast_out = _split_cast_refs(rest, n_casts)
    _cast_slabs(cast_in, cast_out)
    chunks = [(k_ref, v_ref, slice(c0, min(c0 + tk, k_ref.shape[0])))
              for k_ref, v_ref in ((kc_ref, vc_ref), (kl_ref, vl_ref)) for c0 in range(0, k_ref.shape[0], tk)]
    accs = [None] * Q_PER_KV
    for k_ref, v_ref, keys in chunks:
        for g in range(Q_PER_KV):
            s = lax.dot_general(q_ref[g], k_ref[keys, :], (((1,), (1,)), ((), ())), preferred_element_type=f32)
            pv = jnp.dot(jnp.exp2(s).astype(bf16), v_ref[keys, :], preferred_element_type=f32)
            accs[g] = pv if accs[g] is None else accs[g] + pv
    for g, acc in enumerate(accs):
        o_ref[:, g * HEAD_DIM:(g + 1) * HEAD_DIM] = (acc[:, 0:HEAD_DIM] / acc[:, HEAD_DIM:HEAD_DIM + 1]).astype(bf16)


def _attention(qk_lat, k_ctx, v_ctx, v_lat, *cast_weights, bounded, cast_index):
    bsz, _, seq_len, _ = qk_lat.shape
    ctx_len = k_ctx.shape[2]
    kv = lambda n, d, h0=0: pl.BlockSpec((None, None, n, d), lambda b, h, i: (b, h0 + h, 0, 0))
    tq = ATTN_TQ_BOUNDED if bounded else ATTN_TQ
    n_tiles = seq_len // tq
    steps = bsz * N_KV_HEADS * n_tiles
    assert steps % ATTN_CAST_SLABS == 0
    slab_of_step = lambda b, h, i: ((b * N_KV_HEADS + h) * n_tiles + i) // (steps // ATTN_CAST_SLABS)
    cast_args, cast_in_specs, cast_out_specs, cast_shapes = _cast_specs(
        tuple(zip(cast_weights, cast_index)), ATTN_CAST_SLABS, slab_of_step)
    body = functools.partial(_attn_bounded_kernel if bounded else _attn_kernel, tq=tq,
                             tk=ATTN_TK_BOUNDED if bounded else ATTN_TK, n_casts=len(cast_args))
    return pl.pallas_call(
        body,
        grid=(bsz, N_KV_HEADS, n_tiles),
        in_specs=[
            pl.BlockSpec((None, Q_PER_KV, tq, HEAD_DIM), lambda b, h, i: (b, h, i, 0)),
            kv(ctx_len, HEAD_DIM), kv(ctx_len, V_WIDTH), kv(seq_len, HEAD_DIM, N_Q_HEADS), kv(seq_len, V_WIDTH),
        ] + cast_in_specs,
        out_specs=[pl.BlockSpec((None, tq, Q_PER_KV * HEAD_DIM), lambda b, h, i: (b, i, h))] + cast_out_specs,
        out_shape=[jax.ShapeDtypeStruct((bsz, seq_len, N_Q_HEADS * HEAD_DIM), bf16)] + cast_shapes,
        compiler_params=_params("arbitrary", "arbitrary", "arbitrary"),
        name="attention_bounded" if bounded else "attention",
    )(qk_lat, k_ctx, v_ctx, qk_lat, v_lat, *cast_args)


def kernel(x, c, ctx, c_ctx, w_ada, b_ada, g_mix_pre, g_mix_post, g_mlp_pre, g_mlp_post, w_pool, pool_scale,
           w_qkv, g_q, g_k, w_o, w_mlp_in, w_mlp_out):
    bsz, seq_len, d = x.shape
    ctx_len = ctx.shape[1]
    assert d == D_MODEL and seq_len % ROW_TILE == 0 and bsz + 1 <= MOD_ROWS
    assert ctx_len % POOL_HALO == 0 and (bsz * ctx_len) % ROW_TILE == 0
    row = lambda v: v.reshape(1, -1)

    cond = jnp.concatenate([c, c_ctx[None], jnp.zeros((MOD_ROWS - bsz - 1, D_MODEL), f32)], axis=0)
    mods = _ada_mods(cond, w_ada, b_ada).reshape(DEPTH, MOD_ROWS, 1, N_MOD * D_MODEL)
    lat_row = lambda b: b
    ctx_row = lambda b: bsz

    w_pool0 = w_pool[0].astype(bf16)
    layer0 = functools.partial(_pool_mlp_layer, mods=mods[0], g_mix_pre=row(g_mix_pre[0]),
                               g_mix_post=row(g_mix_post[0]), pool_scale=row(pool_scale[0]), w_pool=w_pool0,
                               g_mlp_pre=row(g_mlp_pre[0]), g_mlp_post=row(g_mlp_post[0]))
    ctx, w_in0, w_out0 = layer0(ctx, mod_row=ctx_row, w_in=w_mlp_in, w_out=w_mlp_out, convert=True)
    x = layer0(x, mod_row=lat_row, w_in=w_in0, w_out=w_out0)

    qk_cols = (N_Q_HEADS + N_KV_HEADS) * HEAD_DIM
    w_qkv1 = jnp.concatenate([_to_rope_lanes(w_qkv[0, :, :qk_cols]), w_qkv[0, :, qk_cols:]], axis=-1).astype(bf16)
    qkv = functools.partial(_qkv_project, mods=mods[1], g_pre=row(g_mix_pre[1]),
                            g_q=row(_to_rope_lanes(g_q[0])), g_k=row(_to_rope_lanes(g_k[0])))
    qk_lat, v_lat = qkv(x, mod_row=lat_row, w=w_qkv1, w_col0=0, tile=QKV_TILE, n_q=N_Q_HEADS, rope=True)
    k_ctx, v_ctx = qkv(ctx, mod_row=ctx_row, w=w_qkv1, w_col0=N_Q_HEADS * HEAD_DIM, tile=ctx_len, n_q=0, rope=False)
    score_bound = 1.02 * Q_SCALE * HEAD_DIM * jnp.max(jnp.abs(g_q[0])) * jnp.max(jnp.abs(g_k[0]))
    attn = functools.partial(_attention, cast_index=(1, 1, 0))
    attn_out, w_in1, w_out1, w_o1 = lax.cond(
        score_bound <= SCORE_LOG2_LIMIT, functools.partial(attn, bounded=True), functools.partial(attn, bounded=False),
        qk_lat, k_ctx, v_ctx, v_lat, w_mlp_in, w_mlp_out, w_o)
    return _proj_mlp_layer(x, attn_out, w_o1, row(g_mix_post[1]), mods[1], lat_row,
                           row(g_mlp_pre[1]), row(g_mlp_post[1]), w_in1, w_out1)
```
